```python
import jax, jax.numpy as jnp
from jax import lax
import numpy as np

D_MODEL = 2048
BATCH = 2
SEQ = 4096
DEPTH = 1
DEC_BATCH = 4
DEC_SEQ = 4096
PAST_LEN = 128

MLA_HEADS = 16
MLA_Q_RANK = 512
MLA_KV_RANK = 512
MLA_NOPE = 128
MLA_ROPE = 64
MLA_V = 128
Q_BLOCK = 128
SWA_Q_HEADS = 32
SWA_KV_HEADS = 8
SWA_HEAD_DIM = 64
SWA_GROUPS = SWA_Q_HEADS // SWA_KV_HEADS
WINDOW = 128
N_EXPERTS = 32
TOP_K = 4
D_FF = 2048
SWIGLU_LIMIT = 7.0
SWIGLU_ALPHA = 1.702
EXPERT_BLOCK = 256
ROPE_THETA = 10000.0
RMS_EPS = 1e-6
N_MOD = 6

IN_SPLIT_SIZES = (MLA_Q_RANK, MLA_KV_RANK, MLA_ROPE, SWA_Q_HEADS * SWA_HEAD_DIM, SWA_KV_HEADS * SWA_HEAD_DIM, SWA_KV_HEADS * SWA_HEAD_DIM, D_MODEL, D_MODEL)
IN_COLS = sum(IN_SPLIT_SIZES)
IN_SPLIT_POINTS = tuple(int(v) for v in np.cumsum(IN_SPLIT_SIZES)[:-1])

kernel_name = "hybrid_mla_swa_moe_encoder"


def rmsnorm(x, g):
    xf = x.astype(jnp.float32)
    xf = xf * lax.rsqrt(jnp.mean(xf * xf, axis=-1, keepdims=True) + RMS_EPS)
    return (xf * g.astype(jnp.float32)).astype(x.dtype)


def rope(x, pos):
    d = x.shape[-1]
    half = d // 2
    freqs = jnp.power(ROPE_THETA, -2.0 * jnp.arange(half, dtype=jnp.float32) / d)
    ang = pos[:, None] * freqs[None, :]
    cos = jnp.cos(ang)[:, None, :]
    sin = jnp.sin(ang)[:, None, :]
    xf = x.astype(jnp.float32)
    x1, x2 = xf[..., :half], xf[..., half:]
    return jnp.concatenate([x1 * cos - x2 * sin, x2 * cos + x1 * sin], axis=-1).astype(x.dtype)


def dense_block_attention(q, k, v, scale):
    B, S, H, Dq = q.shape
    nb = S // Q_BLOCK
    qb = q.reshape(B, nb, Q_BLOCK, H, Dq).transpose(1, 0, 2, 3, 4)

    def one(qblk):
        s = jnp.einsum('bqhd,bkhd->bhqk', qblk, k).astype(jnp.float32) * scale
        p = jax.nn.softmax(s, axis=-1).astype(v.dtype)
        return jnp.einsum('bhqk,bkhd->bqhd', p, v)

    o = lax.map(one, qb)
    return o.transpose(1, 0, 2, 3, 4).reshape(B, S, H * v.shape[-1])


def banded_window_attention(q, k, v, sinks):
    B, S = q.shape[0], q.shape[1]
    W = WINDOW
    nb = S // W
    qb = q.reshape(B, nb, W, SWA_KV_HEADS, SWA_GROUPS, SWA_HEAD_DIM).transpose(1, 0, 2, 3, 4, 5)
    pad = ((0, 0), (W, W), (0, 0), (0, 0))
    kp, vp = jnp.pad(k, pad), jnp.pad(v, pad)

    def blocks(t):
        parts = [t[:, i * W:i * W + S].reshape(B, nb, W, SWA_KV_HEADS, SWA_HEAD_DIM) for i in range(3)]
        return jnp.concatenate(parts, axis=2).transpose(1, 0, 2, 3, 4)

    kb, vb = blocks(kp), blocks(vp)
    rel = jnp.arange(3 * W)[None, :] - W - jnp.arange(W)[:, None]
    kpos = jnp.arange(nb)[:, None] * W - W + jnp.arange(3 * W)[None, :]
    mask = (jnp.abs(rel) <= W)[None] & ((kpos >= 0) & (kpos < S))[:, None, :]
    sink = sinks.astype(jnp.float32).reshape(SWA_KV_HEADS, SWA_GROUPS)[None, :, :, None, None]
    scale = SWA_HEAD_DIM ** -0.5

    def one(args):
        qblk, kblk, vblk, m = args
        s = jnp.einsum('bqhgd,bkhd->bhgqk', qblk, kblk).astype(jnp.float32) * scale
        s = jnp.where(m[None, None, None], s, -jnp.inf)
        mx = jnp.maximum(jnp.max(s, axis=-1, keepdims=True), sink)
        e = jnp.exp(s - mx)
        p = e / (jnp.sum(e, axis=-1, keepdims=True) + jnp.exp(sink - mx))
        return jnp.einsum('bhgqk,bkhd->bqhgd', p.astype(vblk.dtype), vblk)

    o = lax.map(one, (qb, kb, vb, mask))
    return o.transpose(1, 0, 2, 3, 4, 5).reshape(B, S, SWA_Q_HEADS * SWA_HEAD_DIM)


def token_mixers(h, w_in, g_q_lat, w_uq, g_kv_lat, w_ukv, sinks, w_br_mla, w_br_swa, w_out):
    B, S, _ = h.shape
    z = h @ w_in
    c_q, c_kv, k_rope, q_s, k_s, v_s, g_a, g_b = jnp.split(z, IN_SPLIT_POINTS, axis=-1)
    pos = jnp.arange(S, dtype=jnp.float32)
    q = (rmsnorm(c_q, g_q_lat) @ w_uq).reshape(B, S, MLA_HEADS, MLA_NOPE + MLA_ROPE)
    q_nope, q_pe = q[..., :MLA_NOPE], rope(q[..., MLA_NOPE:], pos)
    kv = (rmsnorm(c_kv, g_kv_lat) @ w_ukv).reshape(B, S, MLA_HEADS, MLA_NOPE + MLA_V)
    k_nope, v_m = kv[..., :MLA_NOPE], kv[..., MLA_NOPE:]
    k_pe = jnp.broadcast_to(rope(k_rope[:, :, None, :], pos), (B, S, MLA_HEADS, MLA_ROPE))
    q_m = jnp.concatenate([q_nope, q_pe], axis=-1)
    k_m = jnp.concatenate([k_nope, k_pe], axis=-1)
    o_a = dense_block_attention(q_m, k_m, v_m, (MLA_NOPE + MLA_ROPE) ** -0.5)
    q_s = rope(q_s.reshape(B, S, SWA_Q_HEADS, SWA_HEAD_DIM), pos)
    k_s = rope(k_s.reshape(B, S, SWA_KV_HEADS, SWA_HEAD_DIM), pos)
    v_s = v_s.reshape(B, S, SWA_KV_HEADS, SWA_HEAD_DIM)
    o_b = banded_window_attention(q_s, k_s, v_s, sinks)
    y = jax.nn.sigmoid(g_a) * (o_a @ w_br_mla) + jax.nn.sigmoid(g_b) * (o_b @ w_br_swa)
    return y @ w_out


def moe(h, w_router, b_router, w_gu, b_gu, w_dn, b_dn):
    T = h.shape[0]
    logits = (h @ w_router).astype(jnp.float32) + b_router.astype(jnp.float32)
    top_v, top_i = lax.top_k(logits, TOP_K)
    gates = jax.nn.softmax(top_v, axis=-1).astype(h.dtype)
    A = T * TOP_K
    e_flat = top_i.reshape(A)
    tok_flat = jnp.repeat(jnp.arange(T, dtype=jnp.int32), TOP_K)
    g_flat = gates.reshape(A)
    order = jnp.argsort(e_flat)
    se = e_flat[order]
    counts = jnp.bincount(e_flat, length=N_EXPERTS)
    padded = (counts + EXPERT_BLOCK - 1) // EXPERT_BLOCK * EXPERT_BLOCK
    start = jnp.cumsum(counts) - counts
    pend = jnp.cumsum(padded)
    pstart = pend - padded
    dest = pstart[se] + jnp.arange(A) - start[se]
    n_blocks = -(-A // EXPERT_BLOCK) + N_EXPERTS
    P = n_blocks * EXPERT_BLOCK
    buf_tok = jnp.zeros((P,), jnp.int32).at[dest].set(tok_flat[order])
    buf_g = jnp.zeros((P,), h.dtype).at[dest].set(g_flat[order])
    blk_e = jnp.minimum(jnp.searchsorted(pend, jnp.arange(n_blocks) * EXPERT_BLOCK, side='right'), N_EXPERTS - 1)

    def run(args):
        tok, e = args
        xb = h[tok]
        gu = xb @ w_gu[e] + b_gu[e]
        glu, lin = gu[:, :D_FF], gu[:, D_FF:]
        glu = jnp.minimum(glu, SWIGLU_LIMIT)
        lin = jnp.clip(lin, -SWIGLU_LIMIT, SWIGLU_LIMIT)
        act = glu * jax.nn.sigmoid(SWIGLU_ALPHA * glu) * (lin + 1.0)
        return act @ w_dn[e] + b_dn[e]

    out = lax.map(run, (buf_tok.reshape(n_blocks, EXPERT_BLOCK), blk_e))
    return jnp.zeros((T, h.shape[1]), h.dtype).at[buf_tok].add(out.reshape(P, -1) * buf_g[:, None])


def encoder_layer(x, c, w_ada, b_ada, g_pre_mix, w_in, g_q_lat, w_uq, g_kv_lat, w_ukv, attn_sinks,
                  w_br_mla, w_br_swa, w_out, g_post_mix, g_pre_ffn, w_router, b_router,
                  w_gu, b_gu, w_dn, b_dn, g_post_ffn):
    B, S, D = x.shape
    mod = jax.nn.silu(c) @ w_ada + b_ada
    sh1, sc1, gt1, sh2, sc2, gt2 = [m[:, None, :] for m in jnp.split(mod, N_MOD, axis=-1)]
    h = rmsnorm(x, g_pre_mix) * (1.0 + sc1) + sh1
    y = token_mixers(h, w_in, g_q_lat, w_uq, g_kv_lat, w_ukv, attn_sinks, w_br_mla, w_br_swa, w_out)
    x = x + gt1 * rmsnorm(y, g_post_mix)
    h2 = rmsnorm(x, g_pre_ffn) * (1.0 + sc2) + sh2
    f = moe(h2.reshape(B * S, D), w_router, b_router, w_gu, b_gu, w_dn, b_dn).reshape(B, S, D)
    return x + gt2 * rmsnorm(f, g_post_ffn)


def setup_inputs(seed: int = 0) -> dict:
    key = jax.random.key(seed)
    ks = jax.random.split(key, 26)

    def nrm(k, shape, scale):
        return jax.random.normal(k, shape, jnp.float32) * scale

    def gain(k, n):
        return 1.0 + nrm(k, (DEPTH, n), 0.01)

    L = DEPTH
    return {
        "x_prompt": nrm(ks[0], (BATCH, SEQ, D_MODEL), 1.0),
        "x_sample": nrm(ks[1], (DEC_BATCH, DEC_SEQ, D_MODEL), 1.0),
        "c_prompt": nrm(ks[2], (BATCH, D_MODEL), 1.0),
        "c_sample": nrm(ks[3], (DEC_BATCH, D_MODEL), 1.0),
        "w_ada": nrm(ks[4], (L, D_MODEL, N_MOD * D_MODEL), D_MODEL ** -0.5),
        "b_ada": nrm(ks[5], (L, N_MOD * D_MODEL), 0.01),
        "g_pre_mix": gain(ks[6], D_MODEL),
        "w_in": nrm(ks[7], (L, D_MODEL, IN_COLS), D_MODEL ** -0.5),
        "g_q_lat": gain(ks[8], MLA_Q_RANK),
        "w_uq": nrm(ks[9], (L, MLA_Q_RANK, MLA_HEADS * (MLA_NOPE + MLA_ROPE)), MLA_Q_RANK ** -0.5),
        "g_kv_lat": gain(ks[10], MLA_KV_RANK),
        "w_ukv": nrm(ks[11], (L, MLA_KV_RANK, MLA_HEADS * (MLA_NOPE + MLA_V)), MLA_KV_RANK ** -0.5),
        "attn_sinks": nrm(ks[12], (L, SWA_Q_HEADS), 1.0),
        "w_br_mla": nrm(ks[13], (L, MLA_HEADS * MLA_V, D_MODEL), (MLA_HEADS * MLA_V) ** -0.5),
        "w_br_swa": nrm(ks[14], (L, SWA_Q_HEADS * SWA_HEAD_DIM, D_MODEL), (SWA_Q_HEADS * SWA_HEAD_DIM) ** -0.5),
        "w_out": nrm(ks[15], (L, D_MODEL, D_MODEL), D_MODEL ** -0.5),
        "g_post_mix": gain(ks[16], D_MODEL),
        "g_pre_ffn": gain(ks[17], D_MODEL),
        "w_router": nrm(ks[18], (L, D_MODEL, N_EXPERTS), D_MODEL ** -0.5),
        "b_router": nrm(ks[19], (L, N_EXPERTS), 0.01),
        "w_gu": nrm(ks[20], (L, N_EXPERTS, D_MODEL, 2 * D_FF), D_MODEL ** -0.5),
        "b_gu": nrm(ks[21], (L, N_EXPERTS, 2 * D_FF), 0.01),
        "w_dn": nrm(ks[22], (L, N_EXPERTS, D_FF, D_MODEL), D_FF ** -0.5),
        "b_dn": nrm(ks[23], (L, N_EXPERTS, D_MODEL), 0.01),
        "g_post_ffn": gain(ks[24], D_MODEL),
    }


def reference(x_prompt, x_sample, c_prompt, c_sample, w_ada, b_ada, g_pre_mix, w_in, g_q_lat, w_uq,
              g_kv_lat, w_ukv, attn_sinks, w_br_mla, w_br_swa, w_out, g_post_mix, g_pre_ffn,
              w_router, b_router, w_gu, b_gu, w_dn, b_dn, g_post_ffn):
    def trunk(x, c):
        for l in range(DEPTH):
            x = encoder_layer(x, c, w_ada[l], b_ada[l], g_pre_mix[l], w_in[l], g_q_lat[l], w_uq[l],
                              g_kv_lat[l], w_ukv[l], attn_sinks[l], w_br_mla[l], w_br_swa[l], w_out[l],
                              g_post_mix[l], g_pre_ffn[l], w_router[l], b_router[l], w_gu[l], b_gu[l],
                              w_dn[l], b_dn[l], g_post_ffn[l])
        return x

    y_prompt = trunk(x_prompt, c_prompt)
    y_sample = trunk(x_sample, c_sample)
    return (y_prompt, y_sample)
```

```python
import functools

import jax
import jax.numpy as jnp
from jax import lax
from jax.experimental import pallas as pl
from jax.experimental.pallas import tpu as pltpu

MLA_HEADS = 16
MLA_NOPE = 128
MLA_ROPE = 64
MLA_V = 128
SWA_Q_HEADS = 32
SWA_KV_HEADS = 8
SWA_HEAD_DIM = 64
WINDOW = 128
TOP_K = 4
SWIGLU_LIMIT = 7.0
SWIGLU_ALPHA = 1.702
ROPE_THETA = 10000.0
RMS_EPS = 1e-6
N_MOD = 6

LANE = 128
VMEM_LIMIT = 56 << 20

TM_IN = 512
TN_IN = 1664
TM_POST = 256
TQ_MLA = 512
TQ_SWA = 256
TM_MERGE = 512
TN_MERGE = 1024
TM_OUT = 256
TM_POS = 256
TM_DMA = 256
BM_EXP = 512
TF_EXP = 512

F32 = jnp.float32
BF16 = jnp.bfloat16
NEG_INF = float("-inf")


def _params(sem):
    return pltpu.CompilerParams(dimension_semantics=sem, vmem_limit_bytes=VMEM_LIMIT)


def _rms(x, g):
    return x * lax.rsqrt(jnp.mean(x * x, axis=-1, keepdims=True) + RMS_EPS) * g


def _mod_kernel(c_ref, w_ref, b_ref, o_ref):
    c = c_ref[...]
    a = (c * jax.nn.sigmoid(c)).astype(BF16)
    o_ref[...] = jnp.dot(a, w_ref[...].astype(BF16), preferred_element_type=F32) + b_ref[...]


def _modulation(c8, w_ada, b_ada):
    d, n = w_ada.shape
    tn = 1024
    return pl.pallas_call(
        _mod_kernel,
        grid=(n // tn,),
        in_specs=[
            pl.BlockSpec((8, d), lambda j: (0, 0)),
            pl.BlockSpec((d, tn), lambda j: (0, j)),
            pl.BlockSpec((1, tn), lambda j: (0, j)),
        ],
        out_specs=pl.BlockSpec((8, tn), lambda j: (0, j)),
        out_shape=jax.ShapeDtypeStruct((8, n), F32),
        compiler_params=_params(("arbitrary",)),
        name="adaln_mod",
    )(c8, w_ada, b_ada.reshape(1, n))


def _inproj_kernel(x_ref, mod_ref, g_ref, w_ref, z_ref, h_ref):
    @pl.when(pl.program_id(1) == 0)
    def _():
        h = _rms(x_ref[...], g_ref[...]) * (1.0 + mod_ref[0, 1:2, :]) + mod_ref[0, 0:1, :]
        h_ref[...] = h.astype(BF16)

    z_ref[...] = jnp.dot(h_ref[...], w_ref[...], preferred_element_type=F32).astype(z_ref.dtype)


def _in_projection(x, mod3, g, w_in_p, seq):
    t, d = x.shape
    n = w_in_p.shape[1]
    tiles_per_seq = seq // TM_IN
    return pl.pallas_call(
        _inproj_kernel,
        grid=(t // TM_IN, n // TN_IN),
        in_specs=[
            pl.BlockSpec((TM_IN, d), lambda i, j: (i, 0)),
            pl.BlockSpec((1, N_MOD, d), lambda i, j: (i // tiles_per_seq, 0, 0)),
            pl.BlockSpec((1, d), lambda i, j: (0, 0)),
            pl.BlockSpec((d, TN_IN), lambda i, j: (0, j)),
        ],
        out_specs=pl.BlockSpec((TM_IN, TN_IN), lambda i, j: (i, j)),
        out_shape=jax.ShapeDtypeStruct((t, n), BF16),
        scratch_shapes=[pltpu.VMEM((TM_IN, d), BF16)],
        compiler_params=_params(("arbitrary", "arbitrary")),
        name="in_projection",
    )(x, mod3, g.reshape(1, d), w_in_p)


def _postproj_kernel(cq_ref, ckv_ref, qs_ref, ks_ref, vs_ref, kr_ref, cos_ref, sin_ref,
                     gq_ref, gkv_ref, wuq_ref, wukv_ref,
                     q_ref, k_ref, v_ref, qso_ref, klo_ref, khi_ref, vlo_ref, vhi_ref):
    tm = cq_ref.shape[0]
    cos = cos_ref[...]
    sin = sin_ref[...]
    lane = lax.broadcasted_iota(jnp.int32, (tm, LANE), 1)
    first_half = (lane & 63) < 32
    low = lane < 64

    def rope(x):
        rot = jnp.where(first_half, pltpu.roll(x, LANE - 32, 1), pltpu.roll(x, 32, 1))
        return x * cos + rot * sin

    nh = MLA_HEADS
    scale = float((MLA_NOPE + MLA_ROPE) ** -0.5)
    cqn = _rms(cq_ref[...].astype(F32), gq_ref[...]).astype(BF16)
    q = jnp.dot(cqn, wuq_ref[...], preferred_element_type=F32) * scale
    for h in range(nh):
        q_ref[0, h, :, 0:LANE] = q[:, h * LANE:(h + 1) * LANE].astype(BF16)
    for m in range(nh // 2):
        pe = rope(q[:, (nh + m) * LANE:(nh + m + 1) * LANE]).astype(BF16)
        q_ref[0, 2 * m, :, LANE:2 * LANE] = pe
        q_ref[0, 2 * m + 1, :, LANE:2 * LANE] = pe

    ckvn = _rms(ckv_ref[...].astype(F32), gkv_ref[...]).astype(BF16)
    kv = jnp.dot(ckvn, wukv_ref[...], preferred_element_type=F32)
    kr = rope(kr_ref[...].astype(F32))
    kpe_lo = jnp.where(low, kr, 0.0).astype(BF16)
    kpe_hi = jnp.where(low, 0.0, kr).astype(BF16)
    for h in range(nh):
        k_ref[0, h, :, 0:LANE] = kv[:, h * LANE:(h + 1) * LANE].astype(BF16)
        k_ref[0, h, :, LANE:2 * LANE] = kpe_lo if h % 2 == 0 else kpe_hi
        v_ref[0, h, :, :] = kv[:, (nh + h) * LANE:(nh + h + 1) * LANE].astype(BF16)

    swa_scale = float(SWA_HEAD_DIM ** -0.5)
    for m in range(SWA_Q_HEADS // 2):
        x = qs_ref[:, m * LANE:(m + 1) * LANE].astype(F32)
        qso_ref[:, m * LANE:(m + 1) * LANE] = (rope(x) * swa_scale).astype(BF16)

    for m in range(SWA_KV_HEADS // 2):
        sl = slice(m * LANE, (m + 1) * LANE)
        for src_ref, lo_ref, hi_ref, roped in ((ks_ref, klo_ref, khi_ref, True),
                                               (vs_ref, vlo_ref, vhi_ref, False)):
            a = src_ref[:, sl].astype(F32)
            if roped:
                a = rope(a)
            b = pltpu.roll(a, 64, 1)
            e0 = slice((2 * m) * LANE, (2 * m + 1) * LANE)
            e1 = slice((2 * m + 1) * LANE, (2 * m + 2) * LANE)
            lo_ref[:, e0] = jnp.where(low, a, 0.0).astype(BF16)
            hi_ref[:, e0] = jnp.where(low, 0.0, b).astype(BF16)
            lo_ref[:, e1] = jnp.where(low, b, 0.0).astype(BF16)
            hi_ref[:, e1] = jnp.where(low, 0.0, a).astype(BF16)


def _post_projection(z, cols, cos, sin, g_q, g_kv, w_uq_p, w_ukv_p, nseq, seq):
    t = z.shape[0]
    tm = TM_POST
    tps = seq // tm
    nh = MLA_HEADS
    qr, kvr = g_q.shape[0], g_kv.shape[0]
    qsw = SWA_Q_HEADS * SWA_HEAD_DIM
    ksw = SWA_KV_HEADS * SWA_HEAD_DIM

    def zspec(width, off):
        blk = off // width
        return pl.BlockSpec((tm, width), lambda i: (i, blk))

    head_map = lambda i: (i // tps, 0, i % tps, 0)
    tok_map = lambda i: (i, 0)
    out_shapes = (
        jax.ShapeDtypeStruct((nseq, nh, seq, 2 * LANE), BF16),
        jax.ShapeDtypeStruct((nseq, nh, seq, 2 * LANE), BF16),
        jax.ShapeDtypeStruct((nseq, nh, seq, LANE), BF16),
        jax.ShapeDtypeStruct((t, qsw), BF16),
        jax.ShapeDtypeStruct((t, SWA_KV_HEADS * LANE), BF16),
        jax.ShapeDtypeStruct((t, SWA_KV_HEADS * LANE), BF16),
        jax.ShapeDtypeStruct((t, SWA_KV_HEADS * LANE), BF16),
        jax.ShapeDtypeStruct((t, SWA_KV_HEADS * LANE), BF16),
    )
    out_specs = (
        pl.BlockSpec((1, nh, tm, 2 * LANE), head_map),
        pl.BlockSpec((1, nh, tm, 2 * LANE), head_map),
        pl.BlockSpec((1, nh, tm, LANE), head_map),
        pl.BlockSpec((tm, qsw), tok_map),
        pl.BlockSpec((tm, SWA_KV_HEADS * LANE), tok_map),
        pl.BlockSpec((tm, SWA_KV_HEADS * LANE), tok_map),
        pl.BlockSpec((tm, SWA_KV_HEADS * LANE), tok_map),
        pl.BlockSpec((tm, SWA_KV_HEADS * LANE), tok_map),
    )
    return pl.pallas_call(
        _postproj_kernel,
        grid=(t // tm,),
        in_specs=[
            zspec(qr, cols["c_q"]), zspec(kvr, cols["c_kv"]), zspec(qsw, cols["q_s"]),
            zspec(ksw, cols["k_s"]), zspec(ksw, cols["v_s"]), zspec(LANE, cols["k_rope"]),
            pl.BlockSpec((tm, LANE), lambda i: (i % tps, 0)),
            pl.BlockSpec((tm, LANE), lambda i: (i % tps, 0)),
            pl.BlockSpec((1, qr), lambda i: (0, 0)),
            pl.BlockSpec((1, kvr), lambda i: (0, 0)),
            pl.BlockSpec(w_uq_p.shape, lambda i: (0, 0)),
            pl.BlockSpec(w_ukv_p.shape, lambda i: (0, 0)),
        ],
        out_specs=out_specs,
        out_shape=out_shapes,
        compiler_params=_params(("arbitrary",)),
        name="post_projection",
    )(z, z, z, z, z, z, cos, sin, g_q.reshape(1, qr), g_kv.reshape(1, kvr), w_uq_p, w_ukv_p)


def _mla_kernel(q_ref, k_ref, v_ref, o_ref):
    s = lax.dot_general(q_ref[0, 0], k_ref[0, 0], (((1,), (1,)), ((), ())),
                        preferred_element_type=F32)
    m = jnp.max(s, axis=-1, keepdims=True)
    p = jnp.exp(s - m)
    l = jnp.sum(p, axis=-1, keepdims=True)
    o = jnp.dot(p.astype(BF16), v_ref[0, 0], preferred_element_type=F32)
    o_ref[...] = (o / l).astype(o_ref.dtype)


def _mla_attention(q, k, v):
    nseq, nh, seq, dqk = q.shape
    dv = v.shape[-1]
    nq = seq // TQ_MLA
    return pl.pallas_call(
        _mla_kernel,
        grid=(nseq, nh, nq),
        in_specs=[
            pl.BlockSpec((1, 1, TQ_MLA, dqk), lambda b, h, i: (b, h, i, 0)),
            pl.BlockSpec((1, 1, seq, dqk), lambda b, h, i: (b, h, 0, 0)),
            pl.BlockSpec((1, 1, seq, dv), lambda b, h, i: (b, h, 0, 0)),
        ],
        out_specs=pl.BlockSpec((TQ_MLA, dv), lambda b, h, i: (b * nq + i, h)),
        out_shape=jax.ShapeDtypeStruct((nseq * seq, nh * dv), BF16),
        compiler_params=_params(("arbitrary", "arbitrary", "arbitrary")),
        name="mla_attention",
    )(q, k, v)


def _swa_kernel(sink_ref, q_ref,
                klo_p, klo_c, klo_n, khi_p, khi_c, khi_n,
                vlo_p, vlo_c, vlo_n, vhi_p, vhi_c, vhi_n,
                o_ref, klo_w, khi_w, vlo_w, vhi_w, *, seq):
    tq = q_ref.shape[0]
    w = WINDOW
    nk = tq + 2 * w
    for win, (p, c, n) in ((klo_w, (klo_p, klo_c, klo_n)), (khi_w, (khi_p, khi_c, khi_n)),
                           (vlo_w, (vlo_p, vlo_c, vlo_n)), (vhi_w, (vhi_p, vhi_c, vhi_n))):
        win[0:w, :] = p[...]
        win[w:w + tq, :] = c[...]
        win[w + tq:nk, :] = n[...]

    q0 = pl.program_id(1) * tq
    row = lax.broadcasted_iota(jnp.int32, (tq, nk), 0)
    col = lax.broadcasted_iota(jnp.int32, (tq, nk), 1)
    rel = col - w - row
    kpos = q0 - w + col
    valid = (rel <= w) & (rel >= -w) & (kpos >= 0) & (kpos < seq)

    groups = SWA_Q_HEADS // SWA_KV_HEADS
    for m in range(SWA_Q_HEADS // 2):
        g = (2 * m) // groups
        gs = slice(g * LANE, (g + 1) * LANE)
        qp = q_ref[:, m * LANE:(m + 1) * LANE]
        acc = None
        for half, (kw, vw) in enumerate(((klo_w, vlo_w), (khi_w, vhi_w))):
            sink = sink_ref[2 * m + half]
            s = lax.dot_general(qp, kw[:, gs], (((1,), (1,)), ((), ())),
                                preferred_element_type=F32)
            s = jnp.where(valid, s, NEG_INF)
            mx = jnp.maximum(jnp.max(s, axis=-1, keepdims=True), sink)
            e = jnp.exp(s - mx)
            den = jnp.sum(e, axis=-1, keepdims=True) + jnp.exp(sink - mx)
            pv = jnp.dot(e.astype(BF16), vw[:, gs], preferred_element_type=F32) / den
            acc = pv if acc is None else acc + pv
        o_ref[:, m * LANE:(m + 1) * LANE] = acc.astype(o_ref.dtype)


def _swa_attention(sinks, qs, klo, khi, vlo, vhi, nseq, seq):
    t, qw = qs.shape
    kw = klo.shape[1]
    tq = TQ_SWA
    nq = seq // tq
    r = tq // WINDOW
    nwb = seq // WINDOW

    prev = pl.BlockSpec((WINDOW, kw), lambda b, i, s: (b * nwb + jnp.maximum(i * r - 1, 0), 0))
    cur = pl.BlockSpec((tq, kw), lambda b, i, s: (b * nq + i, 0))
    nxt = pl.BlockSpec((WINDOW, kw), lambda b, i, s: (b * nwb + jnp.minimum((i + 1) * r, nwb - 1), 0))
    grid_spec = pltpu.PrefetchScalarGridSpec(
        num_scalar_prefetch=1,
        grid=(nseq, nq),
        in_specs=[pl.BlockSpec((tq, qw), lambda b, i, s: (b * nq + i, 0))] + [prev, cur, nxt] * 4,
        out_specs=pl.BlockSpec((tq, qw), lambda b, i, s: (b * nq + i, 0)),
        scratch_shapes=[pltpu.VMEM((tq + 2 * WINDOW, kw), BF16)] * 4,
    )
    return pl.pallas_call(
        functools.partial(_swa_kernel, seq=seq),
        grid_spec=grid_spec,
        out_shape=jax.ShapeDtypeStruct((t, qw), BF16),
        compiler_params=_params(("arbitrary", "arbitrary")),
        name="swa_attention",
    )(sinks, qs, klo, klo, klo, khi, khi, khi, vlo, vlo, vlo, vhi, vhi, vhi)


def _merge_kernel(oa_ref, ob_ref, ga_ref, gb_ref, wa_ref, wb_ref, y_ref):
    a = jnp.dot(oa_ref[...], wa_ref[...], preferred_element_type=F32)
    b = jnp.dot(ob_ref[...], wb_ref[...], preferred_element_type=F32)
    y = jax.nn.sigmoid(ga_ref[...].astype(F32)) * a + jax.nn.sigmoid(gb_ref[...].astype(F32)) * b
    y_ref[...] = y.astype(y_ref.dtype)


def _merge(o_a, o_b, z, cols, w_a, w_b):
    t, d = o_a.shape[0], w_a.shape[1]
    tm, tn = TM_MERGE, TN_MERGE
    ga_blk, gb_blk = cols["g_a"] // tn, cols["g_b"] // tn
    return pl.pallas_call(
        _merge_kernel,
        grid=(t // tm, d // tn),
        in_specs=[
            pl.BlockSpec((tm, o_a.shape[1]), lambda i, j: (i, 0)),
            pl.BlockSpec((tm, o_b.shape[1]), lambda i, j: (i, 0)),
            pl.BlockSpec((tm, tn), lambda i, j: (i, ga_blk + j)),
            pl.BlockSpec((tm, tn), lambda i, j: (i, gb_blk + j)),
            pl.BlockSpec((w_a.shape[0], tn), lambda i, j: (0, j)),
            pl.BlockSpec((w_b.shape[0], tn), lambda i, j: (0, j)),
        ],
        out_specs=pl.BlockSpec((tm, tn), lambda i, j: (i, j)),
        out_shape=jax.ShapeDtypeStruct((t, d), BF16),
        compiler_params=_params(("arbitrary", "arbitrary")),
        name="branch_merge",
    )(o_a, o_b, z, z, w_a, w_b)


def _outproj_kernel(y_ref, x_ref, mod_ref, gpm_ref, gpf_ref, wo_ref, wr_ref, br_ref,
                    x1_ref, h2_ref, idx_ref, gate_ref, cnt_ref, *, n_experts):
    tm = y_ref.shape[0]
    u = jnp.dot(y_ref[...], wo_ref[...], preferred_element_type=F32)
    x1 = x_ref[...] + mod_ref[0, 2:3, :] * _rms(u, gpm_ref[...])
    x1_ref[...] = x1
    h2 = _rms(x1, gpf_ref[...]) * (1.0 + mod_ref[0, 4:5, :]) + mod_ref[0, 3:4, :]
    h2_ref[...] = h2.reshape(h2_ref.shape)

    logits = jnp.dot(h2.astype(BF16), wr_ref[...], preferred_element_type=F32) + br_ref[...]
    lane = lax.broadcasted_iota(jnp.int32, (tm, LANE), 1)
    lane_f = lane.astype(F32)
    cur = jnp.where(lane < n_experts, logits, NEG_INF)
    vals, idxs = [], []
    for _ in range(TOP_K):
        m = jnp.max(cur, axis=-1, keepdims=True)
        ix = jnp.min(jnp.where(cur == m, lane_f, float(LANE)), axis=-1, keepdims=True)
        vals.append(m)
        idxs.append(ix)
        cur = jnp.where(lane_f == ix, NEG_INF, cur)
    es = [jnp.exp(v - vals[0]) for v in vals]
    den = es[0]
    for e in es[1:]:
        den = den + e
    idx_out = jnp.zeros((tm, LANE), F32)
    gate_out = jnp.zeros((tm, LANE), F32)
    sel = jnp.zeros((tm, LANE), F32)
    for r in range(TOP_K):
        idx_out = jnp.where(lane == r, idxs[r], idx_out)
        gate_out = jnp.where(lane == r, es[r] / den, gate_out)
        sel = sel + jnp.where(lane_f == idxs[r], 1.0, 0.0)
    idx_ref[...] = idx_out.astype(jnp.int32)
    gate_ref[...] = gate_out

    @pl.when(pl.program_id(0) == 0)
    def _():
        cnt_ref[...] = jnp.zeros_like(cnt_ref)

    cnt_ref[0:1, :] += jnp.sum(sel, axis=0, keepdims=True)


def _out_projection(y, x, mod3, g_pm, g_pf, w_out, w_router_p, b_router_p, n_experts, seq):
    t, d = x.shape
    tm = TM_OUT
    tps = seq // tm
    tok = lambda i: (i, 0)
    const = lambda i: (0, 0)
    return pl.pallas_call(
        functools.partial(_outproj_kernel, n_experts=n_experts),
        grid=(t // tm,),
        in_specs=[
            pl.BlockSpec((tm, d), tok),
            pl.BlockSpec((tm, d), tok),
            pl.BlockSpec((1, N_MOD, d), lambda i: (i // tps, 0, 0)),
            pl.BlockSpec((1, d), const),
            pl.BlockSpec((1, d), const),
            pl.BlockSpec((d, d), const),
            pl.BlockSpec((d, LANE), const),
            pl.BlockSpec((1, LANE), const),
        ],
        out_specs=(
            pl.BlockSpec((tm, d), tok),
            pl.BlockSpec((tm, 1, d), lambda i: (i, 0, 0)),
            pl.BlockSpec((tm, LANE), tok),
            pl.BlockSpec((tm, LANE), tok),
            pl.BlockSpec((8, LANE), const),
        ),
        out_shape=(
            jax.ShapeDtypeStruct((t, d), F32),
            jax.ShapeDtypeStruct((t, 1, d), F32),
            jax.ShapeDtypeStruct((t, LANE), jnp.int32),
            jax.ShapeDtypeStruct((t, LANE), F32),
            jax.ShapeDtypeStruct((8, LANE), F32),
        ),
        compiler_params=_params(("arbitrary",)),
        name="out_projection_router",
    )(y, x, mod3, g_pm.reshape(1, d), g_pf.reshape(1, d), w_out, w_router_p, b_router_p)


def _positions_kernel(idx_ref, cnt_ref, dest_ref, meta_ref, zst_ref, carry_ref, ltri_ref, pst_ref,
                      *, n_experts, bm):
    tm = idx_ref.shape[0]
    nbp = meta_ref.shape[0]

    @pl.when(pl.program_id(0) == 0)
    def _():
        r = lax.broadcasted_iota(jnp.int32, (tm, tm), 0)
        c = lax.broadcasted_iota(jnp.int32, (tm, tm), 1)
        ltri_ref[...] = jnp.where(c < r, 1.0, 0.0).astype(BF16)
        carry_ref[...] = jnp.zeros_like(carry_ref)
        nblk = jnp.floor((cnt_ref[...] + float(bm - 1)) * (1.0 / bm))
        ur = lax.broadcasted_iota(jnp.int32, (LANE, LANE), 0)
        uc = lax.broadcasted_iota(jnp.int32, (LANE, LANE), 1)
        upper = jnp.where(ur <= uc, 1.0, 0.0).astype(BF16)
        pend = jnp.dot(nblk.astype(BF16), upper, preferred_element_type=F32)
        pst_ref[...] = (pend - nblk) * float(bm)
        lane8 = lax.broadcasted_iota(jnp.int32, (8, LANE), 1)
        zst_ref[...] = jnp.maximum(pend * float(bm) - float(bm), 0.0).astype(jnp.int32)
        pend0 = pend[0:1, :]
        n_used = jnp.sum(jnp.where(lane8[0:1, :] == n_experts - 1, pend0, 0.0), axis=-1, keepdims=True)
        blk = lax.broadcasted_iota(jnp.int32, (nbp, LANE), 0).astype(F32)
        lane = lax.broadcasted_iota(jnp.int32, (nbp, LANE), 1)
        passed = jnp.where((lane < n_experts) & (pend0 <= blk), 1.0, 0.0)
        blk_e = jnp.minimum(jnp.sum(passed, axis=-1, keepdims=True), float(n_experts - 1))
        meta = jnp.where(lane == 0, blk_e, jnp.where(lane == 1, n_used, 0.0))
        meta_ref[...] = meta.astype(jnp.int32)

    lane = lax.broadcasted_iota(jnp.int32, (tm, LANE), 1)
    idx = idx_ref[...]
    hots = []
    sel = jnp.zeros((tm, LANE), F32)
    for r in range(TOP_K):
        col = jnp.sum(jnp.where(lane == r, idx, 0).astype(F32), axis=-1, keepdims=True)
        hot = jnp.where(lane.astype(F32) == col, 1.0, 0.0)
        hots.append(hot)
        sel = sel + hot
    rank = jnp.dot(ltri_ref[...], sel.astype(BF16), preferred_element_type=F32) + carry_ref[0:1, :]
    pos = pst_ref[0:1, :] + rank
    dest = jnp.zeros((tm, LANE), F32)
    for r in range(TOP_K):
        d = jnp.sum(hots[r] * pos, axis=-1, keepdims=True)
        dest = jnp.where(lane == r, d, dest)
    dest_ref[...] = dest.astype(jnp.int32)
    carry_ref[0:1, :] += jnp.sum(sel, axis=0, keepdims=True)


def _positions(idx, cnt, n_experts, n_blocks):
    t = idx.shape[0]
    tm = TM_POS
    nbp = -(-n_blocks // 8) * 8
    return pl.pallas_call(
        functools.partial(_positions_kernel, n_experts=n_experts, bm=BM_EXP),
        grid=(t // tm,),
        in_specs=[pl.BlockSpec((tm, LANE), lambda i: (i, 0)),
                  pl.BlockSpec((8, LANE), lambda i: (0, 0))],
        out_specs=(pl.BlockSpec((tm, LANE), lambda i: (i, 0)),
                   pl.BlockSpec((nbp, LANE), lambda i: (0, 0)),
                   pl.BlockSpec((8, LANE), lambda i: (0, 0))),
        out_shape=(jax.ShapeDtypeStruct((t, LANE), jnp.int32),
                   jax.ShapeDtypeStruct((nbp, LANE), jnp.int32),
                   jax.ShapeDtypeStruct((8, LANE), jnp.int32)),
        scratch_shapes=[pltpu.VMEM((8, LANE), F32), pltpu.VMEM((tm, tm), BF16), pltpu.VMEM((8, LANE), F32)],
        compiler_params=_params(("arbitrary",)),
        name="dispatch_positions",
    )(idx, cnt)


def _dispatch_kernel(zst_ref, dest_ref, h2_ref, xs_ref, zero_ref, zsem, sem, *, n_experts, bm):
    i = pl.program_id(0)
    rows = dest_ref.shape[0]
    tm = rows // TOP_K

    def zero_copy(e):
        return pltpu.make_async_copy(zero_ref, xs_ref.at[pl.ds(zst_ref[e], bm)], zsem)

    def tail_copy(j):
        return pltpu.make_async_copy(zero_ref, xs_ref.at[pl.ds(j * bm, bm)], zsem)

    @pl.when(i == 0)
    def _():
        zero_ref[...] = jnp.zeros_like(zero_ref)
        n_used = zst_ref[n_experts]
        n_blocks = xs_ref.shape[0] // bm
        for e in range(n_experts):
            zero_copy(e).start()
        lax.fori_loop(n_used, n_blocks, lambda j, c: (tail_copy(j).start(), c)[1], 0)
        for e in range(n_experts):
            zero_copy(e).wait()
        lax.fori_loop(n_used, n_blocks, lambda j, c: (tail_copy(j).wait(), c)[1], 0)

    base = i * tm

    def body(j, carry):
        t = base + j
        for r in range(TOP_K):
            d = dest_ref[j * TOP_K + r]
            pltpu.make_async_copy(h2_ref.at[t], xs_ref.at[d], sem).start()
        return carry

    lax.fori_loop(0, tm, body, 0)
    pltpu.make_async_copy(h2_ref.at[pl.ds(0, rows)], xs_ref.at[pl.ds(0, rows)], sem).wait()


def _dispatch(zst, dest_flat, h2, n_experts, n_blocks):
    t, _, d = h2.shape
    rows = TM_DMA * TOP_K
    grid_spec = pltpu.PrefetchScalarGridSpec(
        num_scalar_prefetch=1,
        grid=(t // TM_DMA,),
        in_specs=[pl.BlockSpec((rows,), lambda i, z: (i,), memory_space=pltpu.SMEM),
                  pl.BlockSpec(memory_space=pl.ANY)],
        out_specs=pl.BlockSpec(memory_space=pl.ANY),
        scratch_shapes=[pltpu.VMEM((BM_EXP, 1, d), F32),
                        pltpu.SemaphoreType.DMA(()), pltpu.SemaphoreType.DMA(())],
    )
    return pl.pallas_call(
        functools.partial(_dispatch_kernel, n_experts=n_experts, bm=BM_EXP),
        grid_spec=grid_spec,
        out_shape=jax.ShapeDtypeStruct((n_blocks * BM_EXP, 1, d), F32),
        compiler_params=_params(("arbitrary",)),
        name="row_dispatch",
    )(zst, dest_flat, h2)


def _expert_kernel(sp_ref, xs_ref, wg_ref, wl_ref, bg_ref, bl_ref, wd_ref, bd_ref, ys_ref,
                   x2d_ref, xb_ref, acc_ref, *, n_blocks):
    j = pl.program_id(0)
    f = pl.program_id(1)
    nf = pl.num_programs(1)
    used = j < sp_ref[n_blocks]

    @pl.when(used & (f == 0))
    def _():
        x2d_ref[...] = xs_ref[...].reshape(x2d_ref.shape)
        xb_ref[...] = x2d_ref[...].astype(BF16)
        acc_ref[...] = jnp.broadcast_to(bd_ref[0], acc_ref.shape)

    @pl.when(jnp.logical_not(used) & (f == 0))
    def _():
        ys_ref[...] = jnp.zeros_like(ys_ref)

    @pl.when(used)
    def _():
        xb = xb_ref[...]
        glu = jnp.dot(xb, wg_ref[0], preferred_element_type=F32) + bg_ref[0]
        lin = jnp.dot(xb, wl_ref[0], preferred_element_type=F32) + bl_ref[0]
        glu = jnp.minimum(glu, SWIGLU_LIMIT)
        lin = jnp.clip(lin, -SWIGLU_LIMIT, SWIGLU_LIMIT)
        act = glu * jax.nn.sigmoid(SWIGLU_ALPHA * glu) * (lin + 1.0)
        acc_ref[...] += jnp.dot(act.astype(BF16), wd_ref[0], preferred_element_type=F32)

    @pl.when(used & (f == nf - 1))
    def _():
        ys_ref[...] = acc_ref[...].reshape(ys_ref.shape)


def _experts(sp, xs, w_gu, b_gu, w_dn, b_dn, n_blocks):
    p, _, d = xs.shape
    n_experts, _, ff2 = w_gu.shape
    ff = ff2 // 2
    nf = ff // TF_EXP
    bm = BM_EXP

    def blk(j, s):
        return jnp.minimum(j, s[n_blocks] - 1)

    def fch(j, f, s):
        return jnp.where(j < s[n_blocks], f, nf - 1)

    grid_spec = pltpu.PrefetchScalarGridSpec(
        num_scalar_prefetch=1,
        grid=(n_blocks, nf),
        in_specs=[
            pl.BlockSpec((bm, 1, d), lambda j, f, s: (blk(j, s), 0, 0)),
            pl.BlockSpec((1, d, TF_EXP), lambda j, f, s: (s[blk(j, s)], 0, fch(j, f, s))),
            pl.BlockSpec((1, d, TF_EXP), lambda j, f, s: (s[blk(j, s)], 0, nf + fch(j, f, s))),
            pl.BlockSpec((1, 1, TF_EXP), lambda j, f, s: (s[blk(j, s)], 0, fch(j, f, s))),
            pl.BlockSpec((1, 1, TF_EXP), lambda j, f, s: (s[blk(j, s)], 0, nf + fch(j, f, s))),
            pl.BlockSpec((1, TF_EXP, d), lambda j, f, s: (s[blk(j, s)], fch(j, f, s), 0)),
            pl.BlockSpec((1, 1, d), lambda j, f, s: (s[blk(j, s)], 0, 0)),
        ],
        out_specs=pl.BlockSpec((bm, 1, d), lambda j, f, s: (j, 0, 0)),
        scratch_shapes=[pltpu.VMEM((bm, d), F32), pltpu.VMEM((bm, d), BF16), pltpu.VMEM((bm, d), F32)],
    )
    return pl.pallas_call(
        functools.partial(_expert_kernel, n_blocks=n_blocks),
        grid_spec=grid_spec,
        out_shape=jax.ShapeDtypeStruct((p, 1, d), F32),
        compiler_params=_params(("arbitrary", "arbitrary")),
        name="expert_mlp",
    )(sp, xs, w_gu, w_gu, b_gu.reshape(n_experts, 1, ff2), b_gu.reshape(n_experts, 1, ff2),
      w_dn, b_dn.reshape(n_experts, 1, d))


def _combine_kernel(dest_ref, gate_ref, x1_ref, mod_ref, g_ref, ys_ref, o_ref, buf_ref, row_ref, sem):
    tm = x1_ref.shape[0]

    def body(j, carry):
        for r in range(TOP_K):
            d = dest_ref[j * TOP_K + r]
            pltpu.make_async_copy(ys_ref.at[d], buf_ref.at[r, j], sem).start()
        return carry

    lax.fori_loop(0, tm, body, 0)
    for r in range(TOP_K):
        pltpu.make_async_copy(ys_ref.at[pl.ds(0, tm)], buf_ref.at[r], sem).wait()

    gates = gate_ref[...]
    lane = lax.broadcasted_iota(jnp.int32, gates.shape, 1)
    f = None
    for r in range(TOP_K):
        g = jnp.sum(jnp.where(lane == r, gates, 0.0), axis=-1, keepdims=True)
        row_ref[...] = buf_ref[r].reshape(row_ref.shape)
        term = row_ref[...] * g
        f = term if f is None else f + term
    o_ref[...] = x1_ref[...] + mod_ref[0, 5:6, :] * _rms(f, g_ref[...])


def _combine(dest_flat, gates, x1, mod3, g_post, ys, seq):
    t, d = x1.shape
    tm = TM_DMA
    tps = seq // tm
    return pl.pallas_call(
        _combine_kernel,
        grid=(t // tm,),
        in_specs=[
            pl.BlockSpec((tm * TOP_K,), lambda i: (i,), memory_space=pltpu.SMEM),
            pl.BlockSpec((tm, LANE), lambda i: (i, 0)),
            pl.BlockSpec((tm, d), lambda i: (i, 0)),
            pl.BlockSpec((1, N_MOD, d), lambda i: (i // tps, 0, 0)),
            pl.BlockSpec((1, d), lambda i: (0, 0)),
            pl.BlockSpec(memory_space=pl.ANY),
        ],
        out_specs=pl.BlockSpec((tm, d), lambda i: (i, 0)),
        out_shape=jax.ShapeDtypeStruct((t, d), F32),
        scratch_shapes=[pltpu.VMEM((TOP_K, tm, 1, d), F32), pltpu.VMEM((tm, d), F32),
                        pltpu.SemaphoreType.DMA(())],
        compiler_params=_params(("arbitrary",)),
        name="expert_combine",
    )(dest_flat, gates, x1, mod3, g_post.reshape(1, d), ys)


def _layout_w_in(w_in, qr, kvr):
    qsw = SWA_Q_HEADS * SWA_HEAD_DIM
    ksw = SWA_KV_HEADS * SWA_HEAD_DIM
    d = w_in.shape[0]
    sizes = (qr, kvr, MLA_ROPE, qsw, ksw, ksw, d, d)
    offs = [0]
    for s in sizes:
        offs.append(offs[-1] + s)
    c_q, c_kv, k_rope, q_s, k_s, v_s, g_a, g_b = [w_in[:, offs[i]:offs[i + 1]] for i in range(8)]
    parts = (("q_s", q_s), ("g_a", g_a), ("g_b", g_b), ("c_q", c_q), ("c_kv", c_kv),
             ("k_s", k_s), ("v_s", v_s), ("k_rope", jnp.concatenate([k_rope, k_rope], axis=1)))
    cols, off = {}, 0
    for name, p in parts:
        cols[name] = off
        off += p.shape[1]
    return jnp.concatenate([p for _, p in parts], axis=1).astype(BF16), cols


def _layout_w_uq(w_uq):
    r = w_uq.shape[0]
    w = w_uq.reshape(r, MLA_HEADS, MLA_NOPE + MLA_ROPE)
    nope = w[:, :, :MLA_NOPE].reshape(r, MLA_HEADS * MLA_NOPE)
    pe = w[:, :, MLA_NOPE:].reshape(r, MLA_HEADS * MLA_ROPE)
    return jnp.concatenate([nope, pe], axis=1).astype(BF16)


def _layout_w_ukv(w_ukv):
    r = w_ukv.shape[0]
    w = w_ukv.reshape(r, MLA_HEADS, MLA_NOPE + MLA_V)
    kn = w[:, :, :MLA_NOPE].reshape(r, MLA_HEADS * MLA_NOPE)
    v = w[:, :, MLA_NOPE:].reshape(r, MLA_HEADS * MLA_V)
    return jnp.concatenate([kn, v], axis=1).astype(BF16)


def _rope_tables(seq):
    half = MLA_ROPE // 2
    freqs = jnp.power(ROPE_THETA, -2.0 * jnp.arange(half, dtype=F32) / MLA_ROPE)
    ang = jnp.arange(seq, dtype=F32)[:, None] * freqs[None, :]
    cos, sin = jnp.cos(ang), jnp.sin(ang)
    return (jnp.concatenate([cos, cos, cos, cos], axis=1),
            jnp.concatenate([-sin, sin, -sin, sin], axis=1))


def kernel(x_prompt, x_sample, c_prompt, c_sample, w_ada, b_ada, g_pre_mix, w_in, g_q_lat, w_uq,
           g_kv_lat, w_ukv, attn_sinks, w_br_mla, w_br_swa, w_out, g_post_mix, g_pre_ffn,
           w_router, b_router, w_gu, b_gu, w_dn, b_dn, g_post_ffn):
    assert MLA_ROPE == SWA_HEAD_DIM == 64 and MLA_NOPE == MLA_V == LANE
    nb_p, seq, d = x_prompt.shape
    nb_s = x_sample.shape[0]
    assert x_sample.shape[1] == seq
    nseq = nb_p + nb_s
    t = nseq * seq
    n_experts = w_router.shape[-1]
    n_blocks = t * TOP_K // BM_EXP + n_experts

    x = jnp.concatenate([x_prompt.reshape(nb_p * seq, d), x_sample.reshape(nb_s * seq, d)], axis=0)
    c8 = jnp.concatenate([c_prompt, c_sample, jnp.zeros((8 - nseq, d), F32)], axis=0)
    cos, sin = _rope_tables(seq)

    for l in range(w_ada.shape[0]):
        w_in_p, cols = _layout_w_in(w_in[l], g_q_lat.shape[-1], g_kv_lat.shape[-1])
        w_uq_p = _layout_w_uq(w_uq[l])
        w_ukv_p = _layout_w_ukv(w_ukv[l])
        w_router_p = jnp.pad(w_router[l], ((0, 0), (0, LANE - n_experts))).astype(BF16)
        b_router_p = jnp.pad(b_router[l], (0, LANE - n_experts)).reshape(1, LANE)

        mod = _modulation(c8, w_ada[l], b_ada[l])
        mod3 = mod[:nseq].reshape(nseq, N_MOD, d)
        z = _in_projection(x, mod3, g_pre_mix[l], w_in_p, seq)
        q, k, v, qs, klo, khi, vlo, vhi = _post_projection(
            z, cols, cos, sin, g_q_lat[l], g_kv_lat[l], w_uq_p, w_ukv_p, nseq, seq)
        o_a = _mla_attention(q, k, v)
        o_b = _swa_attention(attn_sinks[l], qs, klo, khi, vlo, vhi, nseq, seq)
        y = _merge(o_a, o_b, z, cols, w_br_mla[l].astype(BF16), w_br_swa[l].astype(BF16))
        x1, h2, idx, gates, cnt = _out_projection(
            y, x, mod3, g_post_mix[l], g_pre_ffn[l], w_out[l].astype(BF16), w_router_p, b_router_p,
            n_experts, seq)
        dest, meta, zst = _positions(idx, cnt, n_experts, n_blocks)
        dest_flat = dest[:, :TOP_K].reshape(t * TOP_K)
        sp = jnp.concatenate([meta[:n_blocks, 0], meta[0:1, 1]])
        zmeta = jnp.concatenate([zst[0, :n_experts], meta[0:1, 1]])
        xs = _dispatch(zmeta, dest_flat, h2, n_experts, n_blocks)
        ys = _experts(sp, xs, w_gu[l].astype(BF16), b_gu[l], w_dn[l].astype(BF16), b_dn[l], n_blocks)
        x = _combine(dest_flat, gates, x1, mod3, g_post_ffn[l], ys, seq)

    out = x.reshape(nseq, seq, d)
    return (out[:nb_p], out[nb_p:])
```

```python
import functools

import jax
import jax.numpy as jnp
from jax import lax
from jax.experimental import pallas as pl
from jax.experimental.pallas import tpu as pltpu

MLA_HEADS = 16
MLA_NOPE = 128
MLA_ROPE = 64
MLA_V = 128
SWA_Q_HEADS = 32
SWA_KV_HEADS = 8
SWA_HEAD_DIM = 64
WINDOW = 128
TOP_K = 4
SWIGLU_LIMIT = 7.0
SWIGLU_ALPHA = 1.702
ROPE_THETA = 10000.0
RMS_EPS = 1e-6
N_MOD = 6

LANE = 128
VMEM_LIMIT = 56 << 20

TM_IN = 512
TN_IN = 1664
TM_POST = 256
TQ_MLA = 512
KV_CHUNK_MLA = 512
TQ_SWA = 256
TM_MERGE = 512
TN_MERGE = 1024
TM_OUT = 256
TM_POS = 256
TM_DMA = 256
BM_EXP = 512
TF_EXP = 512

F32 = jnp.float32
BF16 = jnp.bfloat16
NEG_INF = float("-inf")
LOG2E = 1.4426950408889634


def _params(sem):
    return pltpu.CompilerParams(dimension_semantics=sem, vmem_limit_bytes=VMEM_LIMIT)


def _rms(x, g):
    return x * lax.rsqrt(jnp.mean(x * x, axis=-1, keepdims=True) + RMS_EPS) * g


def _mod_kernel(c_ref, w_ref, b_ref, o_ref):
    c = c_ref[...]
    a = (c * jax.nn.sigmoid(c)).astype(BF16)
    o_ref[...] = jnp.dot(a, w_ref[...].astype(BF16), preferred_element_type=F32) + b_ref[...]


def _modulation(c8, w_ada, b_ada):
    d, n = w_ada.shape
    tn = 1024
    return pl.pallas_call(
        _mod_kernel,
        grid=(n // tn,),
        in_specs=[
            pl.BlockSpec((8, d), lambda j: (0, 0)),
            pl.BlockSpec((d, tn), lambda j: (0, j)),
            pl.BlockSpec((1, tn), lambda j: (0, j)),
        ],
        out_specs=pl.BlockSpec((8, tn), lambda j: (0, j)),
        out_shape=jax.ShapeDtypeStruct((8, n), F32),
        compiler_params=_params(("arbitrary",)),
        name="adaln_mod",
    )(c8, w_ada, b_ada.reshape(1, n))


def _inproj_kernel(x_ref, mod_ref, g_ref, w_ref, z_ref, h_ref):
    @pl.when(pl.program_id(1) == 0)
    def _():
        h = _rms(x_ref[...], g_ref[...]) * (1.0 + mod_ref[0, 1:2, :]) + mod_ref[0, 0:1, :]
        h_ref[...] = h.astype(BF16)

    z_ref[...] = jnp.dot(h_ref[...], w_ref[...], preferred_element_type=F32).astype(z_ref.dtype)


def _in_projection(x, mod3, g, w_in_p, seq):
    t, d = x.shape
    n = w_in_p.shape[1]
    tiles_per_seq = seq // TM_IN
    return pl.pallas_call(
        _inproj_kernel,
        grid=(t // TM_IN, n // TN_IN),
        in_specs=[
            pl.BlockSpec((TM_IN, d), lambda i, j: (i, 0)),
            pl.BlockSpec((1, N_MOD, d), lambda i, j: (i // tiles_per_seq, 0, 0)),
            pl.BlockSpec((1, d), lambda i, j: (0, 0)),
            pl.BlockSpec((d, TN_IN), lambda i, j: (0, j)),
        ],
        out_specs=pl.BlockSpec((TM_IN, TN_IN), lambda i, j: (i, j)),
        out_shape=jax.ShapeDtypeStruct((t, n), BF16),
        scratch_shapes=[pltpu.VMEM((TM_IN, d), BF16)],
        compiler_params=_params(("arbitrary", "arbitrary")),
        name="in_projection",
    )(x, mod3, g.reshape(1, d), w_in_p)


def _postproj_kernel(cq_ref, ckv_ref, qs_ref, ks_ref, vs_ref, kr_ref, cos_ref, sin_ref,
                     gq_ref, gkv_ref, wuq_ref, wukv_ref,
                     q_ref, k_ref, v_ref, qso_ref, klo_ref, khi_ref, vlo_ref, vhi_ref):
    tm = cq_ref.shape[0]
    cos = cos_ref[...]
    sin = sin_ref[...]
    lane = lax.broadcasted_iota(jnp.int32, (tm, LANE), 1)
    first_half = (lane & 63) < 32
    low = lane < 64

    def rope(x):
        rot = jnp.where(first_half, pltpu.roll(x, LANE - 32, 1), pltpu.roll(x, 32, 1))
        return x * cos + rot * sin

    nh = MLA_HEADS
    scale = float((MLA_NOPE + MLA_ROPE) ** -0.5) * LOG2E
    cqn = _rms(cq_ref[...].astype(F32), gq_ref[...]).astype(BF16)
    q = jnp.dot(cqn, wuq_ref[...], preferred_element_type=F32) * scale
    for h in range(nh):
        q_ref[0, h, :, 0:LANE] = q[:, h * LANE:(h + 1) * LANE].astype(BF16)
    for m in range(nh // 2):
        pe = rope(q[:, (nh + m) * LANE:(nh + m + 1) * LANE]).astype(BF16)
        q_ref[0, 2 * m, :, LANE:2 * LANE] = pe
        q_ref[0, 2 * m + 1, :, LANE:2 * LANE] = pe

    ckvn = _rms(ckv_ref[...].astype(F32), gkv_ref[...]).astype(BF16)
    kv = jnp.dot(ckvn, wukv_ref[...], preferred_element_type=F32)
    kr = rope(kr_ref[...].astype(F32))
    kpe_lo = jnp.where(low, kr, 0.0).astype(BF16)
    kpe_hi = jnp.where(low, 0.0, kr).astype(BF16)
    for h in range(nh):
        k_ref[0, h, :, 0:LANE] = kv[:, h * LANE:(h + 1) * LANE].astype(BF16)
        k_ref[0, h, :, LANE:2 * LANE] = kpe_lo if h % 2 == 0 else kpe_hi
        v_ref[0, h, :, :] = kv[:, (nh + h) * LANE:(nh + h + 1) * LANE].astype(BF16)

    swa_scale = float(SWA_HEAD_DIM ** -0.5) * LOG2E
    for m in range(SWA_Q_HEADS // 2):
        x = qs_ref[:, m * LANE:(m + 1) * LANE].astype(F32)
        qso_ref[:, m * LANE:(m + 1) * LANE] = (rope(x) * swa_scale).astype(BF16)

    for m in range(SWA_KV_HEADS // 2):
        sl = slice(m * LANE, (m + 1) * LANE)
        for src_ref, lo_ref, hi_ref, roped in ((ks_ref, klo_ref, khi_ref, True),
                                               (vs_ref, vlo_ref, vhi_ref, False)):
            a = src_ref[:, sl].astype(F32)
            if roped:
                a = rope(a)
            b = pltpu.roll(a, 64, 1)
            e0 = slice((2 * m) * LANE, (2 * m + 1) * LANE)
            e1 = slice((2 * m + 1) * LANE, (2 * m + 2) * LANE)
            lo_ref[:, e0] = jnp.where(low, a, 0.0).astype(BF16)
            hi_ref[:, e0] = jnp.where(low, 0.0, b).astype(BF16)
            lo_ref[:, e1] = jnp.where(low, b, 0.0).astype(BF16)
            hi_ref[:, e1] = jnp.where(low, 0.0, a).astype(BF16)


def _post_projection(z, cols, cos, sin, g_q, g_kv, w_uq_p, w_ukv_p, nseq, seq):
    t = z.shape[0]
    tm = TM_POST
    tps = seq // tm
    nh = MLA_HEADS
    qr, kvr = g_q.shape[0], g_kv.shape[0]
    qsw = SWA_Q_HEADS * SWA_HEAD_DIM
    ksw = SWA_KV_HEADS * SWA_HEAD_DIM

    def zspec(width, off):
        blk = off // width
        return pl.BlockSpec((tm, width), lambda i: (i, blk))

    head_map = lambda i: (i // tps, 0, i % tps, 0)
    tok_map = lambda i: (i, 0)
    out_shapes = (
        jax.ShapeDtypeStruct((nseq, nh, seq, 2 * LANE), BF16),
        jax.ShapeDtypeStruct((nseq, nh, seq, 2 * LANE), BF16),
        jax.ShapeDtypeStruct((nseq, nh, seq, LANE), BF16),
        jax.ShapeDtypeStruct((t, qsw), BF16),
        jax.ShapeDtypeStruct((t, SWA_KV_HEADS * LANE), BF16),
        jax.ShapeDtypeStruct((t, SWA_KV_HEADS * LANE), BF16),
        jax.ShapeDtypeStruct((t, SWA_KV_HEADS * LANE), BF16),
        jax.ShapeDtypeStruct((t, SWA_KV_HEADS * LANE), BF16),
    )
    out_specs = (
        pl.BlockSpec((1, nh, tm, 2 * LANE), head_map),
        pl.BlockSpec((1, nh, tm, 2 * LANE), head_map),
        pl.BlockSpec((1, nh, tm, LANE), head_map),
        pl.BlockSpec((tm, qsw), tok_map),
        pl.BlockSpec((tm, SWA_KV_HEADS * LANE), tok_map),
        pl.BlockSpec((tm, SWA_KV_HEADS * LANE), tok_map),
        pl.BlockSpec((tm, SWA_KV_HEADS * LANE), tok_map),
        pl.BlockSpec((tm, SWA_KV_HEADS * LANE), tok_map),
    )
    return pl.pallas_call(
        _postproj_kernel,
        grid=(t // tm,),
        in_specs=[
            zspec(qr, cols["c_q"]), zspec(kvr, cols["c_kv"]), zspec(qsw, cols["q_s"]),
            zspec(ksw, cols["k_s"]), zspec(ksw, cols["v_s"]), zspec(LANE, cols["k_rope"]),
            pl.BlockSpec((tm, LANE), lambda i: (i % tps, 0)),
            pl.BlockSpec((tm, LANE), lambda i: (i % tps, 0)),
            pl.BlockSpec((1, qr), lambda i: (0, 0)),
            pl.BlockSpec((1, kvr), lambda i: (0, 0)),
            pl.BlockSpec(w_uq_p.shape, lambda i: (0, 0)),
            pl.BlockSpec(w_ukv_p.shape, lambda i: (0, 0)),
        ],
        out_specs=out_specs,
        out_shape=out_shapes,
        compiler_params=_params(("arbitrary",)),
        name="post_projection",
    )(z, z, z, z, z, z, cos, sin, g_q.reshape(1, qr), g_kv.reshape(1, kvr), w_uq_p, w_ukv_p)


def _mla_kernel(q_ref, k_ref, v_ref, o_ref):
    q = q_ref[0, 0]
    seq = k_ref.shape[2]
    m = l = acc = None
    for c in range(seq // KV_CHUNK_MLA):
        rows = slice(c * KV_CHUNK_MLA, (c + 1) * KV_CHUNK_MLA)
        s = lax.dot_general(q, k_ref[0, 0, rows, :], (((1,), (1,)), ((), ())),
                            preferred_element_type=F32)
        mc = jnp.max(s, axis=-1, keepdims=True)
        m_new = mc if m is None else jnp.maximum(m, mc)
        p = jnp.exp2(s - m_new)
        ps = jnp.sum(p, axis=-1, keepdims=True)
        pv = jnp.dot(p.astype(BF16), v_ref[0, 0, rows, :], preferred_element_type=F32)
        if m is None:
            l, acc = ps, pv
        else:
            alpha = jnp.exp2(m - m_new)
            l = alpha * l + ps
            acc = alpha * acc + pv
        m = m_new
    o_ref[...] = (acc / l).astype(o_ref.dtype)


def _mla_attention(q, k, v):
    nseq, nh, seq, dqk = q.shape
    dv = v.shape[-1]
    nq = seq // TQ_MLA
    return pl.pallas_call(
        _mla_kernel,
        grid=(nseq, nh, nq),
        in_specs=[
            pl.BlockSpec((1, 1, TQ_MLA, dqk), lambda b, h, i: (b, h, i, 0)),
            pl.BlockSpec((1, 1, seq, dqk), lambda b, h, i: (b, h, 0, 0)),
            pl.BlockSpec((1, 1, seq, dv), lambda b, h, i: (b, h, 0, 0)),
        ],
        out_specs=pl.BlockSpec((TQ_MLA, dv), lambda b, h, i: (b * nq + i, h)),
        out_shape=jax.ShapeDtypeStruct((nseq * seq, nh * dv), BF16),
        compiler_params=_params(("arbitrary", "arbitrary", "arbitrary")),
        name="mla_attention",
    )(q, k, v)


def _swa_kernel(sink_ref, q_ref,
                klo_p, klo_c, klo_n, khi_p, khi_c, khi_n,
                vlo_p, vlo_c, vlo_n, vhi_p, vhi_c, vhi_n,
                o_ref, klo_w, khi_w, vlo_w, vhi_w, *, seq):
    tq = q_ref.shape[0]
    w = WINDOW
    nk = tq + 2 * w
    for win, (p, c, n) in ((klo_w, (klo_p, klo_c, klo_n)), (khi_w, (khi_p, khi_c, khi_n)),
                           (vlo_w, (vlo_p, vlo_c, vlo_n)), (vhi_w, (vhi_p, vhi_c, vhi_n))):
        win[0:w, :] = p[...]
        win[w:w + tq, :] = c[...]
        win[w + tq:nk, :] = n[...]

    q0 = pl.program_id(1) * tq
    row = lax.broadcasted_iota(jnp.int32, (tq, nk), 0)
    col = lax.broadcasted_iota(jnp.int32, (tq, nk), 1)
    rel = col - w - row
    kpos = q0 - w + col
    valid = (rel <= w) & (rel >= -w) & (kpos >= 0) & (kpos < seq)

    groups = SWA_Q_HEADS // SWA_KV_HEADS
    for m in range(SWA_Q_HEADS // 2):
        g = (2 * m) // groups
        gs = slice(g * LANE, (g + 1) * LANE)
        qp = q_ref[:, m * LANE:(m + 1) * LANE]
        acc = None
        for half, (kw, vw) in enumerate(((klo_w, vlo_w), (khi_w, vhi_w))):
            sink = sink_ref[2 * m + half] * LOG2E
            s = lax.dot_general(qp, kw[:, gs], (((1,), (1,)), ((), ())),
                                preferred_element_type=F32)
            s = jnp.where(valid, s, NEG_INF)
            mx = jnp.maximum(jnp.max(s, axis=-1, keepdims=True), sink)
            e = jnp.exp2(s - mx)
            den = jnp.sum(e, axis=-1, keepdims=True) + jnp.exp2(sink - mx)
            pv = jnp.dot(e.astype(BF16), vw[:, gs], preferred_element_type=F32) / den
            acc = pv if acc is None else acc + pv
        o_ref[:, m * LANE:(m + 1) * LANE] = acc.astype(o_ref.dtype)


def _swa_attention(sinks, qs, klo, khi, vlo, vhi, nseq, seq):
    t, qw = qs.shape
    kw = klo.shape[1]
    tq = TQ_SWA
    nq = seq // tq
    r = tq // WINDOW
    nwb = seq // WINDOW

    prev = pl.BlockSpec((WINDOW, kw), lambda b, i, s: (b * nwb + jnp.maximum(i * r - 1, 0), 0))
    cur = pl.BlockSpec((tq, kw), lambda b, i, s: (b * nq + i, 0))
    nxt = pl.BlockSpec((WINDOW, kw), lambda b, i, s: (b * nwb + jnp.minimum((i + 1) * r, nwb - 1), 0))
    grid_spec = pltpu.PrefetchScalarGridSpec(
        num_scalar_prefetch=1,
        grid=(nseq, nq),
        in_specs=[pl.BlockSpec((tq, qw), lambda b, i, s: (b * nq + i, 0))] + [prev, cur, nxt] * 4,
        out_specs=pl.BlockSpec((tq, qw), lambda b, i, s: (b * nq + i, 0)),
        scratch_shapes=[pltpu.VMEM((tq + 2 * WINDOW, kw), BF16)] * 4,
    )
    return pl.pallas_call(
        functools.partial(_swa_kernel, seq=seq),
        grid_spec=grid_spec,
        out_shape=jax.ShapeDtypeStruct((t, qw), BF16),
        compiler_params=_params(("arbitrary", "arbitrary")),
        name="swa_attention",
    )(sinks, qs, klo, klo, klo, khi, khi, khi, vlo, vlo, vlo, vhi, vhi, vhi)


def _merge_kernel(oa_ref, ob_ref, ga_ref, gb_ref, wa_ref, wb_ref, y_ref):
    a = jnp.dot(oa_ref[...], wa_ref[...], preferred_element_type=F32)
    b = jnp.dot(ob_ref[...], wb_ref[...], preferred_element_type=F32)
    y = jax.nn.sigmoid(ga_ref[...].astype(F32)) * a + jax.nn.sigmoid(gb_ref[...].astype(F32)) * b
    y_ref[...] = y.astype(y_ref.dtype)


def _merge(o_a, o_b, z, cols, w_a, w_b):
    t, d = o_a.shape[0], w_a.shape[1]
    tm, tn = TM_MERGE, TN_MERGE
    ga_blk, gb_blk = cols["g_a"] // tn, cols["g_b"] // tn
    return pl.pallas_call(
        _merge_kernel,
        grid=(t // tm, d // tn),
        in_specs=[
            pl.BlockSpec((tm, o_a.shape[1]), lambda i, j: (i, 0)),
            pl.BlockSpec((tm, o_b.shape[1]), lambda i, j: (i, 0)),
            pl.BlockSpec((tm, tn), lambda i, j: (i, ga_blk + j)),
            pl.BlockSpec((tm, tn), lambda i, j: (i, gb_blk + j)),
            pl.BlockSpec((w_a.shape[0], tn), lambda i, j: (0, j)),
            pl.BlockSpec((w_b.shape[0], tn), lambda i, j: (0, j)),
        ],
        out_specs=pl.BlockSpec((tm, tn), lambda i, j: (i, j)),
        out_shape=jax.ShapeDtypeStruct((t, d), BF16),
        compiler_params=_params(("arbitrary", "arbitrary")),
        name="branch_merge",
    )(o_a, o_b, z, z, w_a, w_b)


def _outproj_kernel(y_ref, x_ref, mod_ref, gpm_ref, gpf_ref, wo_ref, wr_ref, br_ref,
                    x1_ref, h2_ref, idx_ref, gate_ref, cnt_ref, *, n_experts):
    tm = y_ref.shape[0]
    u = jnp.dot(y_ref[...], wo_ref[...], preferred_element_type=F32)
    x1 = x_ref[...] + mod_ref[0, 2:3, :] * _rms(u, gpm_ref[...])
    x1_ref[...] = x1
    h2 = _rms(x1, gpf_ref[...]) * (1.0 + mod_ref[0, 4:5, :]) + mod_ref[0, 3:4, :]
    h2_ref[...] = h2.reshape(h2_ref.shape)

    logits = jnp.dot(h2.astype(BF16), wr_ref[...], preferred_element_type=F32) + br_ref[...]
    lane = lax.broadcasted_iota(jnp.int32, (tm, LANE), 1)
    lane_f = lane.astype(F32)
    cur = jnp.where(lane < n_experts, logits, NEG_INF)
    vals, idxs = [], []
    for _ in range(TOP_K):
        m = jnp.max(cur, axis=-1, keepdims=True)
        ix = jnp.min(jnp.where(cur == m, lane_f, float(LANE)), axis=-1, keepdims=True)
        vals.append(m)
        idxs.append(ix)
        cur = jnp.where(lane_f == ix, NEG_INF, cur)
    es = [jnp.exp(v - vals[0]) for v in vals]
    den = es[0]
    for e in es[1:]:
        den = den + e
    idx_out = jnp.zeros((tm, LANE), F32)
    gate_out = jnp.zeros((tm, LANE), F32)
    sel = jnp.zeros((tm, LANE), F32)
    for r in range(TOP_K):
        idx_out = jnp.where(lane == r, idxs[r], idx_out)
        gate_out = jnp.where(lane == r, es[r] / den, gate_out)
        sel = sel + jnp.where(lane_f == idxs[r], 1.0, 0.0)
    idx_ref[...] = idx_out.astype(jnp.int32)
    gate_ref[...] = gate_out

    @pl.when(pl.program_id(0) == 0)
    def _():
        cnt_ref[...] = jnp.zeros_like(cnt_ref)

    cnt_ref[0:1, :] += jnp.sum(sel, axis=0, keepdims=True)


def _out_projection(y, x, mod3, g_pm, g_pf, w_out, w_router_p, b_router_p, n_experts, seq):
    t, d = x.shape
    tm = TM_OUT
    tps = seq // tm
    tok = lambda i: (i, 0)
    const = lambda i: (0, 0)
    return pl.pallas_call(
        functools.partial(_outproj_kernel, n_experts=n_experts),
        grid=(t // tm,),
        in_specs=[
            pl.BlockSpec((tm, d), tok),
            pl.BlockSpec((tm, d), tok),
            pl.BlockSpec((1, N_MOD, d), lambda i: (i // tps, 0, 0)),
            pl.BlockSpec((1, d), const),
            pl.BlockSpec((1, d), const),
            pl.BlockSpec((d, d), const),
            pl.BlockSpec((d, LANE), const),
            pl.BlockSpec((1, LANE), const),
        ],
        out_specs=(
            pl.BlockSpec((tm, d), tok),
            pl.BlockSpec((tm, 1, d), lambda i: (i, 0, 0)),
            pl.BlockSpec((tm, LANE), tok),
            pl.BlockSpec((tm, LANE), tok),
            pl.BlockSpec((8, LANE), const),
        ),
        out_shape=(
            jax.ShapeDtypeStruct((t, d), F32),
            jax.ShapeDtypeStruct((t, 1, d), F32),
            jax.ShapeDtypeStruct((t, LANE), jnp.int32),
            jax.ShapeDtypeStruct((t, LANE), F32),
            jax.ShapeDtypeStruct((8, LANE), F32),
        ),
        compiler_params=_params(("arbitrary",)),
        name="out_projection_router",
    )(y, x, mod3, g_pm.reshape(1, d), g_pf.reshape(1, d), w_out, w_router_p, b_router_p)


def _positions_kernel(idx_ref, cnt_ref, dest_ref, meta_ref, zst_ref, carry_ref, ltri_ref, pst_ref,
                      *, n_experts, bm):
    tm = idx_ref.shape[0]
    nbp = meta_ref.shape[0]

    @pl.when(pl.program_id(0) == 0)
    def _():
        r = lax.broadcasted_iota(jnp.int32, (tm, tm), 0)
        c = lax.broadcasted_iota(jnp.int32, (tm, tm), 1)
        ltri_ref[...] = jnp.where(c < r, 1.0, 0.0).astype(BF16)
        carry_ref[...] = jnp.zeros_like(carry_ref)
        nblk = jnp.floor((cnt_ref[...] + float(bm - 1)) * (1.0 / bm))
        ur = lax.broadcasted_iota(jnp.int32, (LANE, LANE), 0)
        uc = lax.broadcasted_iota(jnp.int32, (LANE, LANE), 1)
        upper = jnp.where(ur <= uc, 1.0, 0.0).astype(BF16)
        pend = jnp.dot(nblk.astype(BF16), upper, preferred_element_type=F32)
        pst_ref[...] = (pend - nblk) * float(bm)
        lane8 = lax.broadcasted_iota(jnp.int32, (8, LANE), 1)
        zst_ref[...] = jnp.maximum(pend * float(bm) - float(bm), 0.0).astype(jnp.int32)
        pend0 = pend[0:1, :]
        n_used = jnp.sum(jnp.where(lane8[0:1, :] == n_experts - 1, pend0, 0.0), axis=-1, keepdims=True)
        blk = lax.broadcasted_iota(jnp.int32, (nbp, LANE), 0).astype(F32)
        lane = lax.broadcasted_iota(jnp.int32, (nbp, LANE), 1)
        passed = jnp.where((lane < n_experts) & (pend0 <= blk), 1.0, 0.0)
        blk_e = jnp.minimum(jnp.sum(passed, axis=-1, keepdims=True), float(n_experts - 1))
        meta = jnp.where(lane == 0, blk_e, jnp.where(lane == 1, n_used, 0.0))
        meta_ref[...] = meta.astype(jnp.int32)

    lane = lax.broadcasted_iota(jnp.int32, (tm, LANE), 1)
    idx = idx_ref[...]
    hots = []
    sel = jnp.zeros((tm, LANE), F32)
    for r in range(TOP_K):
        col = jnp.sum(jnp.where(lane == r, idx, 0).astype(F32), axis=-1, keepdims=True)
        hot = jnp.where(lane.astype(F32) == col, 1.0, 0.0)
        hots.append(hot)
        sel = sel + hot
    rank = jnp.dot(ltri_ref[...], sel.astype(BF16), preferred_element_type=F32) + carry_ref[0:1, :]
    pos = pst_ref[0:1, :] + rank
    dest = jnp.zeros((tm, LANE), F32)
    for r in range(TOP_K):
        d = jnp.sum(hots[r] * pos, axis=-1, keepdims=True)
        dest = jnp.where(lane == r, d, dest)
    dest_ref[...] = dest.T[0:8, :].astype(jnp.int32)
    carry_ref[0:1, :] += jnp.sum(sel, axis=0, keepdims=True)


def _positions(idx, cnt, n_experts, n_blocks):
    t = idx.shape[0]
    tm = TM_POS
    nbp = -(-n_blocks // 8) * 8
    return pl.pallas_call(
        functools.partial(_positions_kernel, n_experts=n_experts, bm=BM_EXP),
        grid=(t // tm,),
        in_specs=[pl.BlockSpec((tm, LANE), lambda i: (i, 0)),
                  pl.BlockSpec((8, LANE), lambda i: (0, 0))],
        out_specs=(pl.BlockSpec((8, tm), lambda i: (0, i)),
                   pl.BlockSpec((nbp, LANE), lambda i: (0, 0)),
                   pl.BlockSpec((8, LANE), lambda i: (0, 0))),
        out_shape=(jax.ShapeDtypeStruct((8, t), jnp.int32),
                   jax.ShapeDtypeStruct((nbp, LANE), jnp.int32),
                   jax.ShapeDtypeStruct((8, LANE), jnp.int32)),
        scratch_shapes=[pltpu.VMEM((8, LANE), F32), pltpu.VMEM((tm, tm), BF16), pltpu.VMEM((8, LANE), F32)],
        compiler_params=_params(("arbitrary",)),
        name="dispatch_positions",
    )(idx, cnt)


def _dispatch_kernel(zst_ref, dest_ref, h2_ref, xs_ref, zero_ref, zsem, sem, *, n_experts, bm):
    i = pl.program_id(0)
    tm = dest_ref.shape[1]

    def zero_copy(e):
        return pltpu.make_async_copy(zero_ref, xs_ref.at[pl.ds(zst_ref[e], bm)], zsem)

    def tail_copy(j):
        return pltpu.make_async_copy(zero_ref, xs_ref.at[pl.ds(j * bm, bm)], zsem)

    @pl.when(i == 0)
    def _():
        zero_ref[...] = jnp.zeros_like(zero_ref)
        n_used = zst_ref[n_experts]
        n_blocks = xs_ref.shape[0] // bm
        for e in range(n_experts):
            zero_copy(e).start()
        lax.fori_loop(n_used, n_blocks, lambda j, c: (tail_copy(j).start(), c)[1], 0)
        for e in range(n_experts):
            zero_copy(e).wait()
        lax.fori_loop(n_used, n_blocks, lambda j, c: (tail_copy(j).wait(), c)[1], 0)

    def body(j, carry):
        for r in range(TOP_K):
            d = dest_ref[r, j]
            pltpu.make_async_copy(h2_ref.at[j], xs_ref.at[d], sem).start()
        return carry

    lax.fori_loop(0, tm, body, 0, unroll=8)
    for _ in range(TOP_K):
        pltpu.make_async_copy(h2_ref, xs_ref.at[pl.ds(0, tm)], sem).wait()


def _dispatch(zst, dest, h2, n_experts, n_blocks):
    t, _, d = h2.shape
    grid_spec = pltpu.PrefetchScalarGridSpec(
        num_scalar_prefetch=1,
        grid=(t // TM_DMA,),
        in_specs=[pl.BlockSpec((8, TM_DMA), lambda i, z: (0, i), memory_space=pltpu.SMEM),
                  pl.BlockSpec((TM_DMA, 1, d), lambda i, z: (i, 0, 0))],
        out_specs=pl.BlockSpec(memory_space=pl.ANY),
        scratch_shapes=[pltpu.VMEM((BM_EXP, 1, d), F32),
                        pltpu.SemaphoreType.DMA(()), pltpu.SemaphoreType.DMA(())],
    )
    return pl.pallas_call(
        functools.partial(_dispatch_kernel, n_experts=n_experts, bm=BM_EXP),
        grid_spec=grid_spec,
        out_shape=jax.ShapeDtypeStruct((n_blocks * BM_EXP, 1, d), F32),
        compiler_params=_params(("arbitrary",)),
        name="row_dispatch",
    )(zst, dest, h2)


def _expert_kernel(sp_ref, xs_ref, wg_ref, wl_ref, bg_ref, bl_ref, wd_ref, bd_ref, ys_ref,
                   x2d_ref, xb_ref, acc_ref, *, n_blocks):
    j = pl.program_id(0)
    f = pl.program_id(1)
    nf = pl.num_programs(1)
    used = j < sp_ref[n_blocks]

    @pl.when(used & (f == 0))
    def _():
        x2d_ref[...] = xs_ref[...].reshape(x2d_ref.shape)
        xb_ref[...] = x2d_ref[...].astype(BF16)
        acc_ref[...] = jnp.broadcast_to(bd_ref[0], acc_ref.shape)

    @pl.when(jnp.logical_not(used) & (f == 0))
    def _():
        ys_ref[...] = jnp.zeros_like(ys_ref)

    @pl.when(used)
    def _():
        xb = xb_ref[...]
        glu = jnp.dot(xb, wg_ref[0], preferred_element_type=F32) + bg_ref[0]
        lin = jnp.dot(xb, wl_ref[0], preferred_element_type=F32) + bl_ref[0]
        glu = jnp.minimum(glu, SWIGLU_LIMIT)
        lin = jnp.clip(lin, -SWIGLU_LIMIT, SWIGLU_LIMIT)
        act = glu * jax.nn.sigmoid(SWIGLU_ALPHA * glu) * (lin + 1.0)
        acc_ref[...] += jnp.dot(act.astype(BF16), wd_ref[0], preferred_element_type=F32)

    @pl.when(used & (f == nf - 1))
    def _():
        ys_ref[...] = acc_ref[...].reshape(ys_ref.shape)


def _experts(sp, xs, w_gu, b_gu, w_dn, b_dn, n_blocks):
    p, _, d = xs.shape
    n_experts, _, ff2 = w_gu.shape
    ff = ff2 // 2
    nf = ff // TF_EXP
    bm = BM_EXP

    def blk(j, s):
        return jnp.minimum(j, s[n_blocks] - 1)

    def fch(j, f, s):
        return jnp.where(j < s[n_blocks], f, nf - 1)

    grid_spec = pltpu.PrefetchScalarGridSpec(
        num_scalar_prefetch=1,
        grid=(n_blocks, nf),
        in_specs=[
            pl.BlockSpec((bm, 1, d), lambda j, f, s: (blk(j, s), 0, 0)),
            pl.BlockSpec((1, d, TF_EXP), lambda j, f, s: (s[blk(j, s)], 0, fch(j, f, s))),
            pl.BlockSpec((1, d, TF_EXP), lambda j, f, s: (s[blk(j, s)], 0, nf + fch(j, f, s))),
            pl.BlockSpec((1, 1, TF_EXP), lambda j, f, s: (s[blk(j, s)], 0, fch(j, f, s))),
            pl.BlockSpec((1, 1, TF_EXP), lambda j, f, s: (s[blk(j, s)], 0, nf + fch(j, f, s))),
            pl.BlockSpec((1, TF_EXP, d), lambda j, f, s: (s[blk(j, s)], fch(j, f, s), 0)),
            pl.BlockSpec((1, 1, d), lambda j, f, s: (s[blk(j, s)], 0, 0)),
        ],
        out_specs=pl.BlockSpec((bm, 1, d), lambda j, f, s: (j, 0, 0)),
        scratch_shapes=[pltpu.VMEM((bm, d), F32), pltpu.VMEM((bm, d), BF16), pltpu.VMEM((bm, d), F32)],
    )
    return pl.pallas_call(
        functools.partial(_expert_kernel, n_blocks=n_blocks),
        grid_spec=grid_spec,
        out_shape=jax.ShapeDtypeStruct((p, 1, d), F32),
        compiler_params=_params(("arbitrary", "arbitrary")),
        name="expert_mlp",
    )(sp, xs, w_gu, w_gu, b_gu.reshape(n_experts, 1, ff2), b_gu.reshape(n_experts, 1, ff2),
      w_dn, b_dn.reshape(n_experts, 1, d))


def _combine_kernel(dest_ref, gate_ref, x1_ref, mod_ref, g_ref, ys_ref, oa_ref, ob_ref,
                    buf_ref, row_ref, sem, *, first_tiles):
    tm = x1_ref.shape[0]

    def body(j, carry):
        for r in range(TOP_K):
            d = dest_ref[r, j]
            pltpu.make_async_copy(ys_ref.at[d], buf_ref.at[r, j], sem).start()
        return carry

    lax.fori_loop(0, tm, body, 0, unroll=8)
    for r in range(TOP_K):
        pltpu.make_async_copy(ys_ref.at[pl.ds(0, tm)], buf_ref.at[r], sem).wait()

    gates = gate_ref[...]
    lane = lax.broadcasted_iota(jnp.int32, gates.shape, 1)
    f = None
    for r in range(TOP_K):
        g = jnp.sum(jnp.where(lane == r, gates, 0.0), axis=-1, keepdims=True)
        row_ref[...] = buf_ref[r].reshape(row_ref.shape)
        term = row_ref[...] * g
        f = term if f is None else f + term
    out = x1_ref[...] + mod_ref[0, 5:6, :] * _rms(f, g_ref[...])

    @pl.when(pl.program_id(0) < first_tiles)
    def _():
        oa_ref[...] = out

    @pl.when(pl.program_id(0) >= first_tiles)
    def _():
        ob_ref[...] = out


def _combine(dest, gates, x1, mod3, g_post, ys, seq, n_first):
    t, d = x1.shape
    tm = TM_DMA
    tps = seq // tm
    first_tiles = n_first * tps
    return pl.pallas_call(
        functools.partial(_combine_kernel, first_tiles=first_tiles),
        grid=(t // tm,),
        in_specs=[
            pl.BlockSpec((8, tm), lambda i: (0, i), memory_space=pltpu.SMEM),
            pl.BlockSpec((tm, LANE), lambda i: (i, 0)),
            pl.BlockSpec((tm, d), lambda i: (i, 0)),
            pl.BlockSpec((1, N_MOD, d), lambda i: (i // tps, 0, 0)),
            pl.BlockSpec((1, d), lambda i: (0, 0)),
            pl.BlockSpec(memory_space=pl.ANY),
        ],
        out_specs=(pl.BlockSpec((tm, d), lambda i: (jnp.minimum(i, first_tiles - 1), 0)),
                   pl.BlockSpec((tm, d), lambda i: (jnp.maximum(i - first_tiles, 0), 0))),
        out_shape=(jax.ShapeDtypeStruct((first_tiles * tm, d), F32),
                   jax.ShapeDtypeStruct((t - first_tiles * tm, d), F32)),
        scratch_shapes=[pltpu.VMEM((TOP_K, tm, 1, d), F32), pltpu.VMEM((tm, d), F32),
                        pltpu.SemaphoreType.DMA(())],
        compiler_params=_params(("arbitrary",)),
        name="expert_combine",
    )(dest, gates, x1, mod3, g_post.reshape(1, d), ys)


def _layout_w_in(w_in, qr, kvr):
    qsw = SWA_Q_HEADS * SWA_HEAD_DIM
    ksw = SWA_KV_HEADS * SWA_HEAD_DIM
    d = w_in.shape[0]
    sizes = (qr, kvr, MLA_ROPE, qsw, ksw, ksw, d, d)
    offs = [0]
    for s in sizes:
        offs.append(offs[-1] + s)
    c_q, c_kv, k_rope, q_s, k_s, v_s, g_a, g_b = [w_in[:, offs[i]:offs[i + 1]] for i in range(8)]
    parts = (("q_s", q_s), ("g_a", g_a), ("g_b", g_b), ("c_q", c_q), ("c_kv", c_kv),
             ("k_s", k_s), ("v_s", v_s), ("k_rope", jnp.concatenate([k_rope, k_rope], axis=1)))
    cols, off = {}, 0
    for name, p in parts:
        cols[name] = off
        off += p.shape[1]
    return jnp.concatenate([p for _, p in parts], axis=1).astype(BF16), cols


def _layout_w_uq(w_uq):
    r = w_uq.shape[0]
    w = w_uq.reshape(r, MLA_HEADS, MLA_NOPE + MLA_ROPE)
    nope = w[:, :, :MLA_NOPE].reshape(r, MLA_HEADS * MLA_NOPE)
    pe = w[:, :, MLA_NOPE:].reshape(r, MLA_HEADS * MLA_ROPE)
    return jnp.concatenate([nope, pe], axis=1).astype(BF16)


def _layout_w_ukv(w_ukv):
    r = w_ukv.shape[0]
    w = w_ukv.reshape(r, MLA_HEADS, MLA_NOPE + MLA_V)
    kn = w[:, :, :MLA_NOPE].reshape(r, MLA_HEADS * MLA_NOPE)
    v = w[:, :, MLA_NOPE:].reshape(r, MLA_HEADS * MLA_V)
    return jnp.concatenate([kn, v], axis=1).astype(BF16)


def _rope_tables(seq):
    half = MLA_ROPE // 2
    freqs = jnp.power(ROPE_THETA, -2.0 * jnp.arange(half, dtype=F32) / MLA_ROPE)
    ang = jnp.arange(seq, dtype=F32)[:, None] * freqs[None, :]
    cos, sin = jnp.cos(ang), jnp.sin(ang)
    return (jnp.concatenate([cos, cos, cos, cos], axis=1),
            jnp.concatenate([-sin, sin, -sin, sin], axis=1))


def kernel(x_prompt, x_sample, c_prompt, c_sample, w_ada, b_ada, g_pre_mix, w_in, g_q_lat, w_uq,
           g_kv_lat, w_ukv, attn_sinks, w_br_mla, w_br_swa, w_out, g_post_mix, g_pre_ffn,
           w_router, b_router, w_gu, b_gu, w_dn, b_dn, g_post_ffn):
    assert MLA_ROPE == SWA_HEAD_DIM == 64 and MLA_NOPE == MLA_V == LANE
    nb_p, seq, d = x_prompt.shape
    nb_s = x_sample.shape[0]
    assert x_sample.shape[1] == seq
    nseq = nb_p + nb_s
    t = nseq * seq
    n_experts = w_router.shape[-1]
    n_blocks = t * TOP_K // BM_EXP + n_experts

    xa, xb = x_prompt.reshape(nb_p * seq, d), x_sample.reshape(nb_s * seq, d)
    c8 = jnp.concatenate([c_prompt, c_sample, jnp.zeros((8 - nseq, d), F32)], axis=0)
    cos, sin = _rope_tables(seq)

    for l in range(w_ada.shape[0]):
        x = jnp.concatenate([xa, xb], axis=0)
        w_in_p, cols = _layout_w_in(w_in[l], g_q_lat.shape[-1], g_kv_lat.shape[-1])
        w_uq_p = _layout_w_uq(w_uq[l])
        w_ukv_p = _layout_w_ukv(w_ukv[l])
        w_router_p = jnp.pad(w_router[l], ((0, 0), (0, LANE - n_experts))).astype(BF16)
        b_router_p = jnp.pad(b_router[l], (0, LANE - n_experts)).reshape(1, LANE)

        mod = _modulation(c8, w_ada[l], b_ada[l])
        mod3 = mod[:nseq].reshape(nseq, N_MOD, d)
        z = _in_projection(x, mod3, g_pre_mix[l], w_in_p, seq)
        q, k, v, qs, klo, khi, vlo, vhi = _post_projection(
            z, cols, cos, sin, g_q_lat[l], g_kv_lat[l], w_uq_p, w_ukv_p, nseq, seq)
        o_a = _mla_attention(q, k, v)
        o_b = _swa_attention(attn_sinks[l], qs, klo, khi, vlo, vhi, nseq, seq)
        y = _merge(o_a, o_b, z, cols, w_br_mla[l].astype(BF16), w_br_swa[l].astype(BF16))
        x1, h2, idx, gates, cnt = _out_projection(
            y, x, mod3, g_post_mix[l], g_pre_ffn[l], w_out[l].astype(BF16), w_router_p, b_router_p,
            n_experts, seq)
        dest, meta, zst = _positions(idx, cnt, n_experts, n_blocks)
        sp = jnp.concatenate([meta[:n_blocks, 0], meta[0:1, 1]])
        zmeta = jnp.concatenate([zst[0, :n_experts], meta[0:1, 1]])
        xs = _dispatch(zmeta, dest, h2, n_experts, n_blocks)
        ys = _experts(sp, xs, w_gu[l].astype(BF16), b_gu[l], w_dn[l].astype(BF16), b_dn[l], n_blocks)
        xa, xb = _combine(dest, gates, x1, mod3, g_post_ffn[l], ys, seq, nb_p)

    return (xa.reshape(nb_p, seq, d), xb.reshape(nb_s, seq, d))
```

```python
import functools

import jax
import jax.numpy as jnp
from jax import lax
from jax.experimental import pallas as pl
from jax.experimental.pallas import tpu as pltpu

MLA_HEADS = 16
MLA_NOPE = 128
MLA_ROPE = 64
MLA_V = 128
SWA_Q_HEADS = 32
SWA_KV_HEADS = 8
SWA_HEAD_DIM = 64
WINDOW = 128
TOP_K = 4
SWIGLU_LIMIT = 7.0
SWIGLU_ALPHA = 1.702
ROPE_THETA = 10000.0
RMS_EPS = 1e-6
N_MOD = 6

LANE = 128
VMEM_LIMIT = 56 << 20

TM_IN = 512
TN_IN = 1664
TM_POST = 256
TQ_MLA = 1024
KV_CHUNK_MLA = 1024
TQ_SWA = 256
TM_MERGE = 512
TN_MERGE = 1024
TM_OUT = 256
TM_POS = 256
TM_DMA = 256
BM_EXP = 512
TF_EXP = 1024

F32 = jnp.float32
BF16 = jnp.bfloat16
NEG_INF = float("-inf")
LOG2E = 1.4426950408889634


def _params(sem):
    return pltpu.CompilerParams(dimension_semantics=sem, vmem_limit_bytes=VMEM_LIMIT)


def _rms(x, g):
    return x * lax.rsqrt(jnp.mean(x * x, axis=-1, keepdims=True) + RMS_EPS) * g


def _mod_kernel(c_ref, w_ref, b_ref, o_ref):
    c = c_ref[...]
    a = (c * jax.nn.sigmoid(c)).astype(BF16)
    o_ref[...] = jnp.dot(a, w_ref[...].astype(BF16), preferred_element_type=F32) + b_ref[...]


def _modulation(c8, w_ada, b_ada):
    d, n = w_ada.shape
    tn = 1024
    return pl.pallas_call(
        _mod_kernel,
        grid=(n // tn,),
        in_specs=[
            pl.BlockSpec((8, d), lambda j: (0, 0)),
            pl.BlockSpec((d, tn), lambda j: (0, j)),
            pl.BlockSpec((1, tn), lambda j: (0, j)),
        ],
        out_specs=pl.BlockSpec((8, tn), lambda j: (0, j)),
        out_shape=jax.ShapeDtypeStruct((8, n), F32),
        compiler_params=_params(("arbitrary",)),
        name="adaln_mod",
    )(c8, w_ada, b_ada.reshape(1, n))


def _two_group_specs(tm, d, first_tiles, n_grid_axes):
    if n_grid_axes == 1:
        return (pl.BlockSpec((tm, d), lambda i: (jnp.minimum(i, first_tiles - 1), 0)),
                pl.BlockSpec((tm, d), lambda i: (jnp.maximum(i - first_tiles, 0), 0)))
    return (pl.BlockSpec((tm, d), lambda i, j: (jnp.minimum(i, first_tiles - 1), 0)),
            pl.BlockSpec((tm, d), lambda i, j: (jnp.maximum(i - first_tiles, 0), 0)))


def _inproj_kernel(xa_ref, xb_ref, mod_ref, g_ref, w_ref, z_ref, h_ref, *, first_tiles):
    @pl.when(pl.program_id(1) == 0)
    def _():
        x = jnp.where(pl.program_id(0) < first_tiles, xa_ref[...], xb_ref[...])
        h = _rms(x, g_ref[...]) * (1.0 + mod_ref[0, 1:2, :]) + mod_ref[0, 0:1, :]
        h_ref[...] = h.astype(BF16)

    z_ref[...] = jnp.dot(h_ref[...], w_ref[...], preferred_element_type=F32).astype(z_ref.dtype)


def _in_projection(xa, xb, mod3, g, w_in_p, seq):
    d = xa.shape[1]
    t = xa.shape[0] + xb.shape[0]
    n = w_in_p.shape[1]
    tiles_per_seq = seq // TM_IN
    first_tiles = xa.shape[0] // TM_IN
    return pl.pallas_call(
        functools.partial(_inproj_kernel, first_tiles=first_tiles),
        grid=(t // TM_IN, n // TN_IN),
        in_specs=[
            *_two_group_specs(TM_IN, d, first_tiles, 2),
            pl.BlockSpec((1, N_MOD, d), lambda i, j: (i // tiles_per_seq, 0, 0)),
            pl.BlockSpec((1, d), lambda i, j: (0, 0)),
            pl.BlockSpec((d, TN_IN), lambda i, j: (0, j)),
        ],
        out_specs=pl.BlockSpec((TM_IN, TN_IN), lambda i, j: (i, j)),
        out_shape=jax.ShapeDtypeStruct((t, n), BF16),
        scratch_shapes=[pltpu.VMEM((TM_IN, d), BF16)],
        compiler_params=_params(("arbitrary", "arbitrary")),
        name="in_projection",
    )(xa, xb, mod3, g.reshape(1, d), w_in_p)


def _postproj_kernel(cq_ref, ckv_ref, qs_ref, ks_ref, vs_ref, kr_ref, cos_ref, sin_ref,
                     gq_ref, gkv_ref, wuq_ref, wukv_ref,
                     q_ref, k_ref, v_ref, qso_ref, klo_ref, khi_ref, vlo_ref, vhi_ref):
    tm = cq_ref.shape[0]
    cos = cos_ref[...]
    sin = sin_ref[...]
    lane = lax.broadcasted_iota(jnp.int32, (tm, LANE), 1)
    first_half = (lane & 63) < 32
    low = lane < 64

    def rope(x):
        rot = jnp.where(first_half, pltpu.roll(x, LANE - 32, 1), pltpu.roll(x, 32, 1))
        return x * cos + rot * sin

    nh = MLA_HEADS
    scale = float((MLA_NOPE + MLA_ROPE) ** -0.5) * LOG2E
    cqn = _rms(cq_ref[...].astype(F32), gq_ref[...]).astype(BF16)
    q = jnp.dot(cqn, wuq_ref[...], preferred_element_type=F32) * scale
    for h in range(nh):
        q_ref[0, h, :, 0:LANE] = q[:, h * LANE:(h + 1) * LANE].astype(BF16)
    for m in range(nh // 2):
        pe = rope(q[:, (nh + m) * LANE:(nh + m + 1) * LANE]).astype(BF16)
        q_ref[0, 2 * m, :, LANE:2 * LANE] = pe
        q_ref[0, 2 * m + 1, :, LANE:2 * LANE] = pe

    ckvn = _rms(ckv_ref[...].astype(F32), gkv_ref[...]).astype(BF16)
    kv = jnp.dot(ckvn, wukv_ref[...], preferred_element_type=F32)
    kr = rope(kr_ref[...].astype(F32))
    kpe_lo = jnp.where(low, kr, 0.0).astype(BF16)
    kpe_hi = jnp.where(low, 0.0, kr).astype(BF16)
    for h in range(nh):
        k_ref[0, h, :, 0:LANE] = kv[:, h * LANE:(h + 1) * LANE].astype(BF16)
        k_ref[0, h, :, LANE:2 * LANE] = kpe_lo if h % 2 == 0 else kpe_hi
        v_ref[0, h, :, :] = kv[:, (nh + h) * LANE:(nh + h + 1) * LANE].astype(BF16)

    swa_scale = float(SWA_HEAD_DIM ** -0.5) * LOG2E
    for m in range(SWA_Q_HEADS // 2):
        x = qs_ref[:, m * LANE:(m + 1) * LANE].astype(F32)
        qso_ref[:, m * LANE:(m + 1) * LANE] = (rope(x) * swa_scale).astype(BF16)

    for m in range(SWA_KV_HEADS // 2):
        sl = slice(m * LANE, (m + 1) * LANE)
        for src_ref, lo_ref, hi_ref, roped in ((ks_ref, klo_ref, khi_ref, True),
                                               (vs_ref, vlo_ref, vhi_ref, False)):
            a = src_ref[:, sl].astype(F32)
            if roped:
                a = rope(a)
            b = pltpu.roll(a, 64, 1)
            e0 = slice((2 * m) * LANE, (2 * m + 1) * LANE)
            e1 = slice((2 * m + 1) * LANE, (2 * m + 2) * LANE)
            lo_ref[:, e0] = jnp.where(low, a, 0.0).astype(BF16)
            hi_ref[:, e0] = jnp.where(low, 0.0, b).astype(BF16)
            lo_ref[:, e1] = jnp.where(low, b, 0.0).astype(BF16)
            hi_ref[:, e1] = jnp.where(low, 0.0, a).astype(BF16)


def _post_projection(z, cols, cos, sin, g_q, g_kv, w_uq_p, w_ukv_p, nseq, seq):
    t = z.shape[0]
    tm = TM_POST
    tps = seq // tm
    nh = MLA_HEADS
    qr, kvr = g_q.shape[0], g_kv.shape[0]
    qsw = SWA_Q_HEADS * SWA_HEAD_DIM
    ksw = SWA_KV_HEADS * SWA_HEAD_DIM

    def zspec(width, off):
        blk = off // width
        return pl.BlockSpec((tm, width), lambda i: (i, blk))

    head_map = lambda i: (i // tps, 0, i % tps, 0)
    tok_map = lambda i: (i, 0)
    out_shapes = (
        jax.ShapeDtypeStruct((nseq, nh, seq, 2 * LANE), BF16),
        jax.ShapeDtypeStruct((nseq, nh, seq, 2 * LANE), BF16),
        jax.ShapeDtypeStruct((nseq, nh, seq, LANE), BF16),
        jax.ShapeDtypeStruct((t, qsw), BF16),
        jax.ShapeDtypeStruct((t, SWA_KV_HEADS * LANE), BF16),
        jax.ShapeDtypeStruct((t, SWA_KV_HEADS * LANE), BF16),
        jax.ShapeDtypeStruct((t, SWA_KV_HEADS * LANE), BF16),
        jax.ShapeDtypeStruct((t, SWA_KV_HEADS * LANE), BF16),
    )
    out_specs = (
        pl.BlockSpec((1, nh, tm, 2 * LANE), head_map),
        pl.BlockSpec((1, nh, tm, 2 * LANE), head_map),
        pl.BlockSpec((1, nh, tm, LANE), head_map),
        pl.BlockSpec((tm, qsw), tok_map),
        pl.BlockSpec((tm, SWA_KV_HEADS * LANE), tok_map),
        pl.BlockSpec((tm, SWA_KV_HEADS * LANE), tok_map),
        pl.BlockSpec((tm, SWA_KV_HEADS * LANE), tok_map),
        pl.BlockSpec((tm, SWA_KV_HEADS * LANE), tok_map),
    )
    return pl.pallas_call(
        _postproj_kernel,
        grid=(t // tm,),
        in_specs=[
            zspec(qr, cols["c_q"]), zspec(kvr, cols["c_kv"]), zspec(qsw, cols["q_s"]),
            zspec(ksw, cols["k_s"]), zspec(ksw, cols["v_s"]), zspec(LANE, cols["k_rope"]),
            pl.BlockSpec((tm, LANE), lambda i: (i % tps, 0)),
            pl.BlockSpec((tm, LANE), lambda i: (i % tps, 0)),
            pl.BlockSpec((1, qr), lambda i: (0, 0)),
            pl.BlockSpec((1, kvr), lambda i: (0, 0)),
            pl.BlockSpec(w_uq_p.shape, lambda i: (0, 0)),
            pl.BlockSpec(w_ukv_p.shape, lambda i: (0, 0)),
        ],
        out_specs=out_specs,
        out_shape=out_shapes,
        compiler_params=_params(("arbitrary",)),
        name="post_projection",
    )(z, z, z, z, z, z, cos, sin, g_q.reshape(1, qr), g_kv.reshape(1, kvr), w_uq_p, w_ukv_p)


def _mla_kernel(q_ref, k_ref, v_ref, wgu_ref, wdn_ref, o_ref, wgu_o_ref, wdn_o_ref):
    wgu_o_ref[...] = wgu_ref[...].astype(BF16)
    wdn_o_ref[...] = wdn_ref[...].astype(BF16)

    q = q_ref[0, 0]
    seq = k_ref.shape[2]
    m = l = acc = None
    for c in range(seq // KV_CHUNK_MLA):
        rows = slice(c * KV_CHUNK_MLA, (c + 1) * KV_CHUNK_MLA)
        s = lax.dot_general(q, k_ref[0, 0, rows, :], (((1,), (1,)), ((), ())),
                            preferred_element_type=F32)
        mc = jnp.max(s, axis=-1, keepdims=True)
        m_new = mc if m is None else jnp.maximum(m, mc)
        p = jnp.exp2(s - m_new)
        ps = jnp.sum(p, axis=-1, keepdims=True)
        pv = jnp.dot(p.astype(BF16), v_ref[0, 0, rows, :], preferred_element_type=F32)
        if m is None:
            l, acc = ps, pv
        else:
            alpha = jnp.exp2(m - m_new)
            l = alpha * l + ps
            acc = alpha * acc + pv
        m = m_new
    o_ref[...] = (acc / l).astype(o_ref.dtype)


def _mla_attention(q, k, v, w_gu, w_dn):
    nseq, nh, seq, dqk = q.shape
    dv = v.shape[-1]
    nq = seq // TQ_MLA
    n_steps = nseq * nh * nq
    gu2d = w_gu.reshape(-1, w_gu.shape[-1])
    dn2d = w_dn.reshape(-1, w_dn.shape[-1])
    n_gu, n_dn = 2 * n_steps // 3, n_steps // 3
    assert n_gu + n_dn == n_steps and gu2d.shape[0] % n_gu == 0 and dn2d.shape[0] % n_dn == 0
    gu_rows, dn_rows = gu2d.shape[0] // n_gu, dn2d.shape[0] // n_dn
    assert gu_rows % 16 == 0 and dn_rows % 16 == 0

    def step(b, h, i):
        return (b * nh + h) * nq + i

    gu_map = lambda b, h, i: (jnp.minimum(step(b, h, i), n_gu - 1), 0)
    dn_map = lambda b, h, i: (jnp.maximum(step(b, h, i) - n_gu, 0), 0)
    o, gu_b, dn_b = pl.pallas_call(
        _mla_kernel,
        grid=(nseq, nh, nq),
        in_specs=[
            pl.BlockSpec((1, 1, TQ_MLA, dqk), lambda b, h, i: (b, h, i, 0)),
            pl.BlockSpec((1, 1, seq, dqk), lambda b, h, i: (b, h, 0, 0)),
            pl.BlockSpec((1, 1, seq, dv), lambda b, h, i: (b, h, 0, 0)),
            pl.BlockSpec((gu_rows, gu2d.shape[1]), gu_map),
            pl.BlockSpec((dn_rows, dn2d.shape[1]), dn_map),
        ],
        out_specs=(
            pl.BlockSpec((TQ_MLA, dv), lambda b, h, i: (b * nq + i, h)),
            pl.BlockSpec((gu_rows, gu2d.shape[1]), gu_map),
            pl.BlockSpec((dn_rows, dn2d.shape[1]), dn_map),
        ),
        out_shape=(
            jax.ShapeDtypeStruct((nseq * seq, nh * dv), BF16),
            jax.ShapeDtypeStruct(gu2d.shape, BF16),
            jax.ShapeDtypeStruct(dn2d.shape, BF16),
        ),
        compiler_params=_params(("arbitrary", "arbitrary", "arbitrary")),
        name="mla_attention",
    )(q, k, v, gu2d, dn2d)
    return o, gu_b.reshape(w_gu.shape), dn_b.reshape(w_dn.shape)


def _swa_kernel(sink_ref, q_ref,
                klo_p, klo_c, klo_n, khi_p, khi_c, khi_n,
                vlo_p, vlo_c, vlo_n, vhi_p, vhi_c, vhi_n,
                o_ref, klo_w, khi_w, vlo_w, vhi_w, *, seq):
    tq = q_ref.shape[0]
    w = WINDOW
    nk = tq + 2 * w
    for win, (p, c, n) in ((klo_w, (klo_p, klo_c, klo_n)), (khi_w, (khi_p, khi_c, khi_n)),
                           (vlo_w, (vlo_p, vlo_c, vlo_n)), (vhi_w, (vhi_p, vhi_c, vhi_n))):
        win[0:w, :] = p[...]
        win[w:w + tq, :] = c[...]
        win[w + tq:nk, :] = n[...]

    assert SWA_Q_HEADS // SWA_KV_HEADS == 4 and tq & (tq - 1) == 0
    q0 = pl.program_id(1) * tq
    row2 = lax.broadcasted_iota(jnp.int32, (2 * tq, nk), 0)
    col = lax.broadcasted_iota(jnp.int32, (2 * tq, nk), 1)
    rel = col - w - (row2 & (tq - 1))
    kpos = q0 - w + col
    valid = (rel <= w) & (rel >= -w) & (kpos >= 0) & (kpos < seq)
    top = lax.broadcasted_iota(jnp.int32, (2 * tq, 1), 0) < tq

    for g in range(SWA_KV_HEADS):
        gs = slice(g * LANE, (g + 1) * LANE)
        pa = slice((2 * g) * LANE, (2 * g + 1) * LANE)
        pb = slice((2 * g + 1) * LANE, (2 * g + 2) * LANE)
        qq = jnp.concatenate([q_ref[:, pa], q_ref[:, pb]], axis=0)
        acc = None
        for half, (kw, vw) in enumerate(((klo_w, vlo_w), (khi_w, vhi_w))):
            sink = jnp.where(top, sink_ref[4 * g + half], sink_ref[4 * g + 2 + half]) * LOG2E
            s = lax.dot_general(qq, kw[:, gs], (((1,), (1,)), ((), ())),
                                preferred_element_type=F32)
            s = jnp.where(valid, s, NEG_INF)
            mx = jnp.maximum(jnp.max(s, axis=-1, keepdims=True), sink)
            e = jnp.exp2(s - mx)
            den = jnp.sum(e, axis=-1, keepdims=True) + jnp.exp2(sink - mx)
            pv = jnp.dot(e.astype(BF16), vw[:, gs], preferred_element_type=F32) / den
            acc = pv if acc is None else acc + pv
        o_ref[:, pa] = acc[0:tq].astype(o_ref.dtype)
        o_ref[:, pb] = acc[tq:2 * tq].astype(o_ref.dtype)


def _swa_attention(sinks, qs, klo, khi, vlo, vhi, nseq, seq):
    t, qw = qs.shape
    kw = klo.shape[1]
    tq = TQ_SWA
    nq = seq // tq
    r = tq // WINDOW
    nwb = seq // WINDOW

    prev = pl.BlockSpec((WINDOW, kw), lambda b, i, s: (b * nwb + jnp.maximum(i * r - 1, 0), 0))
    cur = pl.BlockSpec((tq, kw), lambda b, i, s: (b * nq + i, 0))
    nxt = pl.BlockSpec((WINDOW, kw), lambda b, i, s: (b * nwb + jnp.minimum((i + 1) * r, nwb - 1), 0))
    grid_spec = pltpu.PrefetchScalarGridSpec(
        num_scalar_prefetch=1,
        grid=(nseq, nq),
        in_specs=[pl.BlockSpec((tq, qw), lambda b, i, s: (b * nq + i, 0))] + [prev, cur, nxt] * 4,
        out_specs=pl.BlockSpec((tq, qw), lambda b, i, s: (b * nq + i, 0)),
        scratch_shapes=[pltpu.VMEM((tq + 2 * WINDOW, kw), BF16)] * 4,
    )
    return pl.pallas_call(
        functools.partial(_swa_kernel, seq=seq),
        grid_spec=grid_spec,
        out_shape=jax.ShapeDtypeStruct((t, qw), BF16),
        compiler_params=_params(("arbitrary", "arbitrary")),
        name="swa_attention",
    )(sinks, qs, klo, klo, klo, khi, khi, khi, vlo, vlo, vlo, vhi, vhi, vhi)


def _merge_kernel(oa_ref, ob_ref, ga_ref, gb_ref, wa_ref, wb_ref, y_ref):
    a = jnp.dot(oa_ref[...], wa_ref[...], preferred_element_type=F32)
    b = jnp.dot(ob_ref[...], wb_ref[...], preferred_element_type=F32)
    y = jax.nn.sigmoid(ga_ref[...].astype(F32)) * a + jax.nn.sigmoid(gb_ref[...].astype(F32)) * b
    y_ref[...] = y.astype(y_ref.dtype)


def _merge(o_a, o_b, z, cols, w_a, w_b):
    t, d = o_a.shape[0], w_a.shape[1]
    tm, tn = TM_MERGE, TN_MERGE
    ga_blk, gb_blk = cols["g_a"] // tn, cols["g_b"] // tn
    return pl.pallas_call(
        _merge_kernel,
        grid=(t // tm, d // tn),
        in_specs=[
            pl.BlockSpec((tm, o_a.shape[1]), lambda i, j: (i, 0)),
            pl.BlockSpec((tm, o_b.shape[1]), lambda i, j: (i, 0)),
            pl.BlockSpec((tm, tn), lambda i, j: (i, ga_blk + j)),
            pl.BlockSpec((tm, tn), lambda i, j: (i, gb_blk + j)),
            pl.BlockSpec((w_a.shape[0], tn), lambda i, j: (0, j)),
            pl.BlockSpec((w_b.shape[0], tn), lambda i, j: (0, j)),
        ],
        out_specs=pl.BlockSpec((tm, tn), lambda i, j: (i, j)),
        out_shape=jax.ShapeDtypeStruct((t, d), BF16),
        compiler_params=_params(("arbitrary", "arbitrary")),
        name="branch_merge",
    )(o_a, o_b, z, z, w_a, w_b)


def _outproj_kernel(y_ref, xa_ref, xb_ref, mod_ref, gpm_ref, gpf_ref, wo_ref, wr_ref, br_ref,
                    x1_ref, h2_ref, idx_ref, gate_ref, cnt_ref, *, n_experts, first_tiles):
    tm = y_ref.shape[0]
    u = jnp.dot(y_ref[...], wo_ref[...], preferred_element_type=F32)
    x = jnp.where(pl.program_id(0) < first_tiles, xa_ref[...], xb_ref[...])
    x1 = x + mod_ref[0, 2:3, :] * _rms(u, gpm_ref[...])
    x1_ref[...] = x1
    h2 = _rms(x1, gpf_ref[...]) * (1.0 + mod_ref[0, 4:5, :]) + mod_ref[0, 3:4, :]
    h2_ref[...] = h2.reshape(h2_ref.shape)

    logits = jnp.dot(h2.astype(BF16), wr_ref[...], preferred_element_type=F32) + br_ref[...]
    lane = lax.broadcasted_iota(jnp.int32, (tm, LANE), 1)
    lane_f = lane.astype(F32)
    cur = jnp.where(lane < n_experts, logits, NEG_INF)
    vals, idxs = [], []
    for _ in range(TOP_K):
        m = jnp.max(cur, axis=-1, keepdims=True)
        ix = jnp.min(jnp.where(cur == m, lane_f, float(LANE)), axis=-1, keepdims=True)
        vals.append(m)
        idxs.append(ix)
        cur = jnp.where(lane_f == ix, NEG_INF, cur)
    es = [jnp.exp(v - vals[0]) for v in vals]
    den = es[0]
    for e in es[1:]:
        den = den + e
    idx_out = jnp.zeros((tm, LANE), F32)
    gate_out = jnp.zeros((tm, LANE), F32)
    sel = jnp.zeros((tm, LANE), F32)
    for r in range(TOP_K):
        idx_out = jnp.where(lane == r, idxs[r], idx_out)
        gate_out = jnp.where(lane == r, es[r] / den, gate_out)
        sel = sel + jnp.where(lane_f == idxs[r], 1.0, 0.0)
    idx_ref[...] = idx_out.astype(jnp.int32)
    gate_ref[...] = gate_out

    @pl.when(pl.program_id(0) == 0)
    def _():
        cnt_ref[...] = jnp.zeros_like(cnt_ref)

    cnt_ref[0:1, :] += jnp.sum(sel, axis=0, keepdims=True)


def _out_projection(y, xa, xb, mod3, g_pm, g_pf, w_out, w_router_p, b_router_p, n_experts, seq):
    t, d = y.shape
    tm = TM_OUT
    tps = seq // tm
    first_tiles = xa.shape[0] // tm
    tok = lambda i: (i, 0)
    const = lambda i: (0, 0)
    return pl.pallas_call(
        functools.partial(_outproj_kernel, n_experts=n_experts, first_tiles=first_tiles),
        grid=(t // tm,),
        in_specs=[
            pl.BlockSpec((tm, d), tok),
            *_two_group_specs(tm, d, first_tiles, 1),
            pl.BlockSpec((1, N_MOD, d), lambda i: (i // tps, 0, 0)),
            pl.BlockSpec((1, d), const),
            pl.BlockSpec((1, d), const),
            pl.BlockSpec((d, d), const),
            pl.BlockSpec((d, LANE), const),
            pl.BlockSpec((1, LANE), const),
        ],
        out_specs=(
            pl.BlockSpec((tm, d), tok),
            pl.BlockSpec((tm, 1, d), lambda i: (i, 0, 0)),
            pl.BlockSpec((tm, LANE), tok),
            pl.BlockSpec((tm, LANE), tok),
            pl.BlockSpec((8, LANE), const),
        ),
        out_shape=(
            jax.ShapeDtypeStruct((t, d), F32),
            jax.ShapeDtypeStruct((t, 1, d), F32),
            jax.ShapeDtypeStruct((t, LANE), jnp.int32),
            jax.ShapeDtypeStruct((t, LANE), F32),
            jax.ShapeDtypeStruct((8, LANE), F32),
        ),
        compiler_params=_params(("arbitrary",)),
        name="out_projection_router",
    )(y, xa, xb, mod3, g_pm.reshape(1, d), g_pf.reshape(1, d), w_out, w_router_p, b_router_p)


def _positions_kernel(idx_ref, cnt_ref, dest_ref, meta_ref, zst_ref, carry_ref, ltri_ref, pst_ref,
                      *, n_experts, bm):
    tm = idx_ref.shape[0]
    nbp = meta_ref.shape[0]

    @pl.when(pl.program_id(0) == 0)
    def _():
        r = lax.broadcasted_iota(jnp.int32, (tm, tm), 0)
        c = lax.broadcasted_iota(jnp.int32, (tm, tm), 1)
        ltri_ref[...] = jnp.where(c < r, 1.0, 0.0).astype(BF16)
        carry_ref[...] = jnp.zeros_like(carry_ref)
        nblk = jnp.floor((cnt_ref[...] + float(bm - 1)) * (1.0 / bm))
        ur = lax.broadcasted_iota(jnp.int32, (LANE, LANE), 0)
        uc = lax.broadcasted_iota(jnp.int32, (LANE, LANE), 1)
        upper = jnp.where(ur <= uc, 1.0, 0.0).astype(BF16)
        pend = jnp.dot(nblk.astype(BF16), upper, preferred_element_type=F32)
        pst_ref[...] = (pend - nblk) * float(bm)
        lane8 = lax.broadcasted_iota(jnp.int32, (8, LANE), 1)
        zst_ref[...] = jnp.maximum(pend * float(bm) - float(bm), 0.0).astype(jnp.int32)
        pend0 = pend[0:1, :]
        n_used = jnp.sum(jnp.where(lane8[0:1, :] == n_experts - 1, pend0, 0.0), axis=-1, keepdims=True)
        blk = lax.broadcasted_iota(jnp.int32, (nbp, LANE), 0).astype(F32)
        lane = lax.broadcasted_iota(jnp.int32, (nbp, LANE), 1)
        passed = jnp.where((lane < n_experts) & (pend0 <= blk), 1.0, 0.0)
        blk_e = jnp.minimum(jnp.sum(passed, axis=-1, keepdims=True), float(n_experts - 1))
        meta = jnp.where(lane == 0, blk_e, jnp.where(lane == 1, n_used, 0.0))
        meta_ref[...] = meta.astype(jnp.int32)

    lane = lax.broadcasted_iota(jnp.int32, (tm, LANE), 1)
    idx = idx_ref[...]
    hots = []
    sel = jnp.zeros((tm, LANE), F32)
    for r in range(TOP_K):
        col = jnp.sum(jnp.where(lane == r, idx, 0).astype(F32), axis=-1, keepdims=True)
        hot = jnp.where(lane.astype(F32) == col, 1.0, 0.0)
        hots.append(hot)
        sel = sel + hot
    rank = jnp.dot(ltri_ref[...], sel.astype(BF16), preferred_element_type=F32) + carry_ref[0:1, :]
    pos = pst_ref[0:1, :] + rank
    dest = jnp.zeros((tm, LANE), F32)
    for r in range(TOP_K):
        d = jnp.sum(hots[r] * pos, axis=-1, keepdims=True)
        dest = jnp.where(lane == r, d, dest)
    dest_ref[...] = dest.T[0:8, :].astype(jnp.int32)
    carry_ref[0:1, :] += jnp.sum(sel, axis=0, keepdims=True)


def _positions(idx, cnt, n_experts, n_blocks):
    t = idx.shape[0]
    tm = TM_POS
    nbp = -(-n_blocks // 8) * 8
    return pl.pallas_call(
        functools.partial(_positions_kernel, n_experts=n_experts, bm=BM_EXP),
        grid=(t // tm,),
        in_specs=[pl.BlockSpec((tm, LANE), lambda i: (i, 0)),
                  pl.BlockSpec((8, LANE), lambda i: (0, 0))],
        out_specs=(pl.BlockSpec((8, tm), lambda i: (0, i)),
                   pl.BlockSpec((nbp, LANE), lambda i: (0, 0)),
                   pl.BlockSpec((8, LANE), lambda i: (0, 0))),
        out_shape=(jax.ShapeDtypeStruct((8, t), jnp.int32),
                   jax.ShapeDtypeStruct((nbp, LANE), jnp.int32),
                   jax.ShapeDtypeStruct((8, LANE), jnp.int32)),
        scratch_shapes=[pltpu.VMEM((8, LANE), F32), pltpu.VMEM((tm, tm), BF16), pltpu.VMEM((8, LANE), F32)],
        compiler_params=_params(("arbitrary",)),
        name="dispatch_positions",
    )(idx, cnt)


def _dispatch_kernel(zst_ref, dest_ref, h2_ref, xs_ref, zero_ref, zsem, sem, *, n_experts, bm):
    i = pl.program_id(0)
    tm = dest_ref.shape[1]

    def zero_copy(e):
        return pltpu.make_async_copy(zero_ref, xs_ref.at[pl.ds(zst_ref[e], bm)], zsem)

    def tail_copy(j):
        return pltpu.make_async_copy(zero_ref, xs_ref.at[pl.ds(j * bm, bm)], zsem)

    @pl.when(i == 0)
    def _():
        zero_ref[...] = jnp.zeros_like(zero_ref)
        n_used = zst_ref[n_experts]
        n_blocks = xs_ref.shape[0] // bm
        for e in range(n_experts):
            zero_copy(e).start()
        lax.fori_loop(n_used, n_blocks, lambda j, c: (tail_copy(j).start(), c)[1], 0)
        for e in range(n_experts):
            zero_copy(e).wait()
        lax.fori_loop(n_used, n_blocks, lambda j, c: (tail_copy(j).wait(), c)[1], 0)

    def body(j, carry):
        for r in range(TOP_K):
            d = dest_ref[r, j]
            pltpu.make_async_copy(h2_ref.at[j], xs_ref.at[d], sem).start(priority=r % 2)
        return carry

    lax.fori_loop(0, tm, body, 0, unroll=8)
    for _ in range(TOP_K):
        pltpu.make_async_copy(h2_ref, xs_ref.at[pl.ds(0, tm)], sem).wait()


def _dispatch(zst, dest, h2, n_experts, n_blocks):
    t, _, d = h2.shape
    grid_spec = pltpu.PrefetchScalarGridSpec(
        num_scalar_prefetch=1,
        grid=(t // TM_DMA,),
        in_specs=[pl.BlockSpec((8, TM_DMA), lambda i, z: (0, i), memory_space=pltpu.SMEM),
                  pl.BlockSpec((TM_DMA, 1, d), lambda i, z: (i, 0, 0))],
        out_specs=pl.BlockSpec(memory_space=pl.ANY),
        scratch_shapes=[pltpu.VMEM((BM_EXP, 1, d), F32),
                        pltpu.SemaphoreType.DMA(()), pltpu.SemaphoreType.DMA(())],
    )
    return pl.pallas_call(
        functools.partial(_dispatch_kernel, n_experts=n_experts, bm=BM_EXP),
        grid_spec=grid_spec,
        out_shape=jax.ShapeDtypeStruct((n_blocks * BM_EXP, 1, d), F32),
        compiler_params=_params(("arbitrary",)),
        name="row_dispatch",
    )(zst, dest, h2)


def _expert_kernel(sp_ref, xs_ref, wg_ref, wl_ref, bg_ref, bl_ref, wd_ref, bd_ref, ys_ref,
                   x2d_ref, xb_ref, acc_ref, *, n_blocks):
    j = pl.program_id(0)
    f = pl.program_id(1)
    nf = pl.num_programs(1)
    used = j < sp_ref[n_blocks]

    @pl.when(used & (f == 0))
    def _():
        x2d_ref[...] = xs_ref[...].reshape(x2d_ref.shape)
        xb_ref[...] = x2d_ref[...].astype(BF16)
        acc_ref[...] = jnp.broadcast_to(bd_ref[0], acc_ref.shape)

    @pl.when(jnp.logical_not(used) & (f == 0))
    def _():
        ys_ref[...] = jnp.zeros_like(ys_ref)

    @pl.when(used)
    def _():
        xb = xb_ref[...]
        glu = jnp.dot(xb, wg_ref[0], preferred_element_type=F32) + bg_ref[0]
        lin = jnp.dot(xb, wl_ref[0], preferred_element_type=F32) + bl_ref[0]
        glu = jnp.minimum(glu, SWIGLU_LIMIT)
        lin = jnp.clip(lin, -SWIGLU_LIMIT, SWIGLU_LIMIT)
        act = glu * jax.nn.sigmoid(SWIGLU_ALPHA * glu) * (lin + 1.0)
        acc_ref[...] += jnp.dot(act.astype(BF16), wd_ref[0], preferred_element_type=F32)

    @pl.when(used & (f == nf - 1))
    def _():
        ys_ref[...] = acc_ref[...].reshape(ys_ref.shape)


def _experts(sp, xs, w_gu, b_gu, w_dn, b_dn, n_blocks):
    p, _, d = xs.shape
    n_experts, _, ff2 = w_gu.shape
    ff = ff2 // 2
    nf = ff // TF_EXP
    bm = BM_EXP

    def blk(j, s):
        return jnp.minimum(j, s[n_blocks] - 1)

    def fch(j, f, s):
        return jnp.where(j < s[n_blocks], f, nf - 1)

    grid_spec = pltpu.PrefetchScalarGridSpec(
        num_scalar_prefetch=1,
        grid=(n_blocks, nf),
        in_specs=[
            pl.BlockSpec((bm, 1, d), lambda j, f, s: (blk(j, s), 0, 0)),
            pl.BlockSpec((1, d, TF_EXP), lambda j, f, s: (s[blk(j, s)], 0, fch(j, f, s))),
            pl.BlockSpec((1, d, TF_EXP), lambda j, f, s: (s[blk(j, s)], 0, nf + fch(j, f, s))),
            pl.BlockSpec((1, 1, TF_EXP), lambda j, f, s: (s[blk(j, s)], 0, fch(j, f, s))),
            pl.BlockSpec((1, 1, TF_EXP), lambda j, f, s: (s[blk(j, s)], 0, nf + fch(j, f, s))),
            pl.BlockSpec((1, TF_EXP, d), lambda j, f, s: (s[blk(j, s)], fch(j, f, s), 0)),
            pl.BlockSpec((1, 1, d), lambda j, f, s: (s[blk(j, s)], 0, 0)),
        ],
        out_specs=pl.BlockSpec((bm, 1, d), lambda j, f, s: (j, 0, 0)),
        scratch_shapes=[pltpu.VMEM((bm, d), F32), pltpu.VMEM((bm, d), BF16), pltpu.VMEM((bm, d), F32)],
    )
    return pl.pallas_call(
        functools.partial(_expert_kernel, n_blocks=n_blocks),
        grid_spec=grid_spec,
        out_shape=jax.ShapeDtypeStruct((p, 1, d), F32),
        compiler_params=_params(("arbitrary", "arbitrary")),
        name="expert_mlp",
    )(sp, xs, w_gu, w_gu, b_gu.reshape(n_experts, 1, ff2), b_gu.reshape(n_experts, 1, ff2),
      w_dn, b_dn.reshape(n_experts, 1, d))


def _combine_kernel(dest_ref, gate_ref, x1_ref, mod_ref, g_ref, ys_ref, oa_ref, ob_ref,
                    buf_ref, row_ref, sem, *, first_tiles):
    tm = x1_ref.shape[0]

    def body(j, carry):
        for r in range(TOP_K):
            d = dest_ref[r, j]
            pltpu.make_async_copy(ys_ref.at[d], buf_ref.at[r, j], sem).start(priority=r % 2)
        return carry

    lax.fori_loop(0, tm, body, 0, unroll=8)
    for r in range(TOP_K):
        pltpu.make_async_copy(ys_ref.at[pl.ds(0, tm)], buf_ref.at[r], sem).wait()

    gates = gate_ref[...]
    lane = lax.broadcasted_iota(jnp.int32, gates.shape, 1)
    f = None
    for r in range(TOP_K):
        g = jnp.sum(jnp.where(lane == r, gates, 0.0), axis=-1, keepdims=True)
        row_ref[...] = buf_ref[r].reshape(row_ref.shape)
        term = row_ref[...] * g
        f = term if f is None else f + term
    out = x1_ref[...] + mod_ref[0, 5:6, :] * _rms(f, g_ref[...])

    @pl.when(pl.program_id(0) < first_tiles)
    def _():
        oa_ref[...] = out

    @pl.when(pl.program_id(0) >= first_tiles)
    def _():
        ob_ref[...] = out


def _combine(dest, gates, x1, mod3, g_post, ys, seq, n_first):
    t, d = x1.shape
    tm = TM_DMA
    tps = seq // tm
    first_tiles = n_first * tps
    return pl.pallas_call(
        functools.partial(_combine_kernel, first_tiles=first_tiles),
        grid=(t // tm,),
        in_specs=[
            pl.BlockSpec((8, tm), lambda i: (0, i), memory_space=pltpu.SMEM),
            pl.BlockSpec((tm, LANE), lambda i: (i, 0)),
            pl.BlockSpec((tm, d), lambda i: (i, 0)),
            pl.BlockSpec((1, N_MOD, d), lambda i: (i // tps, 0, 0)),
            pl.BlockSpec((1, d), lambda i: (0, 0)),
            pl.BlockSpec(memory_space=pl.ANY),
        ],
        out_specs=(pl.BlockSpec((tm, d), lambda i: (jnp.minimum(i, first_tiles - 1), 0)),
                   pl.BlockSpec((tm, d), lambda i: (jnp.maximum(i - first_tiles, 0), 0))),
        out_shape=(jax.ShapeDtypeStruct((first_tiles * tm, d), F32),
                   jax.ShapeDtypeStruct((t - first_tiles * tm, d), F32)),
        scratch_shapes=[pltpu.VMEM((TOP_K, tm, 1, d), F32), pltpu.VMEM((tm, d), F32),
                        pltpu.SemaphoreType.DMA(())],
        compiler_params=_params(("arbitrary",)),
        name="expert_combine",
    )(dest, gates, x1, mod3, g_post.reshape(1, d), ys)


def _layout_w_in(w_in, qr, kvr):
    qsw = SWA_Q_HEADS * SWA_HEAD_DIM
    ksw = SWA_KV_HEADS * SWA_HEAD_DIM
    d = w_in.shape[0]
    sizes = (qr, kvr, MLA_ROPE, qsw, ksw, ksw, d, d)
    offs = [0]
    for s in sizes:
        offs.append(offs[-1] + s)
    c_q, c_kv, k_rope, q_s, k_s, v_s, g_a, g_b = [w_in[:, offs[i]:offs[i + 1]] for i in range(8)]
    parts = (("q_s", q_s), ("g_a", g_a), ("g_b", g_b), ("c_q", c_q), ("c_kv", c_kv),
             ("k_s", k_s), ("v_s", v_s), ("k_rope", jnp.concatenate([k_rope, k_rope], axis=1)))
    cols, off = {}, 0
    for name, p in parts:
        cols[name] = off
        off += p.shape[1]
    return jnp.concatenate([p for _, p in parts], axis=1).astype(BF16), cols


def _layout_w_uq(w_uq):
    r = w_uq.shape[0]
    w = w_uq.reshape(r, MLA_HEADS, MLA_NOPE + MLA_ROPE)
    nope = w[:, :, :MLA_NOPE].reshape(r, MLA_HEADS * MLA_NOPE)
    pe = w[:, :, MLA_NOPE:].reshape(r, MLA_HEADS * MLA_ROPE)
    return jnp.concatenate([nope, pe], axis=1).astype(BF16)


def _layout_w_ukv(w_ukv):
    r = w_ukv.shape[0]
    w = w_ukv.reshape(r, MLA_HEADS, MLA_NOPE + MLA_V)
    kn = w[:, :, :MLA_NOPE].reshape(r, MLA_HEADS * MLA_NOPE)
    v = w[:, :, MLA_NOPE:].reshape(r, MLA_HEADS * MLA_V)
    return jnp.concatenate([kn, v], axis=1).astype(BF16)


def _rope_tables(seq):
    half = MLA_ROPE // 2
    freqs = jnp.power(ROPE_THETA, -2.0 * jnp.arange(half, dtype=F32) / MLA_ROPE)
    ang = jnp.arange(seq, dtype=F32)[:, None] * freqs[None, :]
    cos, sin = jnp.cos(ang), jnp.sin(ang)
    return (jnp.concatenate([cos, cos, cos, cos], axis=1),
            jnp.concatenate([-sin, sin, -sin, sin], axis=1))


def kernel(x_prompt, x_sample, c_prompt, c_sample, w_ada, b_ada, g_pre_mix, w_in, g_q_lat, w_uq,
           g_kv_lat, w_ukv, attn_sinks, w_br_mla, w_br_swa, w_out, g_post_mix, g_pre_ffn,
           w_router, b_router, w_gu, b_gu, w_dn, b_dn, g_post_ffn):
    assert MLA_ROPE == SWA_HEAD_DIM == 64 and MLA_NOPE == MLA_V == LANE
    nb_p, seq, d = x_prompt.shape
    nb_s = x_sample.shape[0]
    assert x_sample.shape[1] == seq
    nseq = nb_p + nb_s
    t = nseq * seq
    n_experts = w_router.shape[-1]
    n_blocks = t * TOP_K // BM_EXP + n_experts

    xa, xb = x_prompt.reshape(nb_p * seq, d), x_sample.reshape(nb_s * seq, d)
    c8 = jnp.concatenate([c_prompt, c_sample, jnp.zeros((8 - nseq, d), F32)], axis=0)
    cos, sin = _rope_tables(seq)

    for l in range(w_ada.shape[0]):
        w_in_p, cols = _layout_w_in(w_in[l], g_q_lat.shape[-1], g_kv_lat.shape[-1])
        w_uq_p = _layout_w_uq(w_uq[l])
        w_ukv_p = _layout_w_ukv(w_ukv[l])
        w_router_p = jnp.pad(w_router[l], ((0, 0), (0, LANE - n_experts))).astype(BF16)
        b_router_p = jnp.pad(b_router[l], (0, LANE - n_experts)).reshape(1, LANE)

        mod = _modulation(c8, w_ada[l], b_ada[l])
        mod3 = mod[:nseq].reshape(nseq, N_MOD, d)
        z = _in_projection(xa, xb, mod3, g_pre_mix[l], w_in_p, seq)
        q, k, v, qs, klo, khi, vlo, vhi = _post_projection(
            z, cols, cos, sin, g_q_lat[l], g_kv_lat[l], w_uq_p, w_ukv_p, nseq, seq)
        o_a, w_gu_b, w_dn_b = _mla_attention(q, k, v, w_gu[l], w_dn[l])
        o_b = _swa_attention(attn_sinks[l], qs, klo, khi, vlo, vhi, nseq, seq)
        y = _merge(o_a, o_b, z, cols, w_br_mla[l].astype(BF16), w_br_swa[l].astype(BF16))
        x1, h2, idx, gates, cnt = _out_projection(
            y, xa, xb, mod3, g_post_mix[l], g_pre_ffn[l], w_out[l].astype(BF16), w_router_p, b_router_p,
            n_experts, seq)
        dest, meta, zst = _positions(idx, cnt, n_experts, n_blocks)
        sp = jnp.concatenate([meta[:n_blocks, 0], meta[0:1, 1]])
        zmeta = jnp.concatenate([zst[0, :n_experts], meta[0:1, 1]])
        xs = _dispatch(zmeta, dest, h2, n_experts, n_blocks)
        ys = _experts(sp, xs, w_gu_b, b_gu[l], w_dn_b, b_dn[l], n_blocks)
        xa, xb = _combine(dest, gates, x1, mod3, g_post_ffn[l], ys, seq, nb_p)

    return (xa.reshape(nb_p, seq, d), xb.reshape(nb_s, seq, d))
```

```python
import functools

import jax
import jax.numpy as jnp
from jax import lax
from jax.experimental import pallas as pl
from jax.experimental.pallas import tpu as pltpu

MLA_HEADS = 16
MLA_NOPE = 128
MLA_ROPE = 64
MLA_V = 128
SWA_Q_HEADS = 32
SWA_KV_HEADS = 8
SWA_HEAD_DIM = 64
WINDOW = 128
TOP_K = 4
SWIGLU_LIMIT = 7.0
SWIGLU_ALPHA = 1.702
ROPE_THETA = 10000.0
RMS_EPS = 1e-6
N_MOD = 6

LANE = 128
VMEM_LIMIT = 56 << 20

TM_IN = 512
TN_IN = 1664
TM_POST = 256
TQ_MLA = 1024
KV_CHUNK_MLA = 1024
TQ_SWA = 256
TM_MERGE = 512
TN_MERGE = 1024
TM_OUT = 512
TM_POS = 256
TM_DMA = 256
BM_EXP = 512
TF_EXP = 1024

F32 = jnp.float32
BF16 = jnp.bfloat16
NEG_INF = float("-inf")
LOG2E = 1.4426950408889634


def _params(sem):
    return pltpu.CompilerParams(dimension_semantics=sem, vmem_limit_bytes=VMEM_LIMIT)


def _rms(x, g):
    return x * lax.rsqrt(jnp.mean(x * x, axis=-1, keepdims=True) + RMS_EPS) * g


def _mod_kernel(c_ref, w_ref, b_ref, o_ref):
    c = c_ref[...]
    a = (c * jax.nn.sigmoid(c)).astype(BF16)
    o_ref[...] = jnp.dot(a, w_ref[...].astype(BF16), preferred_element_type=F32) + b_ref[...]


def _modulation(c8, w_ada, b_ada):
    d, n = w_ada.shape
    tn = 1024
    return pl.pallas_call(
        _mod_kernel,
        grid=(n // tn,),
        in_specs=[
            pl.BlockSpec((8, d), lambda j: (0, 0)),
            pl.BlockSpec((d, tn), lambda j: (0, j)),
            pl.BlockSpec((1, tn), lambda j: (0, j)),
        ],
        out_specs=pl.BlockSpec((8, tn), lambda j: (0, j)),
        out_shape=jax.ShapeDtypeStruct((8, n), F32),
        compiler_params=_params(("arbitrary",)),
        name="adaln_mod",
    )(c8, w_ada, b_ada.reshape(1, n))


def _two_group_specs(tm, d, first_tiles, n_grid_axes):
    if n_grid_axes == 1:
        return (pl.BlockSpec((tm, d), lambda i: (jnp.minimum(i, first_tiles - 1), 0)),
                pl.BlockSpec((tm, d), lambda i: (jnp.maximum(i - first_tiles, 0), 0)))
    return (pl.BlockSpec((tm, d), lambda i, j: (jnp.minimum(i, first_tiles - 1), 0)),
            pl.BlockSpec((tm, d), lambda i, j: (jnp.maximum(i - first_tiles, 0), 0)))


def _inproj_kernel(xa_ref, xb_ref, mod_ref, g_ref, w_ref, z_ref, h_ref, *, first_tiles):
    @pl.when(pl.program_id(1) == 0)
    def _():
        x = jnp.where(pl.program_id(0) < first_tiles, xa_ref[...], xb_ref[...])
        h = _rms(x, g_ref[...]) * (1.0 + mod_ref[0, 1:2, :]) + mod_ref[0, 0:1, :]
        h_ref[...] = h.astype(BF16)

    z_ref[...] = jnp.dot(h_ref[...], w_ref[...], preferred_element_type=F32).astype(z_ref.dtype)


def _in_projection(xa, xb, mod3, g, w_in_p, seq):
    d = xa.shape[1]
    t = xa.shape[0] + xb.shape[0]
    n = w_in_p.shape[1]
    tiles_per_seq = seq // TM_IN
    first_tiles = xa.shape[0] // TM_IN
    return pl.pallas_call(
        functools.partial(_inproj_kernel, first_tiles=first_tiles),
        grid=(t // TM_IN, n // TN_IN),
        in_specs=[
            *_two_group_specs(TM_IN, d, first_tiles, 2),
            pl.BlockSpec((1, N_MOD, d), lambda i, j: (i // tiles_per_seq, 0, 0)),
            pl.BlockSpec((1, d), lambda i, j: (0, 0)),
            pl.BlockSpec((d, TN_IN), lambda i, j: (0, j)),
        ],
        out_specs=pl.BlockSpec((TM_IN, TN_IN), lambda i, j: (i, j)),
        out_shape=jax.ShapeDtypeStruct((t, n), BF16),
        scratch_shapes=[pltpu.VMEM((TM_IN, d), BF16)],
        compiler_params=_params(("arbitrary", "arbitrary")),
        name="in_projection",
    )(xa, xb, mod3, g.reshape(1, d), w_in_p)


def _postproj_kernel(cq_ref, ckv_ref, qs_ref, ks_ref, vs_ref, kr_ref, cos_ref, sin_ref,
                     gq_ref, gkv_ref, wuq_ref, wukv_ref,
                     q_ref, k_ref, v_ref, qso_ref, klo_ref, khi_ref, vlo_ref, vhi_ref):
    tm = cq_ref.shape[0]
    cos = cos_ref[...]
    sin = sin_ref[...]
    lane = lax.broadcasted_iota(jnp.int32, (tm, LANE), 1)
    first_half = (lane & 63) < 32
    low = lane < 64

    def rope(x):
        rot = jnp.where(first_half, pltpu.roll(x, LANE - 32, 1), pltpu.roll(x, 32, 1))
        return x * cos + rot * sin

    nh = MLA_HEADS
    scale = float((MLA_NOPE + MLA_ROPE) ** -0.5) * LOG2E
    cqn = _rms(cq_ref[...].astype(F32), gq_ref[...]).astype(BF16)
    q = jnp.dot(cqn, wuq_ref[...], preferred_element_type=F32) * scale
    for h in range(nh):
        q_ref[0, h, :, 0:LANE] = q[:, h * LANE:(h + 1) * LANE].astype(BF16)
    for m in range(nh // 2):
        pe = rope(q[:, (nh + m) * LANE:(nh + m + 1) * LANE]).astype(BF16)
        q_ref[0, 2 * m, :, LANE:2 * LANE] = pe
        q_ref[0, 2 * m + 1, :, LANE:2 * LANE] = pe

    ckvn = _rms(ckv_ref[...].astype(F32), gkv_ref[...]).astype(BF16)
    kv = jnp.dot(ckvn, wukv_ref[...], preferred_element_type=F32)
    kr = rope(kr_ref[...].astype(F32))
    kpe_lo = jnp.where(low, kr, 0.0).astype(BF16)
    kpe_hi = jnp.where(low, 0.0, kr).astype(BF16)
    for h in range(nh):
        k_ref[0, h, :, 0:LANE] = kv[:, h * LANE:(h + 1) * LANE].astype(BF16)
        k_ref[0, h, :, LANE:2 * LANE] = kpe_lo if h % 2 == 0 else kpe_hi
        v_ref[0, h, :, :] = kv[:, (nh + h) * LANE:(nh + h + 1) * LANE].astype(BF16)

    swa_scale = float(SWA_HEAD_DIM ** -0.5) * LOG2E
    for m in range(SWA_Q_HEADS // 2):
        x = qs_ref[:, m * LANE:(m + 1) * LANE].astype(F32)
        qso_ref[:, m * LANE:(m + 1) * LANE] = (rope(x) * swa_scale).astype(BF16)

    for m in range(SWA_KV_HEADS // 2):
        sl = slice(m * LANE, (m + 1) * LANE)
        for src_ref, lo_ref, hi_ref, roped in ((ks_ref, klo_ref, khi_ref, True),
                                               (vs_ref, vlo_ref, vhi_ref, False)):
            a = src_ref[:, sl].astype(F32)
            if roped:
                a = rope(a)
            b = pltpu.roll(a, 64, 1)
            e0 = slice((2 * m) * LANE, (2 * m + 1) * LANE)
            e1 = slice((2 * m + 1) * LANE, (2 * m + 2) * LANE)
            lo_ref[:, e0] = jnp.where(low, a, 0.0).astype(BF16)
            hi_ref[:, e0] = jnp.where(low, 0.0, b).astype(BF16)
            lo_ref[:, e1] = jnp.where(low, b, 0.0).astype(BF16)
            hi_ref[:, e1] = jnp.where(low, 0.0, a).astype(BF16)


def _post_projection(z, cols, cos, sin, g_q, g_kv, w_uq_p, w_ukv_p, nseq, seq):
    t = z.shape[0]
    tm = TM_POST
    tps = seq // tm
    nh = MLA_HEADS
    qr, kvr = g_q.shape[0], g_kv.shape[0]
    qsw = SWA_Q_HEADS * SWA_HEAD_DIM
    ksw = SWA_KV_HEADS * SWA_HEAD_DIM

    def zspec(width, off):
        blk = off // width
        return pl.BlockSpec((tm, width), lambda i: (i, blk))

    head_map = lambda i: (i // tps, 0, i % tps, 0)
    tok_map = lambda i: (i, 0)
    out_shapes = (
        jax.ShapeDtypeStruct((nseq, nh, seq, 2 * LANE), BF16),
        jax.ShapeDtypeStruct((nseq, nh, seq, 2 * LANE), BF16),
        jax.ShapeDtypeStruct((nseq, nh, seq, LANE), BF16),
        jax.ShapeDtypeStruct((t, qsw), BF16),
        jax.ShapeDtypeStruct((t, SWA_KV_HEADS * LANE), BF16),
        jax.ShapeDtypeStruct((t, SWA_KV_HEADS * LANE), BF16),
        jax.ShapeDtypeStruct((t, SWA_KV_HEADS * LANE), BF16),
        jax.ShapeDtypeStruct((t, SWA_KV_HEADS * LANE), BF16),
    )
    out_specs = (
        pl.BlockSpec((1, nh, tm, 2 * LANE), head_map),
        pl.BlockSpec((1, nh, tm, 2 * LANE), head_map),
        pl.BlockSpec((1, nh, tm, LANE), head_map),
        pl.BlockSpec((tm, qsw), tok_map),
        pl.BlockSpec((tm, SWA_KV_HEADS * LANE), tok_map),
        pl.BlockSpec((tm, SWA_KV_HEADS * LANE), tok_map),
        pl.BlockSpec((tm, SWA_KV_HEADS * LANE), tok_map),
        pl.BlockSpec((tm, SWA_KV_HEADS * LANE), tok_map),
    )
    return pl.pallas_call(
        _postproj_kernel,
        grid=(t // tm,),
        in_specs=[
            zspec(qr, cols["c_q"]), zspec(kvr, cols["c_kv"]), zspec(qsw, cols["q_s"]),
            zspec(ksw, cols["k_s"]), zspec(ksw, cols["v_s"]), zspec(LANE, cols["k_rope"]),
            pl.BlockSpec((tm, LANE), lambda i: (i % tps, 0)),
            pl.BlockSpec((tm, LANE), lambda i: (i % tps, 0)),
            pl.BlockSpec((1, qr), lambda i: (0, 0)),
            pl.BlockSpec((1, kvr), lambda i: (0, 0)),
            pl.BlockSpec(w_uq_p.shape, lambda i: (0, 0)),
            pl.BlockSpec(w_ukv_p.shape, lambda i: (0, 0)),
        ],
        out_specs=out_specs,
        out_shape=out_shapes,
        compiler_params=_params(("arbitrary",)),
        name="post_projection",
    )(z, z, z, z, z, z, cos, sin, g_q.reshape(1, qr), g_kv.reshape(1, kvr), w_uq_p, w_ukv_p)


def _mla_kernel(q_ref, k_ref, v_ref, wgu_ref, wdn_ref, o_ref, wgu_o_ref, wdn_o_ref):
    wgu_o_ref[...] = wgu_ref[...].astype(BF16)
    wdn_o_ref[...] = wdn_ref[...].astype(BF16)

    q = q_ref[0, 0]
    seq = k_ref.shape[2]
    m = l = acc = None
    for c in range(seq // KV_CHUNK_MLA):
        rows = slice(c * KV_CHUNK_MLA, (c + 1) * KV_CHUNK_MLA)
        s = lax.dot_general(q, k_ref[0, 0, rows, :], (((1,), (1,)), ((), ())),
                            preferred_element_type=F32)
        mc = jnp.max(s, axis=-1, keepdims=True)
        m_new = mc if m is None else jnp.maximum(m, mc)
        p = jnp.exp2(s - m_new)
        ps = jnp.sum(p, axis=-1, keepdims=True)
        pv = jnp.dot(p.astype(BF16), v_ref[0, 0, rows, :], preferred_element_type=F32)
        if m is None:
            l, acc = ps, pv
        else:
            alpha = jnp.exp2(m - m_new)
            l = alpha * l + ps
            acc = alpha * acc + pv
        m = m_new
    o_ref[...] = (acc / l).astype(o_ref.dtype)


def _mla_attention(q, k, v, w_gu, w_dn):
    nseq, nh, seq, dqk = q.shape
    dv = v.shape[-1]
    nq = seq // TQ_MLA
    n_steps = nseq * nh * nq
    gu2d = w_gu.reshape(-1, w_gu.shape[-1])
    dn2d = w_dn.reshape(-1, w_dn.shape[-1])
    n_gu, n_dn = 2 * n_steps // 3, n_steps // 3
    assert n_gu + n_dn == n_steps and gu2d.shape[0] % n_gu == 0 and dn2d.shape[0] % n_dn == 0
    gu_rows, dn_rows = gu2d.shape[0] // n_gu, dn2d.shape[0] // n_dn
    assert gu_rows % 16 == 0 and dn_rows % 16 == 0

    def step(b, h, i):
        return (b * nh + h) * nq + i

    gu_map = lambda b, h, i: (jnp.minimum(step(b, h, i), n_gu - 1), 0)
    dn_map = lambda b, h, i: (jnp.maximum(step(b, h, i) - n_gu, 0), 0)
    o, gu_b, dn_b = pl.pallas_call(
        _mla_kernel,
        grid=(nseq, nh, nq),
        in_specs=[
            pl.BlockSpec((1, 1, TQ_MLA, dqk), lambda b, h, i: (b, h, i, 0)),
            pl.BlockSpec((1, 1, seq, dqk), lambda b, h, i: (b, h, 0, 0)),
            pl.BlockSpec((1, 1, seq, dv), lambda b, h, i: (b, h, 0, 0)),
            pl.BlockSpec((gu_rows, gu2d.shape[1]), gu_map),
            pl.BlockSpec((dn_rows, dn2d.shape[1]), dn_map),
        ],
        out_specs=(
            pl.BlockSpec((TQ_MLA, dv), lambda b, h, i: (b * nq + i, h)),
            pl.BlockSpec((gu_rows, gu2d.shape[1]), gu_map),
            pl.BlockSpec((dn_rows, dn2d.shape[1]), dn_map),
        ),
        out_shape=(
            jax.ShapeDtypeStruct((nseq * seq, nh * dv), BF16),
            jax.ShapeDtypeStruct(gu2d.shape, BF16),
            jax.ShapeDtypeStruct(dn2d.shape, BF16),
        ),
        compiler_params=_params(("arbitrary", "arbitrary", "arbitrary")),
        name="mla_attention",
    )(q, k, v, gu2d, dn2d)
    return o, gu_b.reshape(w_gu.shape), dn_b.reshape(w_dn.shape)


def _swa_kernel(sink_ref, q_ref,
                klo_p, klo_c, klo_n, khi_p, khi_c, khi_n,
                vlo_p, vlo_c, vlo_n, vhi_p, vhi_c, vhi_n,
                o_ref, klo_w, khi_w, vlo_w, vhi_w, *, seq):
    tq = q_ref.shape[0]
    w = WINDOW
    nk = tq + 2 * w
    for win, (p, c, n) in ((klo_w, (klo_p, klo_c, klo_n)), (khi_w, (khi_p, khi_c, khi_n)),
                           (vlo_w, (vlo_p, vlo_c, vlo_n)), (vhi_w, (vhi_p, vhi_c, vhi_n))):
        win[0:w, :] = p[...]
        win[w:w + tq, :] = c[...]
        win[w + tq:nk, :] = n[...]

    assert SWA_Q_HEADS // SWA_KV_HEADS == 4 and tq & (tq - 1) == 0
    q0 = pl.program_id(1) * tq
    row2 = lax.broadcasted_iota(jnp.int32, (2 * tq, nk), 0)
    col = lax.broadcasted_iota(jnp.int32, (2 * tq, nk), 1)
    rel = col - w - (row2 & (tq - 1))
    kpos = q0 - w + col
    valid = (rel <= w) & (rel >= -w) & (kpos >= 0) & (kpos < seq)
    top = lax.broadcasted_iota(jnp.int32, (2 * tq, 1), 0) < tq

    for g in range(SWA_KV_HEADS):
        gs = slice(g * LANE, (g + 1) * LANE)
        pa = slice((2 * g) * LANE, (2 * g + 1) * LANE)
        pb = slice((2 * g + 1) * LANE, (2 * g + 2) * LANE)
        qq = jnp.concatenate([q_ref[:, pa], q_ref[:, pb]], axis=0)
        acc = None
        for half, (kw, vw) in enumerate(((klo_w, vlo_w), (khi_w, vhi_w))):
            sink = jnp.where(top, sink_ref[4 * g + half], sink_ref[4 * g + 2 + half]) * LOG2E
            s = lax.dot_general(qq, kw[:, gs], (((1,), (1,)), ((), ())),
                                preferred_element_type=F32)
            s = jnp.where(valid, s, NEG_INF)
            mx = jnp.maximum(jnp.max(s, axis=-1, keepdims=True), sink)
            e = jnp.exp2(s - mx)
            den = jnp.sum(e, axis=-1, keepdims=True) + jnp.exp2(sink - mx)
            pv = jnp.dot(e.astype(BF16), vw[:, gs], preferred_element_type=F32) / den
            acc = pv if acc is None else acc + pv
        o_ref[:, pa] = acc[0:tq].astype(o_ref.dtype)
        o_ref[:, pb] = acc[tq:2 * tq].astype(o_ref.dtype)


def _swa_attention(sinks, qs, klo, khi, vlo, vhi, nseq, seq):
    t, qw = qs.shape
    kw = klo.shape[1]
    tq = TQ_SWA
    nq = seq // tq
    r = tq // WINDOW
    nwb = seq // WINDOW

    prev = pl.BlockSpec((WINDOW, kw), lambda b, i, s: (b * nwb + jnp.maximum(i * r - 1, 0), 0))
    cur = pl.BlockSpec((tq, kw), lambda b, i, s: (b * nq + i, 0))
    nxt = pl.BlockSpec((WINDOW, kw), lambda b, i, s: (b * nwb + jnp.minimum((i + 1) * r, nwb - 1), 0))
    grid_spec = pltpu.PrefetchScalarGridSpec(
        num_scalar_prefetch=1,
        grid=(nseq, nq),
        in_specs=[pl.BlockSpec((tq, qw), lambda b, i, s: (b * nq + i, 0))] + [prev, cur, nxt] * 4,
        out_specs=pl.BlockSpec((tq, qw), lambda b, i, s: (b * nq + i, 0)),
        scratch_shapes=[pltpu.VMEM((tq + 2 * WINDOW, kw), BF16)] * 4,
    )
    return pl.pallas_call(
        functools.partial(_swa_kernel, seq=seq),
        grid_spec=grid_spec,
        out_shape=jax.ShapeDtypeStruct((t, qw), BF16),
        compiler_params=_params(("arbitrary", "arbitrary")),
        name="swa_attention",
    )(sinks, qs, klo, klo, klo, khi, khi, khi, vlo, vlo, vlo, vhi, vhi, vhi)


def _merge_kernel(oa_ref, ob_ref, ga_ref, gb_ref, wa_ref, wb_ref, y_ref):
    a = jnp.dot(oa_ref[...], wa_ref[...], preferred_element_type=F32)
    b = jnp.dot(ob_ref[...], wb_ref[...], preferred_element_type=F32)
    y = jax.nn.sigmoid(ga_ref[...].astype(F32)) * a + jax.nn.sigmoid(gb_ref[...].astype(F32)) * b
    y_ref[...] = y.astype(y_ref.dtype)


def _merge(o_a, o_b, z, cols, w_a, w_b):
    t, d = o_a.shape[0], w_a.shape[1]
    tm, tn = TM_MERGE, TN_MERGE
    ga_blk, gb_blk = cols["g_a"] // tn, cols["g_b"] // tn
    return pl.pallas_call(
        _merge_kernel,
        grid=(t // tm, d // tn),
        in_specs=[
            pl.BlockSpec((tm, o_a.shape[1]), lambda i, j: (i, 0)),
            pl.BlockSpec((tm, o_b.shape[1]), lambda i, j: (i, 0)),
            pl.BlockSpec((tm, tn), lambda i, j: (i, ga_blk + j)),
            pl.BlockSpec((tm, tn), lambda i, j: (i, gb_blk + j)),
            pl.BlockSpec((w_a.shape[0], tn), lambda i, j: (0, j)),
            pl.BlockSpec((w_b.shape[0], tn), lambda i, j: (0, j)),
        ],
        out_specs=pl.BlockSpec((tm, tn), lambda i, j: (i, j)),
        out_shape=jax.ShapeDtypeStruct((t, d), BF16),
        compiler_params=_params(("arbitrary", "arbitrary")),
        name="branch_merge",
    )(o_a, o_b, z, z, w_a, w_b)


def _outproj_kernel(y_ref, xa_ref, xb_ref, mod_ref, gpm_ref, gpf_ref, wo_ref, wr_ref, br_ref,
                    x1_ref, h2_ref, idx_ref, gate_ref, cnt_ref, pack_ref, *, n_experts, first_tiles):
    tm = y_ref.shape[0]
    u = jnp.dot(y_ref[...], wo_ref[...], preferred_element_type=F32)
    x = jnp.where(pl.program_id(0) < first_tiles, xa_ref[...], xb_ref[...])
    x1 = x + mod_ref[0, 2:3, :] * _rms(u, gpm_ref[...])
    x1_ref[...] = x1
    h2 = _rms(x1, gpf_ref[...]) * (1.0 + mod_ref[0, 4:5, :]) + mod_ref[0, 3:4, :]
    h2b = h2.astype(BF16)
    bits = pltpu.bitcast(h2b.astype(F32), jnp.uint32)
    half = bits.shape[1] // 2
    pack_ref[...] = (bits[:, :half] >> 16) | (bits[:, half:] & jnp.uint32(0xFFFF0000))
    h2_ref[...] = pack_ref[...].reshape(h2_ref.shape)

    logits = jnp.dot(h2b, wr_ref[...], preferred_element_type=F32) + br_ref[...]
    lane = lax.broadcasted_iota(jnp.int32, (tm, LANE), 1)
    lane_f = lane.astype(F32)
    cur = jnp.where(lane < n_experts, logits, NEG_INF)
    vals, idxs = [], []
    for _ in range(TOP_K):
        m = jnp.max(cur, axis=-1, keepdims=True)
        ix = jnp.min(jnp.where(cur == m, lane_f, float(LANE)), axis=-1, keepdims=True)
        vals.append(m)
        idxs.append(ix)
        cur = jnp.where(lane_f == ix, NEG_INF, cur)
    es = [jnp.exp(v - vals[0]) for v in vals]
    den = es[0]
    for e in es[1:]:
        den = den + e
    idx_out = jnp.zeros((tm, LANE), F32)
    gate_out = jnp.zeros((tm, LANE), F32)
    sel = jnp.zeros((tm, LANE), F32)
    for r in range(TOP_K):
        idx_out = jnp.where(lane == r, idxs[r], idx_out)
        gate_out = jnp.where(lane == r, es[r] / den, gate_out)
        sel = sel + jnp.where(lane_f == idxs[r], 1.0, 0.0)
    idx_ref[...] = idx_out.astype(jnp.int32)
    gate_ref[...] = gate_out

    @pl.when(pl.program_id(0) == 0)
    def _():
        cnt_ref[...] = jnp.zeros_like(cnt_ref)

    cnt_ref[0:1, :] += jnp.sum(sel, axis=0, keepdims=True)


def _out_projection(y, xa, xb, mod3, g_pm, g_pf, w_out, w_router_p, b_router_p, n_experts, seq):
    t, d = y.shape
    tm = TM_OUT
    tps = seq // tm
    first_tiles = xa.shape[0] // tm
    tok = lambda i: (i, 0)
    const = lambda i: (0, 0)
    return pl.pallas_call(
        functools.partial(_outproj_kernel, n_experts=n_experts, first_tiles=first_tiles),
        grid=(t // tm,),
        in_specs=[
            pl.BlockSpec((tm, d), tok),
            *_two_group_specs(tm, d, first_tiles, 1),
            pl.BlockSpec((1, N_MOD, d), lambda i: (i // tps, 0, 0)),
            pl.BlockSpec((1, d), const),
            pl.BlockSpec((1, d), const),
            pl.BlockSpec((d, d), const),
            pl.BlockSpec((d, LANE), const),
            pl.BlockSpec((1, LANE), const),
        ],
        out_specs=(
            pl.BlockSpec((tm, d), tok),
            pl.BlockSpec((tm, 1, d // 2), lambda i: (i, 0, 0)),
            pl.BlockSpec((tm, LANE), tok),
            pl.BlockSpec((tm, LANE), tok),
            pl.BlockSpec((8, LANE), const),
        ),
        out_shape=(
            jax.ShapeDtypeStruct((t, d), F32),
            jax.ShapeDtypeStruct((t, 1, d // 2), jnp.uint32),
            jax.ShapeDtypeStruct((t, LANE), jnp.int32),
            jax.ShapeDtypeStruct((t, LANE), F32),
            jax.ShapeDtypeStruct((8, LANE), F32),
        ),
        scratch_shapes=[pltpu.VMEM((tm, d // 2), jnp.uint32)],
        compiler_params=_params(("arbitrary",)),
        name="out_projection_router",
    )(y, xa, xb, mod3, g_pm.reshape(1, d), g_pf.reshape(1, d), w_out, w_router_p, b_router_p)


def _positions_kernel(idx_ref, cnt_ref, dest_ref, meta_ref, zst_ref, carry_ref, ltri_ref, pst_ref,
                      *, n_experts, bm):
    tm = idx_ref.shape[0]
    nbp = meta_ref.shape[0]

    @pl.when(pl.program_id(0) == 0)
    def _():
        r = lax.broadcasted_iota(jnp.int32, (tm, tm), 0)
        c = lax.broadcasted_iota(jnp.int32, (tm, tm), 1)
        ltri_ref[...] = jnp.where(c < r, 1.0, 0.0).astype(BF16)
        carry_ref[...] = jnp.zeros_like(carry_ref)
        nblk = jnp.floor((cnt_ref[...] + float(bm - 1)) * (1.0 / bm))
        ur = lax.broadcasted_iota(jnp.int32, (LANE, LANE), 0)
        uc = lax.broadcasted_iota(jnp.int32, (LANE, LANE), 1)
        upper = jnp.where(ur <= uc, 1.0, 0.0).astype(BF16)
        pend = jnp.dot(nblk.astype(BF16), upper, preferred_element_type=F32)
        pst_ref[...] = (pend - nblk) * float(bm)
        lane8 = lax.broadcasted_iota(jnp.int32, (8, LANE), 1)
        zst_ref[...] = jnp.maximum(pend * float(bm) - float(bm), 0.0).astype(jnp.int32)
        pend0 = pend[0:1, :]
        n_used = jnp.sum(jnp.where(lane8[0:1, :] == n_experts - 1, pend0, 0.0), axis=-1, keepdims=True)
        blk = lax.broadcasted_iota(jnp.int32, (nbp, LANE), 0).astype(F32)
        lane = lax.broadcasted_iota(jnp.int32, (nbp, LANE), 1)
        passed = jnp.where((lane < n_experts) & (pend0 <= blk), 1.0, 0.0)
        blk_e = jnp.minimum(jnp.sum(passed, axis=-1, keepdims=True), float(n_experts - 1))
        meta = jnp.where(lane == 0, blk_e, jnp.where(lane == 1, n_used, 0.0))
        meta_ref[...] = meta.astype(jnp.int32)

    lane = lax.broadcasted_iota(jnp.int32, (tm, LANE), 1)
    idx = idx_ref[...]
    hots = []
    sel = jnp.zeros((tm, LANE), F32)
    for r in range(TOP_K):
        col = jnp.sum(jnp.where(lane == r, idx, 0).astype(F32), axis=-1, keepdims=True)
        hot = jnp.where(lane.astype(F32) == col, 1.0, 0.0)
        hots.append(hot)
        sel = sel + hot
    rank = jnp.dot(ltri_ref[...], sel.astype(BF16), preferred_element_type=F32) + carry_ref[0:1, :]
    pos = pst_ref[0:1, :] + rank
    dest = jnp.zeros((tm, LANE), F32)
    for r in range(TOP_K):
        d = jnp.sum(hots[r] * pos, axis=-1, keepdims=True)
        dest = jnp.where(lane == r, d, dest)
    dest_ref[...] = dest.T[0:8, :].astype(jnp.int32)
    carry_ref[0:1, :] += jnp.sum(sel, axis=0, keepdims=True)


def _positions(idx, cnt, n_experts, n_blocks):
    t = idx.shape[0]
    tm = TM_POS
    nbp = -(-n_blocks // 8) * 8
    return pl.pallas_call(
        functools.partial(_positions_kernel, n_experts=n_experts, bm=BM_EXP),
        grid=(t // tm,),
        in_specs=[pl.BlockSpec((tm, LANE), lambda i: (i, 0)),
                  pl.BlockSpec((8, LANE), lambda i: (0, 0))],
        out_specs=(pl.BlockSpec((8, tm), lambda i: (0, i)),
                   pl.BlockSpec((nbp, LANE), lambda i: (0, 0)),
                   pl.BlockSpec((8, LANE), lambda i: (0, 0))),
        out_shape=(jax.ShapeDtypeStruct((8, t), jnp.int32),
                   jax.ShapeDtypeStruct((nbp, LANE), jnp.int32),
                   jax.ShapeDtypeStruct((8, LANE), jnp.int32)),
        scratch_shapes=[pltpu.VMEM((8, LANE), F32), pltpu.VMEM((tm, tm), BF16), pltpu.VMEM((8, LANE), F32)],
        compiler_params=_params(("arbitrary",)),
        name="dispatch_positions",
    )(idx, cnt)


def _dispatch_kernel(zst_ref, dest_ref, h2_ref, xs_ref, zero_ref, zsem, sem, *, n_experts, bm):
    i = pl.program_id(0)
    tm = dest_ref.shape[1]

    def zero_copy(e):
        return pltpu.make_async_copy(zero_ref, xs_ref.at[pl.ds(zst_ref[e], bm)], zsem)

    def tail_copy(j):
        return pltpu.make_async_copy(zero_ref, xs_ref.at[pl.ds(j * bm, bm)], zsem)

    @pl.when(i == 0)
    def _():
        zero_ref[...] = jnp.zeros_like(zero_ref)
        n_used = zst_ref[n_experts]
        n_blocks = xs_ref.shape[0] // bm
        for e in range(n_experts):
            zero_copy(e).start()
        lax.fori_loop(n_used, n_blocks, lambda j, c: (tail_copy(j).start(), c)[1], 0)
        for e in range(n_experts):
            zero_copy(e).wait()
        lax.fori_loop(n_used, n_blocks, lambda j, c: (tail_copy(j).wait(), c)[1], 0)

    def body(j, carry):
        for r in range(TOP_K):
            d = dest_ref[r, j]
            pltpu.make_async_copy(h2_ref.at[j], xs_ref.at[d], sem).start(priority=r % 2)
        return carry

    lax.fori_loop(0, tm, body, 0, unroll=8)
    for _ in range(TOP_K):
        pltpu.make_async_copy(h2_ref, xs_ref.at[pl.ds(0, tm)], sem).wait()


def _dispatch(zst, dest, h2, n_experts, n_blocks):
    t, _, d = h2.shape
    grid_spec = pltpu.PrefetchScalarGridSpec(
        num_scalar_prefetch=1,
        grid=(t // TM_DMA,),
        in_specs=[pl.BlockSpec((8, TM_DMA), lambda i, z: (0, i), memory_space=pltpu.SMEM),
                  pl.BlockSpec((TM_DMA, 1, d), lambda i, z: (i, 0, 0))],
        out_specs=pl.BlockSpec(memory_space=pl.ANY),
        scratch_shapes=[pltpu.VMEM((BM_EXP, 1, d), h2.dtype),
                        pltpu.SemaphoreType.DMA(()), pltpu.SemaphoreType.DMA(())],
    )
    return pl.pallas_call(
        functools.partial(_dispatch_kernel, n_experts=n_experts, bm=BM_EXP),
        grid_spec=grid_spec,
        out_shape=jax.ShapeDtypeStruct((n_blocks * BM_EXP, 1, d), h2.dtype),
        compiler_params=_params(("arbitrary",)),
        name="row_dispatch",
    )(zst, dest, h2)


def _expert_kernel(sp_ref, xs_ref, wg_ref, wl_ref, bg_ref, bl_ref, wd_ref, bd_ref, ys_ref,
                   x2d_ref, xb_ref, acc_ref, *, n_blocks):
    j = pl.program_id(0)
    f = pl.program_id(1)
    nf = pl.num_programs(1)
    used = j < sp_ref[n_blocks]

    @pl.when(used & (f == 0))
    def _():
        x2d_ref[...] = xs_ref[...].reshape(x2d_ref.shape)
        words = x2d_ref[...]
        half = words.shape[1]
        xb_ref[:, :half] = pltpu.bitcast(words << 16, F32).astype(BF16)
        xb_ref[:, half:] = pltpu.bitcast(words & jnp.uint32(0xFFFF0000), F32).astype(BF16)

    @pl.when(jnp.logical_not(used) & (f == 0))
    def _():
        ys_ref[...] = jnp.zeros_like(ys_ref)

    @pl.when(used)
    def _():
        xb = xb_ref[...]
        glu = jnp.dot(xb, wg_ref[0], preferred_element_type=F32) + bg_ref[0]
        lin = jnp.dot(xb, wl_ref[0], preferred_element_type=F32) + bl_ref[0]
        glu = jnp.minimum(glu, SWIGLU_LIMIT)
        lin = jnp.clip(lin, -SWIGLU_LIMIT, SWIGLU_LIMIT)
        act = glu * jax.nn.sigmoid(SWIGLU_ALPHA * glu) * (lin + 1.0)
        prev = jnp.where(f == 0, jnp.broadcast_to(bd_ref[0], acc_ref.shape), acc_ref[...])
        acc_ref[...] = prev + jnp.dot(act.astype(BF16), wd_ref[0], preferred_element_type=F32)

    @pl.when(used & (f == nf - 1))
    def _():
        ys_ref[...] = acc_ref[...].reshape(ys_ref.shape)


def _experts(sp, xs, w_gu, b_gu, w_dn, b_dn, n_blocks):
    p, _, dh = xs.shape
    n_experts, d, ff2 = w_gu.shape
    assert d == 2 * dh
    ff = ff2 // 2
    nf = ff // TF_EXP
    bm = BM_EXP

    def blk(j, s):
        return jnp.minimum(j, s[n_blocks] - 1)

    def fch(j, f, s):
        return jnp.where(j < s[n_blocks], f, nf - 1)

    grid_spec = pltpu.PrefetchScalarGridSpec(
        num_scalar_prefetch=1,
        grid=(n_blocks, nf),
        in_specs=[
            pl.BlockSpec((bm, 1, dh), lambda j, f, s: (blk(j, s), 0, 0)),
            pl.BlockSpec((1, d, TF_EXP), lambda j, f, s: (s[blk(j, s)], 0, fch(j, f, s))),
            pl.BlockSpec((1, d, TF_EXP), lambda j, f, s: (s[blk(j, s)], 0, nf + fch(j, f, s))),
            pl.BlockSpec((1, 1, TF_EXP), lambda j, f, s: (s[blk(j, s)], 0, fch(j, f, s))),
            pl.BlockSpec((1, 1, TF_EXP), lambda j, f, s: (s[blk(j, s)], 0, nf + fch(j, f, s))),
            pl.BlockSpec((1, TF_EXP, d), lambda j, f, s: (s[blk(j, s)], fch(j, f, s), 0)),
            pl.BlockSpec((1, 1, d), lambda j, f, s: (s[blk(j, s)], 0, 0)),
        ],
        out_specs=pl.BlockSpec((bm, 1, d), lambda j, f, s: (j, 0, 0)),
        scratch_shapes=[pltpu.VMEM((bm, dh), jnp.uint32), pltpu.VMEM((bm, d), BF16), pltpu.VMEM((bm, d), F32)],
    )
    return pl.pallas_call(
        functools.partial(_expert_kernel, n_blocks=n_blocks),
        grid_spec=grid_spec,
        out_shape=jax.ShapeDtypeStruct((p, 1, d), F32),
        compiler_params=_params(("arbitrary", "arbitrary")),
        name="expert_mlp",
    )(sp, xs, w_gu, w_gu, b_gu.reshape(n_experts, 1, ff2), b_gu.reshape(n_experts, 1, ff2),
      w_dn, b_dn.reshape(n_experts, 1, d))


def _combine_kernel(dest_ref, dest_next_ref, gate_ref, x1_ref, mod_ref, g_ref, ys_ref, oa_ref, ob_ref,
                    buf_ref, row_ref, sems, *, first_tiles):
    tm = x1_ref.shape[0]
    i = pl.program_id(0)
    slot = i % 2

    def gather(table_ref, s):
        def body(j, carry):
            for r in range(TOP_K):
                d = table_ref[r, j]
                pltpu.make_async_copy(ys_ref.at[d], buf_ref.at[s, r, j], sems.at[s]).start(priority=r % 2)
            return carry

        lax.fori_loop(0, tm, body, 0, unroll=8)

    @pl.when(i == 0)
    def _():
        gather(dest_ref, 0)

    @pl.when(i + 1 < pl.num_programs(0))
    def _():
        gather(dest_next_ref, 1 - slot)

    for r in range(TOP_K):
        pltpu.make_async_copy(ys_ref.at[pl.ds(0, tm)], buf_ref.at[slot, r], sems.at[slot]).wait()

    gates = gate_ref[...]
    lane = lax.broadcasted_iota(jnp.int32, gates.shape, 1)
    f = None
    for r in range(TOP_K):
        g = jnp.sum(jnp.where(lane == r, gates, 0.0), axis=-1, keepdims=True)
        row_ref[...] = buf_ref[slot, r].reshape(row_ref.shape)
        term = row_ref[...] * g
        f = term if f is None else f + term
    out = x1_ref[...] + mod_ref[0, 5:6, :] * _rms(f, g_ref[...])

    @pl.when(pl.program_id(0) < first_tiles)
    def _():
        oa_ref[...] = out

    @pl.when(pl.program_id(0) >= first_tiles)
    def _():
        ob_ref[...] = out


def _combine(dest, gates, x1, mod3, g_post, ys, seq, n_first):
    t, d = x1.shape
    tm = TM_DMA
    tps = seq // tm
    first_tiles = n_first * tps
    n_tiles = t // tm
    return pl.pallas_call(
        functools.partial(_combine_kernel, first_tiles=first_tiles),
        grid=(n_tiles,),
        in_specs=[
            pl.BlockSpec((8, tm), lambda i: (0, i), memory_space=pltpu.SMEM),
            pl.BlockSpec((8, tm), lambda i: (0, jnp.minimum(i + 1, n_tiles - 1)), memory_space=pltpu.SMEM),
            pl.BlockSpec((tm, LANE), lambda i: (i, 0)),
            pl.BlockSpec((tm, d), lambda i: (i, 0)),
            pl.BlockSpec((1, N_MOD, d), lambda i: (i // tps, 0, 0)),
            pl.BlockSpec((1, d), lambda i: (0, 0)),
            pl.BlockSpec(memory_space=pl.ANY),
        ],
        out_specs=(pl.BlockSpec((tm, d), lambda i: (jnp.minimum(i, first_tiles - 1), 0)),
                   pl.BlockSpec((tm, d), lambda i: (jnp.maximum(i - first_tiles, 0), 0))),
        out_shape=(jax.ShapeDtypeStruct((first_tiles * tm, d), F32),
                   jax.ShapeDtypeStruct((t - first_tiles * tm, d), F32)),
        scratch_shapes=[pltpu.VMEM((2, TOP_K, tm, 1, d), F32), pltpu.VMEM((tm, d), F32),
                        pltpu.SemaphoreType.DMA((2,))],
        compiler_params=_params(("arbitrary",)),
        name="expert_combine",
    )(dest, dest, gates, x1, mod3, g_post.reshape(1, d), ys)


def _layout_w_in(w_in, qr, kvr):
    qsw = SWA_Q_HEADS * SWA_HEAD_DIM
    ksw = SWA_KV_HEADS * SWA_HEAD_DIM
    d = w_in.shape[0]
    sizes = (qr, kvr, MLA_ROPE, qsw, ksw, ksw, d, d)
    offs = [0]
    for s in sizes:
        offs.append(offs[-1] + s)
    c_q, c_kv, k_rope, q_s, k_s, v_s, g_a, g_b = [w_in[:, offs[i]:offs[i + 1]] for i in range(8)]
    parts = (("q_s", q_s), ("g_a", g_a), ("g_b", g_b), ("c_q", c_q), ("c_kv", c_kv),
             ("k_s", k_s), ("v_s", v_s), ("k_rope", jnp.concatenate([k_rope, k_rope], axis=1)))
    cols, off = {}, 0
    for name, p in parts:
        cols[name] = off
        off += p.shape[1]
    return jnp.concatenate([p for _, p in parts], axis=1).astype(BF16), cols


def _layout_w_uq(w_uq):
    r = w_uq.shape[0]
    w = w_uq.reshape(r, MLA_HEADS, MLA_NOPE + MLA_ROPE)
    nope = w[:, :, :MLA_NOPE].reshape(r, MLA_HEADS * MLA_NOPE)
    pe = w[:, :, MLA_NOPE:].reshape(r, MLA_HEADS * MLA_ROPE)
    return jnp.concatenate([nope, pe], axis=1).astype(BF16)


def _layout_w_ukv(w_ukv):
    r = w_ukv.shape[0]
    w = w_ukv.reshape(r, MLA_HEADS, MLA_NOPE + MLA_V)
    kn = w[:, :, :MLA_NOPE].reshape(r, MLA_HEADS * MLA_NOPE)
    v = w[:, :, MLA_NOPE:].reshape(r, MLA_HEADS * MLA_V)
    return jnp.concatenate([kn, v], axis=1).astype(BF16)


def _rope_tables(seq):
    half = MLA_ROPE // 2
    freqs = jnp.power(ROPE_THETA, -2.0 * jnp.arange(half, dtype=F32) / MLA_ROPE)
    ang = jnp.arange(seq, dtype=F32)[:, None] * freqs[None, :]
    cos, sin = jnp.cos(ang), jnp.sin(ang)
    return (jnp.concatenate([cos, cos, cos, cos], axis=1),
            jnp.concatenate([-sin, sin, -sin, sin], axis=1))


def kernel(x_prompt, x_sample, c_prompt, c_sample, w_ada, b_ada, g_pre_mix, w_in, g_q_lat, w_uq,
           g_kv_lat, w_ukv, attn_sinks, w_br_mla, w_br_swa, w_out, g_post_mix, g_pre_ffn,
           w_router, b_router, w_gu, b_gu, w_dn, b_dn, g_post_ffn):
    assert MLA_ROPE == SWA_HEAD_DIM == 64 and MLA_NOPE == MLA_V == LANE
    nb_p, seq, d = x_prompt.shape
    nb_s = x_sample.shape[0]
    assert x_sample.shape[1] == seq
    nseq = nb_p + nb_s
    t = nseq * seq
    n_experts = w_router.shape[-1]
    n_blocks = t * TOP_K // BM_EXP + n_experts

    xa, xb = x_prompt.reshape(nb_p * seq, d), x_sample.reshape(nb_s * seq, d)
    c8 = jnp.concatenate([c_prompt, c_sample, jnp.zeros((8 - nseq, d), F32)], axis=0)
    cos, sin = _rope_tables(seq)

    for l in range(w_ada.shape[0]):
        w_in_p, cols = _layout_w_in(w_in[l], g_q_lat.shape[-1], g_kv_lat.shape[-1])
        w_uq_p = _layout_w_uq(w_uq[l])
        w_ukv_p = _layout_w_ukv(w_ukv[l])
        w_router_p = jnp.pad(w_router[l], ((0, 0), (0, LANE - n_experts))).astype(BF16)
        b_router_p = jnp.pad(b_router[l], (0, LANE - n_experts)).reshape(1, LANE)

        mod = _modulation(c8, w_ada[l], b_ada[l])
        mod3 = mod[:nseq].reshape(nseq, N_MOD, d)
        z = _in_projection(xa, xb, mod3, g_pre_mix[l], w_in_p, seq)
        q, k, v, qs, klo, khi, vlo, vhi = _post_projection(
            z, cols, cos, sin, g_q_lat[l], g_kv_lat[l], w_uq_p, w_ukv_p, nseq, seq)
        o_a, w_gu_b, w_dn_b = _mla_attention(q, k, v, w_gu[l], w_dn[l])
        o_b = _swa_attention(attn_sinks[l], qs, klo, khi, vlo, vhi, nseq, seq)
        y = _merge(o_a, o_b, z, cols, w_br_mla[l].astype(BF16), w_br_swa[l].astype(BF16))
        x1, h2, idx, gates, cnt = _out_projection(
            y, xa, xb, mod3, g_post_mix[l], g_pre_ffn[l], w_out[l].astype(BF16), w_router_p, b_router_p,
            n_experts, seq)
        dest, meta, zst = _positions(idx, cnt, n_experts, n_blocks)
        sp = jnp.concatenate([meta[:n_blocks, 0], meta[0:1, 1]])
        zmeta = jnp.concatenate([zst[0, :n_experts], meta[0:1, 1]])
        xs = _dispatch(zmeta, dest, h2, n_experts, n_blocks)
        ys = _experts(sp, xs, w_gu_b, b_gu[l], w_dn_b, b_dn[l], n_blocks)
        xa, xb = _combine(dest, gates, x1, mod3, g_post_ffn[l], ys, seq, nb_p)

    return (xa.reshape(nb_p, seq, d), xb.reshape(nb_s, seq, d))
```

```python
import functools

import jax
import jax.numpy as jnp
from jax import lax
from jax.experimental import pallas as pl
from jax.experimental.pallas import tpu as pltpu

MLA_HEADS = 16
MLA_NOPE = 128
MLA_ROPE = 64
MLA_V = 128
SWA_Q_HEADS = 32
SWA_KV_HEADS = 8
SWA_HEAD_DIM = 64
WINDOW = 128
TOP_K = 4
SWIGLU_LIMIT = 7.0
SWIGLU_ALPHA = 1.702
ROPE_THETA = 10000.0
RMS_EPS = 1e-6
N_MOD = 6

LANE = 128
VMEM_LIMIT = 56 << 20

TM_IN = 512
TN_IN = 1664
TM_POST = 256
TQ_MLA = 1024
KV_CHUNKS_MLA = (1024, 1024, 1024, 1024)
TQ_SWA = 256
TM_MERGE = 1024
TN_MERGE = 1024
TM_OUT = 512
TM_POS = 512
TM_DISPATCH = 512
TM_DMA = 256
BM_EXP = 512
TF_EXP = 1024

F32 = jnp.float32
BF16 = jnp.bfloat16
NEG_INF = float("-inf")
LOG2E = 1.4426950408889634


def _params(sem):
    return pltpu.CompilerParams(dimension_semantics=sem, vmem_limit_bytes=VMEM_LIMIT)


def _rms(x, g):
    return x * lax.rsqrt(jnp.mean(x * x, axis=-1, keepdims=True) + RMS_EPS) * g


def _mod_kernel(c_ref, w_ref, b_ref, o_ref):
    c = c_ref[...]
    a = (c * jax.nn.sigmoid(c)).astype(BF16)
    o_ref[...] = jnp.dot(a, w_ref[...].astype(BF16), preferred_element_type=F32) + b_ref[...]


def _modulation(c8, w_ada, b_ada):
    d, n = w_ada.shape
    tn = 1024
    return pl.pallas_call(
        _mod_kernel,
        grid=(n // tn,),
        in_specs=[
            pl.BlockSpec((8, d), lambda j: (0, 0)),
            pl.BlockSpec((d, tn), lambda j: (0, j)),
            pl.BlockSpec((1, tn), lambda j: (0, j)),
        ],
        out_specs=pl.BlockSpec((8, tn), lambda j: (0, j)),
        out_shape=jax.ShapeDtypeStruct((8, n), F32),
        compiler_params=_params(("arbitrary",)),
        name="adaln_mod",
    )(c8, w_ada, b_ada.reshape(1, n))


def _two_group_specs(tm, d, first_tiles, n_grid_axes):
    if n_grid_axes == 1:
        return (pl.BlockSpec((tm, d), lambda i: (jnp.minimum(i, first_tiles - 1), 0)),
                pl.BlockSpec((tm, d), lambda i: (jnp.maximum(i - first_tiles, 0), 0)))
    return (pl.BlockSpec((tm, d), lambda i, j: (jnp.minimum(i, first_tiles - 1), 0)),
            pl.BlockSpec((tm, d), lambda i, j: (jnp.maximum(i - first_tiles, 0), 0)))


def _inproj_kernel(xa_ref, xb_ref, mod_ref, g_ref, w_ref, z_ref, h_ref, *, first_tiles):
    @pl.when(pl.program_id(1) == 0)
    def _():
        x = jnp.where(pl.program_id(0) < first_tiles, xa_ref[...], xb_ref[...])
        h = _rms(x, g_ref[...]) * (1.0 + mod_ref[0, 1:2, :]) + mod_ref[0, 0:1, :]
        h_ref[...] = h.astype(BF16)

    z_ref[...] = jnp.dot(h_ref[...], w_ref[...], preferred_element_type=F32).astype(z_ref.dtype)


def _in_projection(xa, xb, mod3, g, w_in_p, seq):
    d = xa.shape[1]
    t = xa.shape[0] + xb.shape[0]
    n = w_in_p.shape[1]
    tiles_per_seq = seq // TM_IN
    first_tiles = xa.shape[0] // TM_IN
    return pl.pallas_call(
        functools.partial(_inproj_kernel, first_tiles=first_tiles),
        grid=(t // TM_IN, n // TN_IN),
        in_specs=[
            *_two_group_specs(TM_IN, d, first_tiles, 2),
            pl.BlockSpec((1, N_MOD, d), lambda i, j: (i // tiles_per_seq, 0, 0)),
            pl.BlockSpec((1, d), lambda i, j: (0, 0)),
            pl.BlockSpec((d, TN_IN), lambda i, j: (0, j)),
        ],
        out_specs=pl.BlockSpec((TM_IN, TN_IN), lambda i, j: (i, j)),
        out_shape=jax.ShapeDtypeStruct((t, n), BF16),
        scratch_shapes=[pltpu.VMEM((TM_IN, d), BF16)],
        compiler_params=_params(("arbitrary", "arbitrary")),
        name="in_projection",
    )(xa, xb, mod3, g.reshape(1, d), w_in_p)


def _postproj_kernel(cq_ref, ckv_ref, qs_ref, ks_ref, vs_ref, kr_ref, cos_ref, sin_ref,
                     gq_ref, gkv_ref, wuq_ref, wukv_ref,
                     q_ref, k_ref, v_ref, qso_ref, klo_ref, khi_ref, vlo_ref, vhi_ref):
    tm = cq_ref.shape[0]
    cos = cos_ref[...]
    sin = sin_ref[...]
    lane = lax.broadcasted_iota(jnp.int32, (tm, LANE), 1)
    first_half = (lane & 63) < 32
    low = lane < 64

    def rope(x):
        rot = jnp.where(first_half, pltpu.roll(x, LANE - 32, 1), pltpu.roll(x, 32, 1))
        return x * cos + rot * sin

    nh = MLA_HEADS
    scale = float((MLA_NOPE + MLA_ROPE) ** -0.5) * LOG2E
    cqn = _rms(cq_ref[...].astype(F32), gq_ref[...]).astype(BF16)
    q = jnp.dot(cqn, wuq_ref[...], preferred_element_type=F32) * scale
    for h in range(nh):
        q_ref[0, h, :, 0:LANE] = q[:, h * LANE:(h + 1) * LANE].astype(BF16)
    for m in range(nh // 2):
        pe = rope(q[:, (nh + m) * LANE:(nh + m + 1) * LANE]).astype(BF16)
        q_ref[0, 2 * m, :, LANE:2 * LANE] = pe
        q_ref[0, 2 * m + 1, :, LANE:2 * LANE] = pe

    ckvn = _rms(ckv_ref[...].astype(F32), gkv_ref[...]).astype(BF16)
    kv = jnp.dot(ckvn, wukv_ref[...], preferred_element_type=F32)
    kr = rope(kr_ref[...].astype(F32))
    kpe_lo = jnp.where(low, kr, 0.0).astype(BF16)
    kpe_hi = jnp.where(low, 0.0, kr).astype(BF16)
    for h in range(nh):
        k_ref[0, h, :, 0:LANE] = kv[:, h * LANE:(h + 1) * LANE].astype(BF16)
        k_ref[0, h, :, LANE:2 * LANE] = kpe_lo if h % 2 == 0 else kpe_hi
        v_ref[0, h, :, :] = kv[:, (nh + h) * LANE:(nh + h + 1) * LANE].astype(BF16)

    swa_scale = float(SWA_HEAD_DIM ** -0.5) * LOG2E
    for m in range(SWA_Q_HEADS // 2):
        x = qs_ref[:, m * LANE:(m + 1) * LANE].astype(F32)
        qso_ref[:, m * LANE:(m + 1) * LANE] = (rope(x) * swa_scale).astype(BF16)

    for m in range(SWA_KV_HEADS // 2):
        sl = slice(m * LANE, (m + 1) * LANE)
        for src_ref, lo_ref, hi_ref, roped in ((ks_ref, klo_ref, khi_ref, True),
                                               (vs_ref, vlo_ref, vhi_ref, False)):
            a = src_ref[:, sl].astype(F32)
            if roped:
                a = rope(a)
            b = pltpu.roll(a, 64, 1)
            e0 = slice((2 * m) * LANE, (2 * m + 1) * LANE)
            e1 = slice((2 * m + 1) * LANE, (2 * m + 2) * LANE)
            lo_ref[:, e0] = jnp.where(low, a, 0.0).astype(BF16)
            hi_ref[:, e0] = jnp.where(low, 0.0, b).astype(BF16)
            lo_ref[:, e1] = jnp.where(low, b, 0.0).astype(BF16)
            hi_ref[:, e1] = jnp.where(low, 0.0, a).astype(BF16)


def _post_projection(z, cols, cos, sin, g_q, g_kv, w_uq_p, w_ukv_p, nseq, seq):
    t = z.shape[0]
    tm = TM_POST
    tps = seq // tm
    nh = MLA_HEADS
    qr, kvr = g_q.shape[0], g_kv.shape[0]
    qsw = SWA_Q_HEADS * SWA_HEAD_DIM
    ksw = SWA_KV_HEADS * SWA_HEAD_DIM

    def zspec(width, off):
        blk = off // width
        return pl.BlockSpec((tm, width), lambda i: (i, blk))

    head_map = lambda i: (i // tps, 0, i % tps, 0)
    tok_map = lambda i: (i, 0)
    out_shapes = (
        jax.ShapeDtypeStruct((nseq, nh, seq, 2 * LANE), BF16),
        jax.ShapeDtypeStruct((nseq, nh, seq, 2 * LANE), BF16),
        jax.ShapeDtypeStruct((nseq, nh, seq, LANE), BF16),
        jax.ShapeDtypeStruct((t, qsw), BF16),
        jax.ShapeDtypeStruct((t, SWA_KV_HEADS * LANE), BF16),
        jax.ShapeDtypeStruct((t, SWA_KV_HEADS * LANE), BF16),
        jax.ShapeDtypeStruct((t, SWA_KV_HEADS * LANE), BF16),
        jax.ShapeDtypeStruct((t, SWA_KV_HEADS * LANE), BF16),
    )
    out_specs = (
        pl.BlockSpec((1, nh, tm, 2 * LANE), head_map),
        pl.BlockSpec((1, nh, tm, 2 * LANE), head_map),
        pl.BlockSpec((1, nh, tm, LANE), head_map),
        pl.BlockSpec((tm, qsw), tok_map),
        pl.BlockSpec((tm, SWA_KV_HEADS * LANE), tok_map),
        pl.BlockSpec((tm, SWA_KV_HEADS * LANE), tok_map),
        pl.BlockSpec((tm, SWA_KV_HEADS * LANE), tok_map),
        pl.BlockSpec((tm, SWA_KV_HEADS * LANE), tok_map),
    )
    return pl.pallas_call(
        _postproj_kernel,
        grid=(t // tm,),
        in_specs=[
            zspec(qr, cols["c_q"]), zspec(kvr, cols["c_kv"]), zspec(qsw, cols["q_s"]),
            zspec(ksw, cols["k_s"]), zspec(ksw, cols["v_s"]), zspec(LANE, cols["k_rope"]),
            pl.BlockSpec((tm, LANE), lambda i: (i % tps, 0)),
            pl.BlockSpec((tm, LANE), lambda i: (i % tps, 0)),
            pl.BlockSpec((1, qr), lambda i: (0, 0)),
            pl.BlockSpec((1, kvr), lambda i: (0, 0)),
            pl.BlockSpec(w_uq_p.shape, lambda i: (0, 0)),
            pl.BlockSpec(w_ukv_p.shape, lambda i: (0, 0)),
        ],
        out_specs=out_specs,
        out_shape=out_shapes,
        compiler_params=_params(("arbitrary",)),
        name="post_projection",
    )(z, z, z, z, z, z, cos, sin, g_q.reshape(1, qr), g_kv.reshape(1, kvr), w_uq_p, w_ukv_p)


def _mla_kernel(q_ref, k_ref, v_ref, wgu_ref, wdn_ref, o_ref, wgu_o_ref, wdn_o_ref):
    wgu_o_ref[...] = wgu_ref[...].astype(BF16)
    wdn_o_ref[...] = wdn_ref[...].astype(BF16)

    q = q_ref[0, 0]
    seq = k_ref.shape[2]
    m = l = acc = None
    assert sum(KV_CHUNKS_MLA) == seq
    start = 0
    for width in KV_CHUNKS_MLA:
        rows = slice(start, start + width)
        start += width
        s = lax.dot_general(q, k_ref[0, 0, rows, :], (((1,), (1,)), ((), ())),
                            preferred_element_type=F32)
        mc = jnp.max(s, axis=-1, keepdims=True)
        m_new = mc if m is None else jnp.maximum(m, mc)
        p = jnp.exp2(s - m_new)
        ps = jnp.sum(p, axis=-1, keepdims=True)
        pv = jnp.dot(p.astype(BF16), v_ref[0, 0, rows, :], preferred_element_type=F32)
        if m is None:
            l, acc = ps, pv
        else:
            alpha = jnp.exp2(m - m_new)
            l = alpha * l + ps
            acc = alpha * acc + pv
        m = m_new
    o_ref[...] = (acc / l).astype(o_ref.dtype)


def _mla_attention(q, k, v, w_gu, w_dn):
    nseq, nh, seq, dqk = q.shape
    dv = v.shape[-1]
    nq = seq // TQ_MLA
    n_steps = nseq * nh * nq
    gu2d = w_gu.reshape(-1, w_gu.shape[-1])
    dn2d = w_dn.reshape(-1, w_dn.shape[-1])
    n_gu, n_dn = 2 * n_steps // 3, n_steps // 3
    assert n_gu + n_dn == n_steps and gu2d.shape[0] % n_gu == 0 and dn2d.shape[0] % n_dn == 0
    gu_rows, dn_rows = gu2d.shape[0] // n_gu, dn2d.shape[0] // n_dn
    assert gu_rows % 16 == 0 and dn_rows % 16 == 0

    def step(b, h, i):
        return (b * nh + h) * nq + i

    gu_map = lambda b, h, i: (jnp.minimum(step(b, h, i), n_gu - 1), 0)
    dn_map = lambda b, h, i: (jnp.maximum(step(b, h, i) - n_gu, 0), 0)
    o, gu_b, dn_b = pl.pallas_call(
        _mla_kernel,
        grid=(nseq, nh, nq),
        in_specs=[
            pl.BlockSpec((1, 1, TQ_MLA, dqk), lambda b, h, i: (b, h, i, 0)),
            pl.BlockSpec((1, 1, seq, dqk), lambda b, h, i: (b, h, 0, 0)),
            pl.BlockSpec((1, 1, seq, dv), lambda b, h, i: (b, h, 0, 0)),
            pl.BlockSpec((gu_rows, gu2d.shape[1]), gu_map),
            pl.BlockSpec((dn_rows, dn2d.shape[1]), dn_map),
        ],
        out_specs=(
            pl.BlockSpec((TQ_MLA, dv), lambda b, h, i: (b * nq + i, h)),
            pl.BlockSpec((gu_rows, gu2d.shape[1]), gu_map),
            pl.BlockSpec((dn_rows, dn2d.shape[1]), dn_map),
        ),
        out_shape=(
            jax.ShapeDtypeStruct((nseq * seq, nh * dv), BF16),
            jax.ShapeDtypeStruct(gu2d.shape, BF16),
            jax.ShapeDtypeStruct(dn2d.shape, BF16),
        ),
        compiler_params=_params(("arbitrary", "arbitrary", "arbitrary")),
        name="mla_attention",
    )(q, k, v, gu2d, dn2d)
    return o, gu_b.reshape(w_gu.shape), dn_b.reshape(w_dn.shape)


def _swa_kernel(sink_ref, q_ref,
                klo_p, klo_c, klo_n, khi_p, khi_c, khi_n,
                vlo_p, vlo_c, vlo_n, vhi_p, vhi_c, vhi_n,
                o_ref, klo_w, khi_w, vlo_w, vhi_w, *, seq):
    tq = q_ref.shape[0]
    w = WINDOW
    nk = tq + 2 * w
    for win, (p, c, n) in ((klo_w, (klo_p, klo_c, klo_n)), (khi_w, (khi_p, khi_c, khi_n)),
                           (vlo_w, (vlo_p, vlo_c, vlo_n)), (vhi_w, (vhi_p, vhi_c, vhi_n))):
        win[0:w, :] = p[...]
        win[w:w + tq, :] = c[...]
        win[w + tq:nk, :] = n[...]

    assert SWA_Q_HEADS // SWA_KV_HEADS == 4 and tq & (tq - 1) == 0
    q0 = pl.program_id(1) * tq
    row2 = lax.broadcasted_iota(jnp.int32, (2 * tq, nk), 0)
    col = lax.broadcasted_iota(jnp.int32, (2 * tq, nk), 1)
    rel = col - w - (row2 & (tq - 1))
    kpos = q0 - w + col
    valid = (rel <= w) & (rel >= -w) & (kpos >= 0) & (kpos < seq)
    top = lax.broadcasted_iota(jnp.int32, (2 * tq, 1), 0) < tq

    for g in range(SWA_KV_HEADS):
        gs = slice(g * LANE, (g + 1) * LANE)
        pa = slice((2 * g) * LANE, (2 * g + 1) * LANE)
        pb = slice((2 * g + 1) * LANE, (2 * g + 2) * LANE)
        qq = jnp.concatenate([q_ref[:, pa], q_ref[:, pb]], axis=0)
        acc = None
        for half, (kw, vw) in enumerate(((klo_w, vlo_w), (khi_w, vhi_w))):
            sink = jnp.where(top, sink_ref[4 * g + half], sink_ref[4 * g + 2 + half]) * LOG2E
            s = lax.dot_general(qq, kw[:, gs], (((1,), (1,)), ((), ())),
                                preferred_element_type=F32)
            s = jnp.where(valid, s, NEG_INF)
            mx = jnp.maximum(jnp.max(s, axis=-1, keepdims=True), sink)
            e = jnp.exp2(s - mx)
            den = jnp.sum(e, axis=-1, keepdims=True) + jnp.exp2(sink - mx)
            pv = jnp.dot(e.astype(BF16), vw[:, gs], preferred_element_type=F32) / den
            acc = pv if acc is None else acc + pv
        o_ref[:, pa] = acc[0:tq].astype(o_ref.dtype)
        o_ref[:, pb] = acc[tq:2 * tq].astype(o_ref.dtype)


def _swa_attention(sinks, qs, klo, khi, vlo, vhi, nseq, seq):
    t, qw = qs.shape
    kw = klo.shape[1]
    tq = TQ_SWA
    nq = seq // tq
    r = tq // WINDOW
    nwb = seq // WINDOW

    prev = pl.BlockSpec((WINDOW, kw), lambda b, i, s: (b * nwb + jnp.maximum(i * r - 1, 0), 0))
    cur = pl.BlockSpec((tq, kw), lambda b, i, s: (b * nq + i, 0))
    nxt = pl.BlockSpec((WINDOW, kw), lambda b, i, s: (b * nwb + jnp.minimum((i + 1) * r, nwb - 1), 0))
    grid_spec = pltpu.PrefetchScalarGridSpec(
        num_scalar_prefetch=1,
        grid=(nseq, nq),
        in_specs=[pl.BlockSpec((tq, qw), lambda b, i, s: (b * nq + i, 0))] + [prev, cur, nxt] * 4,
        out_specs=pl.BlockSpec((tq, qw), lambda b, i, s: (b * nq + i, 0)),
        scratch_shapes=[pltpu.VMEM((tq + 2 * WINDOW, kw), BF16)] * 4,
    )
    return pl.pallas_call(
        functools.partial(_swa_kernel, seq=seq),
        grid_spec=grid_spec,
        out_shape=jax.ShapeDtypeStruct((t, qw), BF16),
        compiler_params=_params(("arbitrary", "arbitrary")),
        name="swa_attention",
    )(sinks, qs, klo, klo, klo, khi, khi, khi, vlo, vlo, vlo, vhi, vhi, vhi)


def _merge_kernel(oa_ref, ob_ref, ga_ref, gb_ref, wa_ref, wb_ref, y_ref):
    a = jnp.dot(oa_ref[...], wa_ref[...], preferred_element_type=F32)
    b = jnp.dot(ob_ref[...], wb_ref[...], preferred_element_type=F32)
    y = jax.nn.sigmoid(ga_ref[...].astype(F32)) * a + jax.nn.sigmoid(gb_ref[...].astype(F32)) * b
    y_ref[...] = y.astype(y_ref.dtype)


def _merge(o_a, o_b, z, cols, w_a, w_b):
    t, d = o_a.shape[0], w_a.shape[1]
    tm, tn = TM_MERGE, TN_MERGE
    ga_blk, gb_blk = cols["g_a"] // tn, cols["g_b"] // tn
    return pl.pallas_call(
        _merge_kernel,
        grid=(t // tm, d // tn),
        in_specs=[
            pl.BlockSpec((tm, o_a.shape[1]), lambda i, j: (i, 0)),
            pl.BlockSpec((tm, o_b.shape[1]), lambda i, j: (i, 0)),
            pl.BlockSpec((tm, tn), lambda i, j: (i, ga_blk + j)),
            pl.BlockSpec((tm, tn), lambda i, j: (i, gb_blk + j)),
            pl.BlockSpec((w_a.shape[0], tn), lambda i, j: (0, j)),
            pl.BlockSpec((w_b.shape[0], tn), lambda i, j: (0, j)),
        ],
        out_specs=pl.BlockSpec((tm, tn), lambda i, j: (i, j)),
        out_shape=jax.ShapeDtypeStruct((t, d), BF16),
        compiler_params=_params(("arbitrary", "arbitrary")),
        name="branch_merge",
    )(o_a, o_b, z, z, w_a, w_b)


def _outproj_kernel(y_ref, xa_ref, xb_ref, mod_ref, gpm_ref, gpf_ref, wo_ref, wr_ref, br_ref,
                    x1_ref, h2_ref, idx_ref, gate_ref, cnt_ref, pack_ref, *, n_experts, first_tiles):
    tm = y_ref.shape[0]
    u = jnp.dot(y_ref[...], wo_ref[...], preferred_element_type=F32)
    x = jnp.where(pl.program_id(0) < first_tiles, xa_ref[...], xb_ref[...])
    x1 = x + mod_ref[0, 2:3, :] * _rms(u, gpm_ref[...])
    x1_ref[...] = x1
    h2 = _rms(x1, gpf_ref[...]) * (1.0 + mod_ref[0, 4:5, :]) + mod_ref[0, 3:4, :]
    h2b = h2.astype(BF16)
    bits = pltpu.bitcast(h2b.astype(F32), jnp.uint32)
    half = bits.shape[1] // 2
    pack_ref[...] = (bits[:, :half] >> 16) | (bits[:, half:] & jnp.uint32(0xFFFF0000))
    h2_ref[...] = pack_ref[...].reshape(h2_ref.shape)

    logits = jnp.dot(h2b, wr_ref[...], preferred_element_type=F32) + br_ref[...]
    lane = lax.broadcasted_iota(jnp.int32, (tm, LANE), 1)
    lane_f = lane.astype(F32)
    cur = jnp.where(lane < n_experts, logits, NEG_INF)
    vals, idxs = [], []
    for _ in range(TOP_K):
        m = jnp.max(cur, axis=-1, keepdims=True)
        ix = jnp.min(jnp.where(cur == m, lane_f, float(LANE)), axis=-1, keepdims=True)
        vals.append(m)
        idxs.append(ix)
        cur = jnp.where(lane_f == ix, NEG_INF, cur)
    es = [jnp.exp(v - vals[0]) for v in vals]
    den = es[0]
    for e in es[1:]:
        den = den + e
    idx_out = jnp.zeros((tm, LANE), F32)
    gate_out = jnp.zeros((tm, LANE), F32)
    sel = jnp.zeros((tm, LANE), F32)
    for r in range(TOP_K):
        idx_out = jnp.where(lane == r, idxs[r], idx_out)
        gate_out = jnp.where(lane == r, es[r] / den, gate_out)
        sel = sel + jnp.where(lane_f == idxs[r], 1.0, 0.0)
    idx_ref[...] = idx_out.astype(jnp.int32)
    gate_ref[...] = gate_out

    @pl.when(pl.program_id(0) == 0)
    def _():
        cnt_ref[...] = jnp.zeros_like(cnt_ref)

    cnt_ref[0:1, :] += jnp.sum(sel, axis=0, keepdims=True)


def _out_projection(y, xa, xb, mod3, g_pm, g_pf, w_out, w_router_p, b_router_p, n_experts, seq):
    t, d = y.shape
    tm = TM_OUT
    tps = seq // tm
    first_tiles = xa.shape[0] // tm
    tok = lambda i: (i, 0)
    const = lambda i: (0, 0)
    return pl.pallas_call(
        functools.partial(_outproj_kernel, n_experts=n_experts, first_tiles=first_tiles),
        grid=(t // tm,),
        in_specs=[
            pl.BlockSpec((tm, d), tok),
            *_two_group_specs(tm, d, first_tiles, 1),
            pl.BlockSpec((1, N_MOD, d), lambda i: (i // tps, 0, 0)),
            pl.BlockSpec((1, d), const),
            pl.BlockSpec((1, d), const),
            pl.BlockSpec((d, d), const),
            pl.BlockSpec((d, LANE), const),
            pl.BlockSpec((1, LANE), const),
        ],
        out_specs=(
            pl.BlockSpec((tm, d), tok),
            pl.BlockSpec((tm, 1, d // 2), lambda i: (i, 0, 0)),
            pl.BlockSpec((tm, LANE), tok),
            pl.BlockSpec((tm, LANE), tok),
            pl.BlockSpec((8, LANE), const),
        ),
        out_shape=(
            jax.ShapeDtypeStruct((t, d), F32),
            jax.ShapeDtypeStruct((t, 1, d // 2), jnp.uint32),
            jax.ShapeDtypeStruct((t, LANE), jnp.int32),
            jax.ShapeDtypeStruct((t, LANE), F32),
            jax.ShapeDtypeStruct((8, LANE), F32),
        ),
        scratch_shapes=[pltpu.VMEM((tm, d // 2), jnp.uint32)],
        compiler_params=_params(("arbitrary",)),
        name="out_projection_router",
    )(y, xa, xb, mod3, g_pm.reshape(1, d), g_pf.reshape(1, d), w_out, w_router_p, b_router_p)


def _positions_kernel(idx_ref, cnt_ref, dest_ref, meta_ref, zst_ref, carry_ref, ltri_ref, pst_ref,
                      *, n_experts, bm):
    tm = idx_ref.shape[0]
    nbp = meta_ref.shape[0]

    @pl.when(pl.program_id(0) == 0)
    def _():
        r = lax.broadcasted_iota(jnp.int32, (tm, tm), 0)
        c = lax.broadcasted_iota(jnp.int32, (tm, tm), 1)
        ltri_ref[...] = jnp.where(c < r, 1.0, 0.0).astype(BF16)
        carry_ref[...] = jnp.zeros_like(carry_ref)
        nblk = jnp.floor((cnt_ref[...] + float(bm - 1)) * (1.0 / bm))
        ur = lax.broadcasted_iota(jnp.int32, (LANE, LANE), 0)
        uc = lax.broadcasted_iota(jnp.int32, (LANE, LANE), 1)
        upper = jnp.where(ur <= uc, 1.0, 0.0).astype(BF16)
        pend = jnp.dot(nblk.astype(BF16), upper, preferred_element_type=F32)
        pst_ref[...] = (pend - nblk) * float(bm)
        lane8 = lax.broadcasted_iota(jnp.int32, (8, LANE), 1)
        zst_ref[...] = jnp.maximum(pend * float(bm) - float(bm), 0.0).astype(jnp.int32)
        pend0 = pend[0:1, :]
        n_used = jnp.sum(jnp.where(lane8[0:1, :] == n_experts - 1, pend0, 0.0), axis=-1, keepdims=True)
        blk = lax.broadcasted_iota(jnp.int32, (nbp, LANE), 0).astype(F32)
        lane = lax.broadcasted_iota(jnp.int32, (nbp, LANE), 1)
        passed = jnp.where((lane < n_experts) & (pend0 <= blk), 1.0, 0.0)
        blk_e = jnp.minimum(jnp.sum(passed, axis=-1, keepdims=True), float(n_experts - 1))
        mine = lane.astype(F32) == blk_e
        e_count = jnp.sum(jnp.where(mine, cnt_ref[0:1, :], 0.0), axis=-1, keepdims=True)
        e_first = jnp.sum(jnp.where(mine, pend0 - nblk[0:1, :], 0.0), axis=-1, keepdims=True)
        n_valid = jnp.clip(e_count - (blk[:, 0:1] - e_first) * float(bm), 0.0, float(bm))
        meta = jnp.where(lane == 0, blk_e, jnp.where(lane == 1, n_used, jnp.where(lane == 2, n_valid, 0.0)))
        meta_ref[...] = meta.astype(jnp.int32)

    lane = lax.broadcasted_iota(jnp.int32, (tm, LANE), 1)
    idx = idx_ref[...]
    hots = []
    sel = jnp.zeros((tm, LANE), F32)
    for r in range(TOP_K):
        col = jnp.sum(jnp.where(lane == r, idx, 0).astype(F32), axis=-1, keepdims=True)
        hot = jnp.where(lane.astype(F32) == col, 1.0, 0.0)
        hots.append(hot)
        sel = sel + hot
    rank = jnp.dot(ltri_ref[...], sel.astype(BF16), preferred_element_type=F32) + carry_ref[0:1, :]
    pos = pst_ref[0:1, :] + rank
    dest = jnp.zeros((tm, LANE), F32)
    for r in range(TOP_K):
        d = jnp.sum(hots[r] * pos, axis=-1, keepdims=True)
        dest = jnp.where(lane == r, d, dest)
    dest_ref[...] = dest.T[0:8, :].astype(jnp.int32)
    carry_ref[0:1, :] += jnp.sum(sel, axis=0, keepdims=True)


def _positions(idx, cnt, n_experts, n_blocks):
    t = idx.shape[0]
    tm = TM_POS
    nbp = -(-n_blocks // 8) * 8
    return pl.pallas_call(
        functools.partial(_positions_kernel, n_experts=n_experts, bm=BM_EXP),
        grid=(t // tm,),
        in_specs=[pl.BlockSpec((tm, LANE), lambda i: (i, 0)),
                  pl.BlockSpec((8, LANE), lambda i: (0, 0))],
        out_specs=(pl.BlockSpec((8, tm), lambda i: (0, i)),
                   pl.BlockSpec((nbp, LANE), lambda i: (0, 0)),
                   pl.BlockSpec((8, LANE), lambda i: (0, 0))),
        out_shape=(jax.ShapeDtypeStruct((8, t), jnp.int32),
                   jax.ShapeDtypeStruct((nbp, LANE), jnp.int32),
                   jax.ShapeDtypeStruct((8, LANE), jnp.int32)),
        scratch_shapes=[pltpu.VMEM((8, LANE), F32), pltpu.VMEM((tm, tm), BF16), pltpu.VMEM((8, LANE), F32)],
        compiler_params=_params(("arbitrary",)),
        name="dispatch_positions",
    )(idx, cnt)


def _dispatch_kernel(zst_ref, dest_ref, h2_ref, xs_ref, zero_ref, zsem, sem, *, n_experts, bm):
    i = pl.program_id(0)
    tm = dest_ref.shape[1]

    def zero_copy(e):
        return pltpu.make_async_copy(zero_ref, xs_ref.at[pl.ds(zst_ref[e], bm)], zsem)

    def tail_copy(j):
        return pltpu.make_async_copy(zero_ref, xs_ref.at[pl.ds(j * bm, bm)], zsem)

    @pl.when(i == 0)
    def _():
        zero_ref[...] = jnp.zeros_like(zero_ref)
        n_used = zst_ref[n_experts]
        n_blocks = xs_ref.shape[0] // bm
        for e in range(n_experts):
            zero_copy(e).start()
        lax.fori_loop(n_used, n_blocks, lambda j, c: (tail_copy(j).start(), c)[1], 0)
        for e in range(n_experts):
            zero_copy(e).wait()
        lax.fori_loop(n_used, n_blocks, lambda j, c: (tail_copy(j).wait(), c)[1], 0)

    def body(j, carry):
        for r in range(TOP_K):
            d = dest_ref[r, j]
            pltpu.make_async_copy(h2_ref.at[j], xs_ref.at[d], sem).start(priority=r % 2)
        return carry

    lax.fori_loop(0, tm, body, 0, unroll=8)
    for _ in range(TOP_K):
        pltpu.make_async_copy(h2_ref, xs_ref.at[pl.ds(0, tm)], sem).wait()


def _dispatch(zst, dest, h2, n_experts, n_blocks):
    t, _, d = h2.shape
    grid_spec = pltpu.PrefetchScalarGridSpec(
        num_scalar_prefetch=1,
        grid=(t // TM_DISPATCH,),
        in_specs=[pl.BlockSpec((8, TM_DISPATCH), lambda i, z: (0, i), memory_space=pltpu.SMEM),
                  pl.BlockSpec((TM_DISPATCH, 1, d), lambda i, z: (i, 0, 0))],
        out_specs=pl.BlockSpec(memory_space=pl.ANY),
        scratch_shapes=[pltpu.VMEM((BM_EXP, 1, d), h2.dtype),
                        pltpu.SemaphoreType.DMA(()), pltpu.SemaphoreType.DMA(())],
    )
    return pl.pallas_call(
        functools.partial(_dispatch_kernel, n_experts=n_experts, bm=BM_EXP),
        grid_spec=grid_spec,
        out_shape=jax.ShapeDtypeStruct((n_blocks * BM_EXP, 1, d), h2.dtype),
        compiler_params=_params(("arbitrary",)),
        name="row_dispatch",
    )(zst, dest, h2)


def _expert_kernel(sp_ref, xs_ref, wg_ref, wl_ref, bg_ref, bl_ref, wd_ref, bd_ref, ys_ref,
                   x2d_ref, xb_ref, acc_ref, *, n_blocks):
    j = pl.program_id(0)
    f = pl.program_id(1)
    nf = pl.num_programs(1)
    used = j < sp_ref[n_blocks]

    @pl.when(used & (f == 0))
    def _():
        x2d_ref[...] = xs_ref[...].reshape(x2d_ref.shape)
        words = x2d_ref[...]
        half = words.shape[1]
        xb_ref[:, :half] = pltpu.bitcast(words << 16, F32).astype(BF16)
        xb_ref[:, half:] = pltpu.bitcast(words & jnp.uint32(0xFFFF0000), F32).astype(BF16)

    @pl.when(jnp.logical_not(used) & (f == 0))
    def _():
        ys_ref[...] = jnp.zeros_like(ys_ref)

    def mlp(rows):
        xb = xb_ref[0:rows]
        glu = jnp.dot(xb, wg_ref[0], preferred_element_type=F32) + bg_ref[0]
        lin = jnp.dot(xb, wl_ref[0], preferred_element_type=F32) + bl_ref[0]
        glu = jnp.minimum(glu, SWIGLU_LIMIT)
        lin = jnp.clip(lin, -SWIGLU_LIMIT, SWIGLU_LIMIT)
        act = glu * jax.nn.sigmoid(SWIGLU_ALPHA * glu) * (lin + 1.0)
        prev = jnp.where(f == 0, jnp.broadcast_to(bd_ref[0], (rows, acc_ref.shape[1])), acc_ref[0:rows])
        acc_ref[0:rows] = prev + jnp.dot(act.astype(BF16), wd_ref[0], preferred_element_type=F32)

    bm = acc_ref.shape[0]
    sparse = sp_ref[n_blocks + 1 + jnp.minimum(j, n_blocks - 1)] <= bm // 2

    @pl.when(used & jnp.logical_not(sparse))
    def _():
        mlp(bm)

    @pl.when(used & sparse)
    def _():
        mlp(bm // 2)
        acc_ref[bm // 2:bm] = jnp.zeros((bm - bm // 2, acc_ref.shape[1]), F32)

    @pl.when(used & (f == nf - 1))
    def _():
        ys_ref[...] = acc_ref[...].reshape(ys_ref.shape)


def _experts(sp, xs, w_gu, b_gu, w_dn, b_dn, n_blocks):
    p, _, dh = xs.shape
    n_experts, d, ff2 = w_gu.shape
    assert d == 2 * dh
    ff = ff2 // 2
    nf = ff // TF_EXP
    bm = BM_EXP

    def blk(j, s):
        return jnp.minimum(j, s[n_blocks] - 1)

    def fch(j, f, s):
        return jnp.where(j < s[n_blocks], f, nf - 1)

    grid_spec = pltpu.PrefetchScalarGridSpec(
        num_scalar_prefetch=1,
        grid=(n_blocks, nf),
        in_specs=[
            pl.BlockSpec((bm, 1, dh), lambda j, f, s: (blk(j, s), 0, 0)),
            pl.BlockSpec((1, d, TF_EXP), lambda j, f, s: (s[blk(j, s)], 0, fch(j, f, s))),
            pl.BlockSpec((1, d, TF_EXP), lambda j, f, s: (s[blk(j, s)], 0, nf + fch(j, f, s))),
            pl.BlockSpec((1, 1, TF_EXP), lambda j, f, s: (s[blk(j, s)], 0, fch(j, f, s))),
            pl.BlockSpec((1, 1, TF_EXP), lambda j, f, s: (s[blk(j, s)], 0, nf + fch(j, f, s))),
            pl.BlockSpec((1, TF_EXP, d), lambda j, f, s: (s[blk(j, s)], fch(j, f, s), 0)),
            pl.BlockSpec((1, 1, d), lambda j, f, s: (s[blk(j, s)], 0, 0)),
        ],
        out_specs=pl.BlockSpec((bm, 1, d), lambda j, f, s: (j, 0, 0)),
        scratch_shapes=[pltpu.VMEM((bm, dh), jnp.uint32), pltpu.VMEM((bm, d), BF16), pltpu.VMEM((bm, d), F32)],
    )
    return pl.pallas_call(
        functools.partial(_expert_kernel, n_blocks=n_blocks),
        grid_spec=grid_spec,
        out_shape=jax.ShapeDtypeStruct((p, 1, d), F32),
        compiler_params=_params(("arbitrary", "arbitrary")),
        name="expert_mlp",
    )(sp, xs, w_gu, w_gu, b_gu.reshape(n_experts, 1, ff2), b_gu.reshape(n_experts, 1, ff2),
      w_dn, b_dn.reshape(n_experts, 1, d))


def _combine_kernel(dest_ref, dest_next_ref, gate_ref, x1_ref, mod_ref, g_ref, ys_ref, oa_ref, ob_ref,
                    buf_ref, row_ref, sems, *, first_tiles):
    tm = x1_ref.shape[0]
    i = pl.program_id(0)
    slot = i % 2

    def gather(table_ref, s):
        def body(j, carry):
            for r in range(TOP_K):
                d = table_ref[r, j]
                pltpu.make_async_copy(ys_ref.at[d], buf_ref.at[s, r, j], sems.at[s]).start(priority=r % 2)
            return carry

        lax.fori_loop(0, tm, body, 0, unroll=8)

    @pl.when(i == 0)
    def _():
        gather(dest_ref, 0)

    @pl.when(i + 1 < pl.num_programs(0))
    def _():
        gather(dest_next_ref, 1 - slot)

    for r in range(TOP_K):
        pltpu.make_async_copy(ys_ref.at[pl.ds(0, tm)], buf_ref.at[slot, r], sems.at[slot]).wait()

    gates = gate_ref[...]
    lane = lax.broadcasted_iota(jnp.int32, gates.shape, 1)
    f = None
    for r in range(TOP_K):
        g = jnp.sum(jnp.where(lane == r, gates, 0.0), axis=-1, keepdims=True)
        row_ref[...] = buf_ref[slot, r].reshape(row_ref.shape)
        term = row_ref[...] * g
        f = term if f is None else f + term
    out = x1_ref[...] + mod_ref[0, 5:6, :] * _rms(f, g_ref[...])

    @pl.when(pl.program_id(0) < first_tiles)
    def _():
        oa_ref[...] = out

    @pl.when(pl.program_id(0) >= first_tiles)
    def _():
        ob_ref[...] = out


def _combine(dest, gates, x1, mod3, g_post, ys, seq, n_first):
    t, d = x1.shape
    tm = TM_DMA
    tps = seq // tm
    first_tiles = n_first * tps
    n_tiles = t // tm
    return pl.pallas_call(
        functools.partial(_combine_kernel, first_tiles=first_tiles),
        grid=(n_tiles,),
        in_specs=[
            pl.BlockSpec((8, tm), lambda i: (0, i), memory_space=pltpu.SMEM),
            pl.BlockSpec((8, tm), lambda i: (0, jnp.minimum(i + 1, n_tiles - 1)), memory_space=pltpu.SMEM),
            pl.BlockSpec((tm, LANE), lambda i: (i, 0)),
            pl.BlockSpec((tm, d), lambda i: (i, 0)),
            pl.BlockSpec((1, N_MOD, d), lambda i: (i // tps, 0, 0)),
            pl.BlockSpec((1, d), lambda i: (0, 0)),
            pl.BlockSpec(memory_space=pl.ANY),
        ],
        out_specs=(pl.BlockSpec((tm, d), lambda i: (jnp.minimum(i, first_tiles - 1), 0)),
                   pl.BlockSpec((tm, d), lambda i: (jnp.maximum(i - first_tiles, 0), 0))),
        out_shape=(jax.ShapeDtypeStruct((first_tiles * tm, d), F32),
                   jax.ShapeDtypeStruct((t - first_tiles * tm, d), F32)),
        scratch_shapes=[pltpu.VMEM((2, TOP_K, tm, 1, d), F32), pltpu.VMEM((tm, d), F32),
                        pltpu.SemaphoreType.DMA((2,))],
        compiler_params=_params(("arbitrary",)),
        name="expert_combine",
    )(dest, dest, gates, x1, mod3, g_post.reshape(1, d), ys)


def _layout_w_in(w_in, qr, kvr):
    qsw = SWA_Q_HEADS * SWA_HEAD_DIM
    ksw = SWA_KV_HEADS * SWA_HEAD_DIM
    d = w_in.shape[0]
    sizes = (qr, kvr, MLA_ROPE, qsw, ksw, ksw, d, d)
    offs = [0]
    for s in sizes:
        offs.append(offs[-1] + s)
    c_q, c_kv, k_rope, q_s, k_s, v_s, g_a, g_b = [w_in[:, offs[i]:offs[i + 1]] for i in range(8)]
    parts = (("q_s", q_s), ("g_a", g_a), ("g_b", g_b), ("c_q", c_q), ("c_kv", c_kv),
             ("k_s", k_s), ("v_s", v_s), ("k_rope", jnp.concatenate([k_rope, k_rope], axis=1)))
    cols, off = {}, 0
    for name, p in parts:
        cols[name] = off
        off += p.shape[1]
    return jnp.concatenate([p for _, p in parts], axis=1).astype(BF16), cols


def _layout_w_uq(w_uq):
    r = w_uq.shape[0]
    w = w_uq.reshape(r, MLA_HEADS, MLA_NOPE + MLA_ROPE)
    nope = w[:, :, :MLA_NOPE].reshape(r, MLA_HEADS * MLA_NOPE)
    pe = w[:, :, MLA_NOPE:].reshape(r, MLA_HEADS * MLA_ROPE)
    return jnp.concatenate([nope, pe], axis=1).astype(BF16)


def _layout_w_ukv(w_ukv):
    r = w_ukv.shape[0]
    w = w_ukv.reshape(r, MLA_HEADS, MLA_NOPE + MLA_V)
    kn = w[:, :, :MLA_NOPE].reshape(r, MLA_HEADS * MLA_NOPE)
    v = w[:, :, MLA_NOPE:].reshape(r, MLA_HEADS * MLA_V)
    return jnp.concatenate([kn, v], axis=1).astype(BF16)


def _rope_tables(seq):
    half = MLA_ROPE // 2
    freqs = jnp.power(ROPE_THETA, -2.0 * jnp.arange(half, dtype=F32) / MLA_ROPE)
    ang = jnp.arange(seq, dtype=F32)[:, None] * freqs[None, :]
    cos, sin = jnp.cos(ang), jnp.sin(ang)
    return (jnp.concatenate([cos, cos, cos, cos], axis=1),
            jnp.concatenate([-sin, sin, -sin, sin], axis=1))


def kernel(x_prompt, x_sample, c_prompt, c_sample, w_ada, b_ada, g_pre_mix, w_in, g_q_lat, w_uq,
           g_kv_lat, w_ukv, attn_sinks, w_br_mla, w_br_swa, w_out, g_post_mix, g_pre_ffn,
           w_router, b_router, w_gu, b_gu, w_dn, b_dn, g_post_ffn):
    assert MLA_ROPE == SWA_HEAD_DIM == 64 and MLA_NOPE == MLA_V == LANE
    nb_p, seq, d = x_prompt.shape
    nb_s = x_sample.shape[0]
    assert x_sample.shape[1] == seq
    nseq = nb_p + nb_s
    t = nseq * seq
    n_experts = w_router.shape[-1]
    n_blocks = t * TOP_K // BM_EXP + n_experts

    xa, xb = x_prompt.reshape(nb_p * seq, d), x_sample.reshape(nb_s * seq, d)
    c8 = jnp.concatenate([c_prompt, c_sample, jnp.zeros((8 - nseq, d), F32)], axis=0)
    cos, sin = _rope_tables(seq)

    for l in range(w_ada.shape[0]):
        w_in_p, cols = _layout_w_in(w_in[l], g_q_lat.shape[-1], g_kv_lat.shape[-1])
        w_uq_p = _layout_w_uq(w_uq[l])
        w_ukv_p = _layout_w_ukv(w_ukv[l])
        w_router_p = jnp.pad(w_router[l], ((0, 0), (0, LANE - n_experts))).astype(BF16)
        b_router_p = jnp.pad(b_router[l], (0, LANE - n_experts)).reshape(1, LANE)

        mod = _modulation(c8, w_ada[l], b_ada[l])
        mod3 = mod[:nseq].reshape(nseq, N_MOD, d)
        z = _in_projection(xa, xb, mod3, g_pre_mix[l], w_in_p, seq)
        q, k, v, qs, klo, khi, vlo, vhi = _post_projection(
            z, cols, cos, sin, g_q_lat[l], g_kv_lat[l], w_uq_p, w_ukv_p, nseq, seq)
        o_a, w_gu_b, w_dn_b = _mla_attention(q, k, v, w_gu[l], w_dn[l])
        o_b = _swa_attention(attn_sinks[l], qs, klo, khi, vlo, vhi, nseq, seq)
        y = _merge(o_a, o_b, z, cols, w_br_mla[l].astype(BF16), w_br_swa[l].astype(BF16))
        x1, h2, idx, gates, cnt = _out_projection(
            y, xa, xb, mod3, g_post_mix[l], g_pre_ffn[l], w_out[l].astype(BF16), w_router_p, b_router_p,
            n_experts, seq)
        dest, meta, zst = _positions(idx, cnt, n_experts, n_blocks)
        sp = jnp.concatenate([meta[:n_blocks, 0], meta[0:1, 1], meta[:n_blocks, 2]])
        zmeta = jnp.concatenate([zst[0, :n_experts], meta[0:1, 1]])
        xs = _dispatch(zmeta, dest, h2, n_experts, n_blocks)
        ys = _experts(sp, xs, w_gu_b, b_gu[l], w_dn_b, b_dn[l], n_blocks)
        xa, xb = _combine(dest, gates, x1, mod3, g_post_ffn[l], ys, seq, nb_p)

    return (xa.reshape(nb_p, seq, d), xb.reshape(nb_s, seq, d))
```

```python
import functools

import jax
import jax.numpy as jnp
from jax import lax
from jax.experimental import pallas as pl
from jax.experimental.pallas import tpu as pltpu

MLA_HEADS = 16
MLA_NOPE = 128
MLA_ROPE = 64
MLA_V = 128
SWA_Q_HEADS = 32
SWA_KV_HEADS = 8
SWA_HEAD_DIM = 64
WINDOW = 128
TOP_K = 4
SWIGLU_LIMIT = 7.0
SWIGLU_ALPHA = 1.702
ROPE_THETA = 10000.0
RMS_EPS = 1e-6
N_MOD = 6

LANE = 128
VMEM_LIMIT = 56 << 20

TM_IN = 512
TN_IN = 1664
TM_POST = 256
TQ_MLA = 1024
KV_CHUNKS_MLA = (1024, 1024, 1024, 1024)
TQ_SWA = 256
TM_MERGE = 1024
TN_MERGE = 1024
TM_OUT = 512
TM_POS = 512
TM_DISPATCH = 512
TM_DMA = 256
BM_EXP = 512
TF_EXP = 1024

F32 = jnp.float32
BF16 = jnp.bfloat16
NEG_INF = float("-inf")
LOG2E = 1.4426950408889634


def _params(sem):
    return pltpu.CompilerParams(dimension_semantics=sem, vmem_limit_bytes=VMEM_LIMIT)


def _rms(x, g):
    return x * lax.rsqrt(jnp.mean(x * x, axis=-1, keepdims=True) + RMS_EPS) * g


def _mod_kernel(c_ref, w_ref, b_ref, o_ref):
    c = c_ref[...]
    a = (c * jax.nn.sigmoid(c)).astype(BF16)
    o_ref[...] = jnp.dot(a, w_ref[...].astype(BF16), preferred_element_type=F32) + b_ref[...]


def _modulation(c8, w_ada, b_ada):
    d, n = w_ada.shape
    tn = 1024
    return pl.pallas_call(
        _mod_kernel,
        grid=(n // tn,),
        in_specs=[
            pl.BlockSpec((8, d), lambda j: (0, 0)),
            pl.BlockSpec((d, tn), lambda j: (0, j)),
            pl.BlockSpec((1, tn), lambda j: (0, j)),
        ],
        out_specs=pl.BlockSpec((8, tn), lambda j: (0, j)),
        out_shape=jax.ShapeDtypeStruct((8, n), F32),
        compiler_params=_params(("arbitrary",)),
        name="adaln_mod",
    )(c8, w_ada, b_ada.reshape(1, n))


def _two_group_specs(tm, d, first_tiles, n_grid_axes):
    if n_grid_axes == 1:
        return (pl.BlockSpec((tm, d), lambda i: (jnp.minimum(i, first_tiles - 1), 0)),
                pl.BlockSpec((tm, d), lambda i: (jnp.maximum(i - first_tiles, 0), 0)))
    return (pl.BlockSpec((tm, d), lambda i, j: (jnp.minimum(i, first_tiles - 1), 0)),
            pl.BlockSpec((tm, d), lambda i, j: (jnp.maximum(i - first_tiles, 0), 0)))


def _inproj_kernel(xa_ref, xb_ref, mod_ref, g_ref, w_ref, z_ref, h_ref, *, first_tiles):
    @pl.when(pl.program_id(1) == 0)
    def _():
        x = jnp.where(pl.program_id(0) < first_tiles, xa_ref[...], xb_ref[...])
        h = _rms(x, g_ref[...]) * (1.0 + mod_ref[0, 1:2, :]) + mod_ref[0, 0:1, :]
        h_ref[...] = h.astype(BF16)

    z_ref[...] = jnp.dot(h_ref[...], w_ref[...], preferred_element_type=F32).astype(z_ref.dtype)


def _in_projection(xa, xb, mod3, g, w_in_p, seq):
    d = xa.shape[1]
    t = xa.shape[0] + xb.shape[0]
    n = w_in_p.shape[1]
    tiles_per_seq = seq // TM_IN
    first_tiles = xa.shape[0] // TM_IN
    return pl.pallas_call(
        functools.partial(_inproj_kernel, first_tiles=first_tiles),
        grid=(t // TM_IN, n // TN_IN),
        in_specs=[
            *_two_group_specs(TM_IN, d, first_tiles, 2),
            pl.BlockSpec((1, N_MOD, d), lambda i, j: (i // tiles_per_seq, 0, 0)),
            pl.BlockSpec((1, d), lambda i, j: (0, 0)),
            pl.BlockSpec((d, TN_IN), lambda i, j: (0, j)),
        ],
        out_specs=pl.BlockSpec((TM_IN, TN_IN), lambda i, j: (i, j)),
        out_shape=jax.ShapeDtypeStruct((t, n), BF16),
        scratch_shapes=[pltpu.VMEM((TM_IN, d), BF16)],
        compiler_params=_params(("arbitrary", "arbitrary")),
        name="in_projection",
    )(xa, xb, mod3, g.reshape(1, d), w_in_p)


def _postproj_kernel(cq_ref, ckv_ref, qs_ref, ks_ref, vs_ref, kr_ref, cos_ref, sin_ref,
                     gq_ref, gkv_ref, wuq_ref, wukv_ref,
                     q_ref, k_ref, v_ref, qso_ref, klo_ref, khi_ref, vlo_ref, vhi_ref):
    tm = cq_ref.shape[0]
    cos = cos_ref[...]
    sin = sin_ref[...]
    lane = lax.broadcasted_iota(jnp.int32, (tm, LANE), 1)
    first_half = (lane & 63) < 32
    low = lane < 64

    def rope(x):
        rot = jnp.where(first_half, pltpu.roll(x, LANE - 32, 1), pltpu.roll(x, 32, 1))
        return x * cos + rot * sin

    nh = MLA_HEADS
    scale = float((MLA_NOPE + MLA_ROPE) ** -0.5) * LOG2E
    cqn = _rms(cq_ref[...].astype(F32), gq_ref[...]).astype(BF16)
    q = jnp.dot(cqn, wuq_ref[...], preferred_element_type=F32) * scale
    for h in range(nh):
        q_ref[0, h, :, 0:LANE] = q[:, h * LANE:(h + 1) * LANE].astype(BF16)
    for m in range(nh // 2):
        pe = rope(q[:, (nh + m) * LANE:(nh + m + 1) * LANE]).astype(BF16)
        q_ref[0, 2 * m, :, LANE:2 * LANE] = pe
        q_ref[0, 2 * m + 1, :, LANE:2 * LANE] = pe

    ckvn = _rms(ckv_ref[...].astype(F32), gkv_ref[...]).astype(BF16)
    kv = jnp.dot(ckvn, wukv_ref[...], preferred_element_type=F32)
    kr = rope(kr_ref[...].astype(F32))
    kpe_lo = jnp.where(low, kr, 0.0).astype(BF16)
    kpe_hi = jnp.where(low, 0.0, kr).astype(BF16)
    for h in range(nh):
        k_ref[0, h, :, 0:LANE] = kv[:, h * LANE:(h + 1) * LANE].astype(BF16)
        k_ref[0, h, :, LANE:2 * LANE] = kpe_lo if h % 2 == 0 else kpe_hi
        v_ref[0, h, :, :] = kv[:, (nh + h) * LANE:(nh + h + 1) * LANE].astype(BF16)

    swa_scale = float(SWA_HEAD_DIM ** -0.5) * LOG2E
    for m in range(SWA_Q_HEADS // 2):
        x = qs_ref[:, m * LANE:(m + 1) * LANE].astype(F32)
        qso_ref[:, m * LANE:(m + 1) * LANE] = (rope(x) * swa_scale).astype(BF16)

    for m in range(SWA_KV_HEADS // 2):
        sl = slice(m * LANE, (m + 1) * LANE)
        for src_ref, lo_ref, hi_ref, roped in ((ks_ref, klo_ref, khi_ref, True),
                                               (vs_ref, vlo_ref, vhi_ref, False)):
            a = src_ref[:, sl].astype(F32)
            if roped:
                a = rope(a)
            b = pltpu.roll(a, 64, 1)
            e0 = slice((2 * m) * LANE, (2 * m + 1) * LANE)
            e1 = slice((2 * m + 1) * LANE, (2 * m + 2) * LANE)
            lo_ref[:, e0] = jnp.where(low, a, 0.0).astype(BF16)
            hi_ref[:, e0] = jnp.where(low, 0.0, b).astype(BF16)
            lo_ref[:, e1] = jnp.where(low, b, 0.0).astype(BF16)
            hi_ref[:, e1] = jnp.where(low, 0.0, a).astype(BF16)


def _post_projection(z, cols, cos, sin, g_q, g_kv, w_uq_p, w_ukv_p, nseq, seq):
    t = z.shape[0]
    tm = TM_POST
    tps = seq // tm
    nh = MLA_HEADS
    qr, kvr = g_q.shape[0], g_kv.shape[0]
    qsw = SWA_Q_HEADS * SWA_HEAD_DIM
    ksw = SWA_KV_HEADS * SWA_HEAD_DIM

    def zspec(width, off):
        blk = off // width
        return pl.BlockSpec((tm, width), lambda i: (i, blk))

    head_map = lambda i: (i // tps, 0, i % tps, 0)
    tok_map = lambda i: (i, 0)
    out_shapes = (
        jax.ShapeDtypeStruct((nseq, nh, seq, 2 * LANE), BF16),
        jax.ShapeDtypeStruct((nseq, nh, seq, 2 * LANE), BF16),
        jax.ShapeDtypeStruct((nseq, nh, seq, LANE), BF16),
        jax.ShapeDtypeStruct((t, qsw), BF16),
        jax.ShapeDtypeStruct((t, SWA_KV_HEADS * LANE), BF16),
        jax.ShapeDtypeStruct((t, SWA_KV_HEADS * LANE), BF16),
        jax.ShapeDtypeStruct((t, SWA_KV_HEADS * LANE), BF16),
        jax.ShapeDtypeStruct((t, SWA_KV_HEADS * LANE), BF16),
    )
    out_specs = (
        pl.BlockSpec((1, nh, tm, 2 * LANE), head_map),
        pl.BlockSpec((1, nh, tm, 2 * LANE), head_map),
        pl.BlockSpec((1, nh, tm, LANE), head_map),
        pl.BlockSpec((tm, qsw), tok_map),
        pl.BlockSpec((tm, SWA_KV_HEADS * LANE), tok_map),
        pl.BlockSpec((tm, SWA_KV_HEADS * LANE), tok_map),
        pl.BlockSpec((tm, SWA_KV_HEADS * LANE), tok_map),
        pl.BlockSpec((tm, SWA_KV_HEADS * LANE), tok_map),
    )
    return pl.pallas_call(
        _postproj_kernel,
        grid=(t // tm,),
        in_specs=[
            zspec(qr, cols["c_q"]), zspec(kvr, cols["c_kv"]), zspec(qsw, cols["q_s"]),
            zspec(ksw, cols["k_s"]), zspec(ksw, cols["v_s"]), zspec(LANE, cols["k_rope"]),
            pl.BlockSpec((tm, LANE), lambda i: (i % tps, 0)),
            pl.BlockSpec((tm, LANE), lambda i: (i % tps, 0)),
            pl.BlockSpec((1, qr), lambda i: (0, 0)),
            pl.BlockSpec((1, kvr), lambda i: (0, 0)),
            pl.BlockSpec(w_uq_p.shape, lambda i: (0, 0)),
            pl.BlockSpec(w_ukv_p.shape, lambda i: (0, 0)),
        ],
        out_specs=out_specs,
        out_shape=out_shapes,
        compiler_params=_params(("arbitrary",)),
        name="post_projection",
    )(z, z, z, z, z, z, cos, sin, g_q.reshape(1, qr), g_kv.reshape(1, kvr), w_uq_p, w_ukv_p)


def _mla_kernel(q_ref, k_ref, v_ref, wgu_ref, wdn_ref, o_ref, wgu_o_ref, wdn_o_ref, *, n_gu):
    step = (pl.program_id(0) * pl.num_programs(1) + pl.program_id(1)) * pl.num_programs(2) + pl.program_id(2)

    @pl.when(step < n_gu)
    def _():
        wgu_o_ref[...] = wgu_ref[...].astype(BF16)

    @pl.when(step >= n_gu)
    def _():
        wdn_o_ref[...] = wdn_ref[...].astype(BF16)

    q = q_ref[0, 0]
    seq = k_ref.shape[2]
    m = l = acc = None
    assert sum(KV_CHUNKS_MLA) == seq
    start = 0
    for width in KV_CHUNKS_MLA:
        rows = slice(start, start + width)
        start += width
        s = lax.dot_general(q, k_ref[0, 0, rows, :], (((1,), (1,)), ((), ())),
                            preferred_element_type=F32)
        mc = jnp.max(s, axis=-1, keepdims=True)
        m_new = mc if m is None else jnp.maximum(m, mc)
        p = jnp.exp2(s - m_new)
        ps = jnp.sum(p, axis=-1, keepdims=True)
        pv = jnp.dot(p.astype(BF16), v_ref[0, 0, rows, :], preferred_element_type=F32)
        if m is None:
            l, acc = ps, pv
        else:
            alpha = jnp.exp2(m - m_new)
            l = alpha * l + ps
            acc = alpha * acc + pv
        m = m_new
    o_ref[...] = (acc / l).astype(o_ref.dtype)


def _mla_attention(q, k, v, w_gu, w_dn):
    nseq, nh, seq, dqk = q.shape
    dv = v.shape[-1]
    nq = seq // TQ_MLA
    n_steps = nseq * nh * nq
    gu2d = w_gu.reshape(-1, w_gu.shape[-1])
    dn2d = w_dn.reshape(-1, w_dn.shape[-1])
    n_gu, n_dn = 2 * n_steps // 3, n_steps // 3
    assert n_gu + n_dn == n_steps and gu2d.shape[0] % n_gu == 0 and dn2d.shape[0] % n_dn == 0
    gu_rows, dn_rows = gu2d.shape[0] // n_gu, dn2d.shape[0] // n_dn
    assert gu_rows % 16 == 0 and dn_rows % 16 == 0

    def step(b, h, i):
        return (b * nh + h) * nq + i

    gu_map = lambda b, h, i: (jnp.minimum(step(b, h, i), n_gu - 1), 0)
    dn_map = lambda b, h, i: (jnp.maximum(step(b, h, i) - n_gu, 0), 0)
    o, gu_b, dn_b = pl.pallas_call(
        functools.partial(_mla_kernel, n_gu=n_gu),
        grid=(nseq, nh, nq),
        in_specs=[
            pl.BlockSpec((1, 1, TQ_MLA, dqk), lambda b, h, i: (b, h, i, 0)),
            pl.BlockSpec((1, 1, seq, dqk), lambda b, h, i: (b, h, 0, 0)),
            pl.BlockSpec((1, 1, seq, dv), lambda b, h, i: (b, h, 0, 0)),
            pl.BlockSpec((gu_rows, gu2d.shape[1]), gu_map),
            pl.BlockSpec((dn_rows, dn2d.shape[1]), dn_map),
        ],
        out_specs=(
            pl.BlockSpec((TQ_MLA, dv), lambda b, h, i: (b * nq + i, h)),
            pl.BlockSpec((gu_rows, gu2d.shape[1]), gu_map),
            pl.BlockSpec((dn_rows, dn2d.shape[1]), dn_map),
        ),
        out_shape=(
            jax.ShapeDtypeStruct((nseq * seq, nh * dv), BF16),
            jax.ShapeDtypeStruct(gu2d.shape, BF16),
            jax.ShapeDtypeStruct(dn2d.shape, BF16),
        ),
        compiler_params=_params(("arbitrary", "arbitrary", "arbitrary")),
        name="mla_attention",
    )(q, k, v, gu2d, dn2d)
    return o, gu_b.reshape(w_gu.shape), dn_b.reshape(w_dn.shape)


def _swa_kernel(sink_ref, q_ref,
                klo_p, klo_c, klo_n, khi_p, khi_c, khi_n,
                vlo_p, vlo_c, vlo_n, vhi_p, vhi_c, vhi_n,
                o_ref, klo_w, khi_w, vlo_w, vhi_w, *, seq):
    tq = q_ref.shape[0]
    w = WINDOW
    nk = tq + 2 * w
    for win, (p, c, n) in ((klo_w, (klo_p, klo_c, klo_n)), (khi_w, (khi_p, khi_c, khi_n)),
                           (vlo_w, (vlo_p, vlo_c, vlo_n)), (vhi_w, (vhi_p, vhi_c, vhi_n))):
        win[0:w, :] = p[...]
        win[w:w + tq, :] = c[...]
        win[w + tq:nk, :] = n[...]

    assert SWA_Q_HEADS // SWA_KV_HEADS == 4 and tq & (tq - 1) == 0
    q0 = pl.program_id(1) * tq
    row2 = lax.broadcasted_iota(jnp.int32, (2 * tq, nk), 0)
    col = lax.broadcasted_iota(jnp.int32, (2 * tq, nk), 1)
    rel = col - w - (row2 & (tq - 1))
    kpos = q0 - w + col
    valid = (rel <= w) & (rel >= -w) & (kpos >= 0) & (kpos < seq)
    top = lax.broadcasted_iota(jnp.int32, (2 * tq, 1), 0) < tq

    for g in range(SWA_KV_HEADS):
        gs = slice(g * LANE, (g + 1) * LANE)
        pa = slice((2 * g) * LANE, (2 * g + 1) * LANE)
        pb = slice((2 * g + 1) * LANE, (2 * g + 2) * LANE)
        qq = jnp.concatenate([q_ref[:, pa], q_ref[:, pb]], axis=0)
        acc = None
        for half, (kw, vw) in enumerate(((klo_w, vlo_w), (khi_w, vhi_w))):
            sink = jnp.where(top, sink_ref[4 * g + half], sink_ref[4 * g + 2 + half]) * LOG2E
            s = lax.dot_general(qq, kw[:, gs], (((1,), (1,)), ((), ())),
                                preferred_element_type=F32)
            s = jnp.where(valid, s, NEG_INF)
            mx = jnp.maximum(jnp.max(s, axis=-1, keepdims=True), sink)
            e = jnp.exp2(s - mx)
            den = jnp.sum(e, axis=-1, keepdims=True) + jnp.exp2(sink - mx)
            pv = jnp.dot(e.astype(BF16), vw[:, gs], preferred_element_type=F32) / den
            acc = pv if acc is None else acc + pv
        o_ref[:, pa] = acc[0:tq].astype(o_ref.dtype)
        o_ref[:, pb] = acc[tq:2 * tq].astype(o_ref.dtype)


def _swa_attention(sinks, qs, klo, khi, vlo, vhi, nseq, seq):
    t, qw = qs.shape
    kw = klo.shape[1]
    tq = TQ_SWA
    nq = seq // tq
    r = tq // WINDOW
    nwb = seq // WINDOW

    prev = pl.BlockSpec((WINDOW, kw), lambda b, i, s: (b * nwb + jnp.maximum(i * r - 1, 0), 0))
    cur = pl.BlockSpec((tq, kw), lambda b, i, s: (b * nq + i, 0))
    nxt = pl.BlockSpec((WINDOW, kw), lambda b, i, s: (b * nwb + jnp.minimum((i + 1) * r, nwb - 1), 0))
    grid_spec = pltpu.PrefetchScalarGridSpec(
        num_scalar_prefetch=1,
        grid=(nseq, nq),
        in_specs=[pl.BlockSpec((tq, qw), lambda b, i, s: (b * nq + i, 0))] + [prev, cur, nxt] * 4,
        out_specs=pl.BlockSpec((tq, qw), lambda b, i, s: (b * nq + i, 0)),
        scratch_shapes=[pltpu.VMEM((tq + 2 * WINDOW, kw), BF16)] * 4,
    )
    return pl.pallas_call(
        functools.partial(_swa_kernel, seq=seq),
        grid_spec=grid_spec,
        out_shape=jax.ShapeDtypeStruct((t, qw), BF16),
        compiler_params=_params(("arbitrary", "arbitrary")),
        name="swa_attention",
    )(sinks, qs, klo, klo, klo, khi, khi, khi, vlo, vlo, vlo, vhi, vhi, vhi)


def _merge_kernel(oa_ref, ob_ref, ga_ref, gb_ref, wa_ref, wb_ref, y_ref):
    a = jnp.dot(oa_ref[...], wa_ref[...], preferred_element_type=F32)
    b = jnp.dot(ob_ref[...], wb_ref[...], preferred_element_type=F32)
    y = jax.nn.sigmoid(ga_ref[...].astype(F32)) * a + jax.nn.sigmoid(gb_ref[...].astype(F32)) * b
    y_ref[...] = y.astype(y_ref.dtype)


def _merge(o_a, o_b, z, cols, w_a, w_b):
    t, d = o_a.shape[0], w_a.shape[1]
    tm, tn = TM_MERGE, TN_MERGE
    ga_blk, gb_blk = cols["g_a"] // tn, cols["g_b"] // tn
    return pl.pallas_call(
        _merge_kernel,
        grid=(t // tm, d // tn),
        in_specs=[
            pl.BlockSpec((tm, o_a.shape[1]), lambda i, j: (i, 0)),
            pl.BlockSpec((tm, o_b.shape[1]), lambda i, j: (i, 0)),
            pl.BlockSpec((tm, tn), lambda i, j: (i, ga_blk + j)),
            pl.BlockSpec((tm, tn), lambda i, j: (i, gb_blk + j)),
            pl.BlockSpec((w_a.shape[0], tn), lambda i, j: (0, j)),
            pl.BlockSpec((w_b.shape[0], tn), lambda i, j: (0, j)),
        ],
        out_specs=pl.BlockSpec((tm, tn), lambda i, j: (i, j)),
        out_shape=jax.ShapeDtypeStruct((t, d), BF16),
        compiler_params=_params(("arbitrary", "arbitrary")),
        name="branch_merge",
    )(o_a, o_b, z, z, w_a, w_b)


def _outproj_kernel(y_ref, xa_ref, xb_ref, mod_ref, gpm_ref, gpf_ref, wo_ref, wr_ref, br_ref,
                    x1_ref, h2_ref, idx_ref, gate_ref, cnt_ref, pack_ref, *, n_experts, first_tiles):
    tm = y_ref.shape[0]
    u = jnp.dot(y_ref[...], wo_ref[...], preferred_element_type=F32)
    x = jnp.where(pl.program_id(0) < first_tiles, xa_ref[...], xb_ref[...])
    x1 = x + mod_ref[0, 2:3, :] * _rms(u, gpm_ref[...])
    x1_ref[...] = x1
    h2 = _rms(x1, gpf_ref[...]) * (1.0 + mod_ref[0, 4:5, :]) + mod_ref[0, 3:4, :]
    h2b = h2.astype(BF16)
    bits = pltpu.bitcast(h2b.astype(F32), jnp.uint32)
    half = bits.shape[1] // 2
    pack_ref[...] = (bits[:, :half] >> 16) | (bits[:, half:] & jnp.uint32(0xFFFF0000))
    h2_ref[...] = pack_ref[...].reshape(h2_ref.shape)

    logits = jnp.dot(h2b, wr_ref[...], preferred_element_type=F32) + br_ref[...]
    lane = lax.broadcasted_iota(jnp.int32, (tm, LANE), 1)
    lane_f = lane.astype(F32)
    cur = jnp.where(lane < n_experts, logits, NEG_INF)
    vals, idxs = [], []
    for _ in range(TOP_K):
        m = jnp.max(cur, axis=-1, keepdims=True)
        ix = jnp.min(jnp.where(cur == m, lane_f, float(LANE)), axis=-1, keepdims=True)
        vals.append(m)
        idxs.append(ix)
        cur = jnp.where(lane_f == ix, NEG_INF, cur)
    es = [jnp.exp(v - vals[0]) for v in vals]
    den = es[0]
    for e in es[1:]:
        den = den + e
    idx_out = jnp.zeros((tm, LANE), F32)
    gate_out = jnp.zeros((tm, LANE), F32)
    sel = jnp.zeros((tm, LANE), F32)
    for r in range(TOP_K):
        idx_out = jnp.where(lane == r, idxs[r], idx_out)
        gate_out = jnp.where(lane == r, es[r] / den, gate_out)
        sel = sel + jnp.where(lane_f == idxs[r], 1.0, 0.0)
    idx_ref[...] = idx_out.astype(jnp.int32)
    gate_ref[...] = gate_out

    @pl.when(pl.program_id(0) == 0)
    def _():
        cnt_ref[...] = jnp.zeros_like(cnt_ref)

    cnt_ref[0:1, :] += jnp.sum(sel, axis=0, keepdims=True)


def _out_projection(y, xa, xb, mod3, g_pm, g_pf, w_out, w_router_p, b_router_p, n_experts, seq):
    t, d = y.shape
    tm = TM_OUT
    tps = seq // tm
    first_tiles = xa.shape[0] // tm
    tok = lambda i: (i, 0)
    const = lambda i: (0, 0)
    return pl.pallas_call(
        functools.partial(_outproj_kernel, n_experts=n_experts, first_tiles=first_tiles),
        grid=(t // tm,),
        in_specs=[
            pl.BlockSpec((tm, d), tok),
            *_two_group_specs(tm, d, first_tiles, 1),
            pl.BlockSpec((1, N_MOD, d), lambda i: (i // tps, 0, 0)),
            pl.BlockSpec((1, d), const),
            pl.BlockSpec((1, d), const),
            pl.BlockSpec((d, d), const),
            pl.BlockSpec((d, LANE), const),
            pl.BlockSpec((1, LANE), const),
        ],
        out_specs=(
            pl.BlockSpec((tm, d), tok),
            pl.BlockSpec((tm, 1, d // 2), lambda i: (i, 0, 0)),
            pl.BlockSpec((tm, LANE), tok),
            pl.BlockSpec((tm, LANE), tok),
            pl.BlockSpec((8, LANE), const),
        ),
        out_shape=(
            jax.ShapeDtypeStruct((t, d), F32),
            jax.ShapeDtypeStruct((t, 1, d // 2), jnp.uint32),
            jax.ShapeDtypeStruct((t, LANE), jnp.int32),
            jax.ShapeDtypeStruct((t, LANE), F32),
            jax.ShapeDtypeStruct((8, LANE), F32),
        ),
        scratch_shapes=[pltpu.VMEM((tm, d // 2), jnp.uint32)],
        compiler_params=_params(("arbitrary",)),
        name="out_projection_router",
    )(y, xa, xb, mod3, g_pm.reshape(1, d), g_pf.reshape(1, d), w_out, w_router_p, b_router_p)


def _positions_kernel(idx_ref, cnt_ref, dest_ref, meta_ref, zst_ref, carry_ref, ltri_ref, pst_ref,
                      *, n_experts, bm):
    tm = idx_ref.shape[0]
    nbp = meta_ref.shape[0]

    @pl.when(pl.program_id(0) == 0)
    def _():
        r = lax.broadcasted_iota(jnp.int32, (tm, tm), 0)
        c = lax.broadcasted_iota(jnp.int32, (tm, tm), 1)
        ltri_ref[...] = jnp.where(c < r, 1.0, 0.0).astype(BF16)
        carry_ref[...] = jnp.zeros_like(carry_ref)
        nblk = jnp.floor((cnt_ref[...] + float(bm - 1)) * (1.0 / bm))
        ur = lax.broadcasted_iota(jnp.int32, (LANE, LANE), 0)
        uc = lax.broadcasted_iota(jnp.int32, (LANE, LANE), 1)
        upper = jnp.where(ur <= uc, 1.0, 0.0).astype(BF16)
        pend = jnp.dot(nblk.astype(BF16), upper, preferred_element_type=F32)
        pst_ref[...] = (pend - nblk) * float(bm)
        lane8 = lax.broadcasted_iota(jnp.int32, (8, LANE), 1)
        zst_ref[...] = jnp.maximum(pend * float(bm) - float(bm), 0.0).astype(jnp.int32)
        pend0 = pend[0:1, :]
        n_used = jnp.sum(jnp.where(lane8[0:1, :] == n_experts - 1, pend0, 0.0), axis=-1, keepdims=True)
        blk = lax.broadcasted_iota(jnp.int32, (nbp, LANE), 0).astype(F32)
        lane = lax.broadcasted_iota(jnp.int32, (nbp, LANE), 1)
        passed = jnp.where((lane < n_experts) & (pend0 <= blk), 1.0, 0.0)
        blk_e = jnp.minimum(jnp.sum(passed, axis=-1, keepdims=True), float(n_experts - 1))
        mine = lane.astype(F32) == blk_e
        e_count = jnp.sum(jnp.where(mine, cnt_ref[0:1, :], 0.0), axis=-1, keepdims=True)
        e_first = jnp.sum(jnp.where(mine, pend0 - nblk[0:1, :], 0.0), axis=-1, keepdims=True)
        n_valid = jnp.clip(e_count - (blk[:, 0:1] - e_first) * float(bm), 0.0, float(bm))
        meta = jnp.where(lane == 0, blk_e, jnp.where(lane == 1, n_used, jnp.where(lane == 2, n_valid, 0.0)))
        meta_ref[...] = meta.astype(jnp.int32)

    lane = lax.broadcasted_iota(jnp.int32, (tm, LANE), 1)
    idx = idx_ref[...]
    hots = []
    sel = jnp.zeros((tm, LANE), F32)
    for r in range(TOP_K):
        col = jnp.sum(jnp.where(lane == r, idx, 0).astype(F32), axis=-1, keepdims=True)
        hot = jnp.where(lane.astype(F32) == col, 1.0, 0.0)
        hots.append(hot)
        sel = sel + hot
    rank = jnp.dot(ltri_ref[...], sel.astype(BF16), preferred_element_type=F32) + carry_ref[0:1, :]
    pos = pst_ref[0:1, :] + rank
    dest = jnp.zeros((tm, LANE), F32)
    for r in range(TOP_K):
        d = jnp.sum(hots[r] * pos, axis=-1, keepdims=True)
        dest = jnp.where(lane == r, d, dest)
    dest_ref[...] = dest.T[0:8, :].astype(jnp.int32)
    carry_ref[0:1, :] += jnp.sum(sel, axis=0, keepdims=True)


def _positions(idx, cnt, n_experts, n_blocks):
    t = idx.shape[0]
    tm = TM_POS
    nbp = -(-n_blocks // 8) * 8
    return pl.pallas_call(
        functools.partial(_positions_kernel, n_experts=n_experts, bm=BM_EXP),
        grid=(t // tm,),
        in_specs=[pl.BlockSpec((tm, LANE), lambda i: (i, 0)),
                  pl.BlockSpec((8, LANE), lambda i: (0, 0))],
        out_specs=(pl.BlockSpec((8, tm), lambda i: (0, i)),
                   pl.BlockSpec((nbp, LANE), lambda i: (0, 0)),
                   pl.BlockSpec((8, LANE), lambda i: (0, 0))),
        out_shape=(jax.ShapeDtypeStruct((8, t), jnp.int32),
                   jax.ShapeDtypeStruct((nbp, LANE), jnp.int32),
                   jax.ShapeDtypeStruct((8, LANE), jnp.int32)),
        scratch_shapes=[pltpu.VMEM((8, LANE), F32), pltpu.VMEM((tm, tm), BF16), pltpu.VMEM((8, LANE), F32)],
        compiler_params=_params(("arbitrary",)),
        name="dispatch_positions",
    )(idx, cnt)


def _dispatch_kernel(zst_ref, dest_ref, h2_ref, xs_ref, zero_ref, zsem, sem, *, n_experts, bm):
    i = pl.program_id(0)
    tm = dest_ref.shape[1]

    def zero_copy(e):
        return pltpu.make_async_copy(zero_ref, xs_ref.at[pl.ds(zst_ref[e], bm)], zsem)

    def tail_copy(j):
        return pltpu.make_async_copy(zero_ref, xs_ref.at[pl.ds(j * bm, bm)], zsem)

    @pl.when(i == 0)
    def _():
        zero_ref[...] = jnp.zeros_like(zero_ref)
        n_used = zst_ref[n_experts]
        n_blocks = xs_ref.shape[0] // bm
        for e in range(n_experts):
            zero_copy(e).start()
        lax.fori_loop(n_used, n_blocks, lambda j, c: (tail_copy(j).start(), c)[1], 0)
        for e in range(n_experts):
            zero_copy(e).wait()
        lax.fori_loop(n_used, n_blocks, lambda j, c: (tail_copy(j).wait(), c)[1], 0)

    def body(j, carry):
        for r in range(TOP_K):
            d = dest_ref[r, j]
            pltpu.make_async_copy(h2_ref.at[j], xs_ref.at[d], sem).start(priority=r % 2)
        return carry

    lax.fori_loop(0, tm, body, 0, unroll=8)
    for _ in range(TOP_K):
        pltpu.make_async_copy(h2_ref, xs_ref.at[pl.ds(0, tm)], sem).wait()


def _dispatch(zst, dest, h2, n_experts, n_blocks):
    t, _, d = h2.shape
    grid_spec = pltpu.PrefetchScalarGridSpec(
        num_scalar_prefetch=1,
        grid=(t // TM_DISPATCH,),
        in_specs=[pl.BlockSpec((8, TM_DISPATCH), lambda i, z: (0, i), memory_space=pltpu.SMEM),
                  pl.BlockSpec((TM_DISPATCH, 1, d), lambda i, z: (i, 0, 0))],
        out_specs=pl.BlockSpec(memory_space=pl.ANY),
        scratch_shapes=[pltpu.VMEM((BM_EXP, 1, d), h2.dtype),
                        pltpu.SemaphoreType.DMA(()), pltpu.SemaphoreType.DMA(())],
    )
    return pl.pallas_call(
        functools.partial(_dispatch_kernel, n_experts=n_experts, bm=BM_EXP),
        grid_spec=grid_spec,
        out_shape=jax.ShapeDtypeStruct((n_blocks * BM_EXP, 1, d), h2.dtype),
        compiler_params=_params(("arbitrary",)),
        name="row_dispatch",
    )(zst, dest, h2)


def _expert_kernel(sp_ref, xs_ref, wg_ref, wl_ref, bg_ref, bl_ref, wd_ref, bd_ref, ys_ref,
                   x2d_ref, xb_ref, acc2_ref, in_sems, sems, *, n_blocks):
    j = pl.program_id(0)
    f = pl.program_id(1)
    nf = pl.num_programs(1)
    n_used = sp_ref[n_blocks]
    used = j < n_used
    bm = acc2_ref.shape[1]
    slot = j % 2
    acc_ref = acc2_ref.at[slot]

    def in_copy(blk, s):
        src = jnp.minimum(blk, n_used - 1)
        return pltpu.make_async_copy(xs_ref.at[pl.ds(src * bm, bm), 0, :], x2d_ref.at[s], in_sems.at[s])

    def out_copy(blk, s):
        return pltpu.make_async_copy(acc2_ref.at[s], ys_ref.at[pl.ds(blk * bm, bm), 0, :], sems.at[s])

    @pl.when((f == 0) & (j == 0))
    def _():
        in_copy(0, 0).start()

    @pl.when((f == 0) & (j + 1 < n_blocks))
    def _():
        in_copy(j + 1, 1 - slot).start()

    @pl.when(f == 0)
    def _():
        in_copy(j, slot).wait()

    @pl.when((f == 0) & (j >= 2))
    def _():
        out_copy(j - 2, slot).wait()

    @pl.when(used & (f == 0))
    def _():
        words = x2d_ref[slot]
        half = words.shape[1]
        xb_ref[:, :half] = pltpu.bitcast(words << 16, F32).astype(BF16)
        xb_ref[:, half:] = pltpu.bitcast(words & jnp.uint32(0xFFFF0000), F32).astype(BF16)

    @pl.when(jnp.logical_not(used) & (f == 0))
    def _():
        acc_ref[...] = jnp.zeros((bm, acc2_ref.shape[2]), F32)

    def mlp(rows):
        xb = xb_ref[0:rows]
        glu = jnp.dot(xb, wg_ref[0], preferred_element_type=F32) + bg_ref[0]
        lin = jnp.dot(xb, wl_ref[0], preferred_element_type=F32) + bl_ref[0]
        glu = jnp.minimum(glu, SWIGLU_LIMIT)
        lin = jnp.clip(lin, -SWIGLU_LIMIT, SWIGLU_LIMIT)
        act = glu * jax.nn.sigmoid(SWIGLU_ALPHA * glu) * (lin + 1.0)
        prev = jnp.where(f == 0, jnp.broadcast_to(bd_ref[0], (rows, acc2_ref.shape[2])), acc_ref[0:rows])
        acc_ref[0:rows] = prev + jnp.dot(act.astype(BF16), wd_ref[0], preferred_element_type=F32)

    sparse = sp_ref[n_blocks + 1 + jnp.minimum(j, n_blocks - 1)] <= bm // 2

    @pl.when(used & jnp.logical_not(sparse))
    def _():
        mlp(bm)

    @pl.when(used & sparse)
    def _():
        mlp(bm // 2)
        acc_ref[bm // 2:bm] = jnp.zeros((bm - bm // 2, acc2_ref.shape[2]), F32)

    @pl.when(f == nf - 1)
    def _():
        out_copy(j, slot).start()

    @pl.when((f == nf - 1) & (j == n_blocks - 1))
    def _():
        out_copy(j - 1, 1 - slot).wait()
        out_copy(j, slot).wait()


def _experts(sp, xs, w_gu, b_gu, w_dn, b_dn, n_blocks):
    p, _, dh = xs.shape
    n_experts, d, ff2 = w_gu.shape
    assert d == 2 * dh
    ff = ff2 // 2
    nf = ff // TF_EXP
    bm = BM_EXP

    def blk(j, s):
        return jnp.minimum(j, s[n_blocks] - 1)

    def fch(j, f, s):
        return jnp.where(j < s[n_blocks], f, nf - 1)

    grid_spec = pltpu.PrefetchScalarGridSpec(
        num_scalar_prefetch=1,
        grid=(n_blocks, nf),
        in_specs=[
            pl.BlockSpec(memory_space=pl.ANY),
            pl.BlockSpec((1, d, TF_EXP), lambda j, f, s: (s[blk(j, s)], 0, fch(j, f, s))),
            pl.BlockSpec((1, d, TF_EXP), lambda j, f, s: (s[blk(j, s)], 0, nf + fch(j, f, s))),
            pl.BlockSpec((1, 1, TF_EXP), lambda j, f, s: (s[blk(j, s)], 0, fch(j, f, s))),
            pl.BlockSpec((1, 1, TF_EXP), lambda j, f, s: (s[blk(j, s)], 0, nf + fch(j, f, s))),
            pl.BlockSpec((1, TF_EXP, d), lambda j, f, s: (s[blk(j, s)], fch(j, f, s), 0)),
            pl.BlockSpec((1, 1, d), lambda j, f, s: (s[blk(j, s)], 0, 0)),
        ],
        out_specs=pl.BlockSpec(memory_space=pl.ANY),
        scratch_shapes=[pltpu.VMEM((2, bm, dh), jnp.uint32), pltpu.VMEM((bm, d), BF16),
                        pltpu.VMEM((2, bm, d), F32),
                        pltpu.SemaphoreType.DMA((2,)), pltpu.SemaphoreType.DMA((2,))],
    )
    return pl.pallas_call(
        functools.partial(_expert_kernel, n_blocks=n_blocks),
        grid_spec=grid_spec,
        out_shape=jax.ShapeDtypeStruct((p, 1, d), F32),
        compiler_params=_params(("arbitrary", "arbitrary")),
        name="expert_mlp",
    )(sp, xs, w_gu, w_gu, b_gu.reshape(n_experts, 1, ff2), b_gu.reshape(n_experts, 1, ff2),
      w_dn, b_dn.reshape(n_experts, 1, d))


def _combine_kernel(dest_ref, dest_next_ref, gate_ref, x1_ref, mod_ref, g_ref, ys_ref, oa_ref, ob_ref,
                    buf_ref, row_ref, sems, *, first_tiles):
    tm = x1_ref.shape[0]
    i = pl.program_id(0)
    slot = i % 2

    def gather(table_ref, s):
        def body(j, carry):
            for r in range(TOP_K):
                d = table_ref[r, j]
                pltpu.make_async_copy(ys_ref.at[d], buf_ref.at[s, r, j], sems.at[s]).start(priority=r % 2)
            return carry

        lax.fori_loop(0, tm, body, 0, unroll=8)

    @pl.when(i == 0)
    def _():
        gather(dest_ref, 0)

    @pl.when(i + 1 < pl.num_programs(0))
    def _():
        gather(dest_next_ref, 1 - slot)

    for r in range(TOP_K):
        pltpu.make_async_copy(ys_ref.at[pl.ds(0, tm)], buf_ref.at[slot, r], sems.at[slot]).wait()

    gates = gate_ref[...]
    lane = lax.broadcasted_iota(jnp.int32, gates.shape, 1)
    f = None
    for r in range(TOP_K):
        g = jnp.sum(jnp.where(lane == r, gates, 0.0), axis=-1, keepdims=True)
        row_ref[...] = buf_ref[slot, r].reshape(row_ref.shape)
        term = row_ref[...] * g
        f = term if f is None else f + term
    out = x1_ref[...] + mod_ref[0, 5:6, :] * _rms(f, g_ref[...])

    @pl.when(pl.program_id(0) < first_tiles)
    def _():
        oa_ref[...] = out

    @pl.when(pl.program_id(0) >= first_tiles)
    def _():
        ob_ref[...] = out


def _combine(dest, gates, x1, mod3, g_post, ys, seq, n_first):
    t, d = x1.shape
    tm = TM_DMA
    tps = seq // tm
    first_tiles = n_first * tps
    n_tiles = t // tm
    return pl.pallas_call(
        functools.partial(_combine_kernel, first_tiles=first_tiles),
        grid=(n_tiles,),
        in_specs=[
            pl.BlockSpec((8, tm), lambda i: (0, i), memory_space=pltpu.SMEM),
            pl.BlockSpec((8, tm), lambda i: (0, jnp.minimum(i + 1, n_tiles - 1)), memory_space=pltpu.SMEM),
            pl.BlockSpec((tm, LANE), lambda i: (i, 0)),
            pl.BlockSpec((tm, d), lambda i: (i, 0)),
            pl.BlockSpec((1, N_MOD, d), lambda i: (i // tps, 0, 0)),
            pl.BlockSpec((1, d), lambda i: (0, 0)),
            pl.BlockSpec(memory_space=pl.ANY),
        ],
        out_specs=(pl.BlockSpec((tm, d), lambda i: (jnp.minimum(i, first_tiles - 1), 0)),
                   pl.BlockSpec((tm, d), lambda i: (jnp.maximum(i - first_tiles, 0), 0))),
        out_shape=(jax.ShapeDtypeStruct((first_tiles * tm, d), F32),
                   jax.ShapeDtypeStruct((t - first_tiles * tm, d), F32)),
        scratch_shapes=[pltpu.VMEM((2, TOP_K, tm, 1, d), F32), pltpu.VMEM((tm, d), F32),
                        pltpu.SemaphoreType.DMA((2,))],
        compiler_params=_params(("arbitrary",)),
        name="expert_combine",
    )(dest, dest, gates, x1, mod3, g_post.reshape(1, d), ys)


def _layout_w_in(w_in, qr, kvr):
    qsw = SWA_Q_HEADS * SWA_HEAD_DIM
    ksw = SWA_KV_HEADS * SWA_HEAD_DIM
    d = w_in.shape[0]
    sizes = (qr, kvr, MLA_ROPE, qsw, ksw, ksw, d, d)
    offs = [0]
    for s in sizes:
        offs.append(offs[-1] + s)
    c_q, c_kv, k_rope, q_s, k_s, v_s, g_a, g_b = [w_in[:, offs[i]:offs[i + 1]] for i in range(8)]
    parts = (("q_s", q_s), ("g_a", g_a), ("g_b", g_b), ("c_q", c_q), ("c_kv", c_kv),
             ("k_s", k_s), ("v_s", v_s), ("k_rope", jnp.concatenate([k_rope, k_rope], axis=1)))
    cols, off = {}, 0
    for name, p in parts:
        cols[name] = off
        off += p.shape[1]
    return jnp.concatenate([p for _, p in parts], axis=1).astype(BF16), cols


def _layout_w_uq(w_uq):
    r = w_uq.shape[0]
    w = w_uq.reshape(r, MLA_HEADS, MLA_NOPE + MLA_ROPE)
    nope = w[:, :, :MLA_NOPE].reshape(r, MLA_HEADS * MLA_NOPE)
    pe = w[:, :, MLA_NOPE:].reshape(r, MLA_HEADS * MLA_ROPE)
    return jnp.concatenate([nope, pe], axis=1).astype(BF16)


def _layout_w_ukv(w_ukv):
    r = w_ukv.shape[0]
    w = w_ukv.reshape(r, MLA_HEADS, MLA_NOPE + MLA_V)
    kn = w[:, :, :MLA_NOPE].reshape(r, MLA_HEADS * MLA_NOPE)
    v = w[:, :, MLA_NOPE:].reshape(r, MLA_HEADS * MLA_V)
    return jnp.concatenate([kn, v], axis=1).astype(BF16)


def _rope_tables(seq):
    half = MLA_ROPE // 2
    freqs = jnp.power(ROPE_THETA, -2.0 * jnp.arange(half, dtype=F32) / MLA_ROPE)
    ang = jnp.arange(seq, dtype=F32)[:, None] * freqs[None, :]
    cos, sin = jnp.cos(ang), jnp.sin(ang)
    return (jnp.concatenate([cos, cos, cos, cos], axis=1),
            jnp.concatenate([-sin, sin, -sin, sin], axis=1))


def kernel(x_prompt, x_sample, c_prompt, c_sample, w_ada, b_ada, g_pre_mix, w_in, g_q_lat, w_uq,
           g_kv_lat, w_ukv, attn_sinks, w_br_mla, w_br_swa, w_out, g_post_mix, g_pre_ffn,
           w_router, b_router, w_gu, b_gu, w_dn, b_dn, g_post_ffn):
    assert MLA_ROPE == SWA_HEAD_DIM == 64 and MLA_NOPE == MLA_V == LANE
    nb_p, seq, d = x_prompt.shape
    nb_s = x_sample.shape[0]
    assert x_sample.shape[1] == seq
    nseq = nb_p + nb_s
    t = nseq * seq
    n_experts = w_router.shape[-1]
    n_blocks = t * TOP_K // BM_EXP + n_experts

    xa, xb = x_prompt.reshape(nb_p * seq, d), x_sample.reshape(nb_s * seq, d)
    c8 = jnp.concatenate([c_prompt, c_sample, jnp.zeros((8 - nseq, d), F32)], axis=0)
    cos, sin = _rope_tables(seq)

    for l in range(w_ada.shape[0]):
        w_in_p, cols = _layout_w_in(w_in[l], g_q_lat.shape[-1], g_kv_lat.shape[-1])
        w_uq_p = _layout_w_uq(w_uq[l])
        w_ukv_p = _layout_w_ukv(w_ukv[l])
        w_router_p = jnp.pad(w_router[l], ((0, 0), (0, LANE - n_experts))).astype(BF16)
        b_router_p = jnp.pad(b_router[l], (0, LANE - n_experts)).reshape(1, LANE)

        mod = _modulation(c8, w_ada[l], b_ada[l])
        mod3 = mod[:nseq].reshape(nseq, N_MOD, d)
        z = _in_projection(xa, xb, mod3, g_pre_mix[l], w_in_p, seq)
        q, k, v, qs, klo, khi, vlo, vhi = _post_projection(
            z, cols, cos, sin, g_q_lat[l], g_kv_lat[l], w_uq_p, w_ukv_p, nseq, seq)
        o_a, w_gu_b, w_dn_b = _mla_attention(q, k, v, w_gu[l], w_dn[l])
        o_b = _swa_attention(attn_sinks[l], qs, klo, khi, vlo, vhi, nseq, seq)
        y = _merge(o_a, o_b, z, cols, w_br_mla[l].astype(BF16), w_br_swa[l].astype(BF16))
        x1, h2, idx, gates, cnt = _out_projection(
            y, xa, xb, mod3, g_post_mix[l], g_pre_ffn[l], w_out[l].astype(BF16), w_router_p, b_router_p,
            n_experts, seq)
        dest, meta, zst = _positions(idx, cnt, n_experts, n_blocks)
        sp = jnp.concatenate([meta[:n_blocks, 0], meta[0:1, 1], meta[:n_blocks, 2]])
        zmeta = jnp.concatenate([zst[0, :n_experts], meta[0:1, 1]])
        xs = _dispatch(zmeta, dest, h2, n_experts, n_blocks)
        ys = _experts(sp, xs, w_gu_b, b_gu[l], w_dn_b, b_dn[l], n_blocks)
        xa, xb = _combine(dest, gates, x1, mod3, g_post_ffn[l], ys, seq, nb_p)

    return (xa.reshape(nb_p, seq, d), xb.reshape(nb_s, seq, d))
```

```python
import functools

import jax
import jax.numpy as jnp
from jax import lax
from jax.experimental import pallas as pl
from jax.experimental.pallas import tpu as pltpu

MLA_HEADS = 16
MLA_NOPE = 128
MLA_ROPE = 64
MLA_V = 128
SWA_Q_HEADS = 32
SWA_KV_HEADS = 8
SWA_HEAD_DIM = 64
WINDOW = 128
TOP_K = 4
SWIGLU_LIMIT = 7.0
SWIGLU_ALPHA = 1.702
ROPE_THETA = 10000.0
RMS_EPS = 1e-6
N_MOD = 6

LANE = 128
VMEM_LIMIT = 56 << 20

TM_IN = 512
TN_IN = 1664
TM_POST = 256
TQ_MLA = 1024
KV_CHUNKS_MLA = (1024, 1024, 1024, 1024)
TQ_SWA = 512
SUB_SWA = 256
TM_MERGE = 1024
TN_MERGE = 1024
TM_OUT = 512
TM_POS = 512
TM_DISPATCH = 512
TM_DMA = 256
BM_EXP = 512
TF_EXP = 1024

F32 = jnp.float32
BF16 = jnp.bfloat16
NEG_INF = float("-inf")
LOG2E = 1.4426950408889634


def _params(sem):
    return pltpu.CompilerParams(dimension_semantics=sem, vmem_limit_bytes=VMEM_LIMIT)


def _rms(x, g):
    return x * lax.rsqrt(jnp.mean(x * x, axis=-1, keepdims=True) + RMS_EPS) * g


def _mod_kernel(c_ref, w_ref, b_ref, o_ref):
    c = c_ref[...]
    a = (c * jax.nn.sigmoid(c)).astype(BF16)
    o_ref[...] = jnp.dot(a, w_ref[...].astype(BF16), preferred_element_type=F32) + b_ref[...]


def _modulation(c8, w_ada, b_ada):
    d, n = w_ada.shape
    tn = 1024
    return pl.pallas_call(
        _mod_kernel,
        grid=(n // tn,),
        in_specs=[
            pl.BlockSpec((8, d), lambda j: (0, 0)),
            pl.BlockSpec((d, tn), lambda j: (0, j)),
            pl.BlockSpec((1, tn), lambda j: (0, j)),
        ],
        out_specs=pl.BlockSpec((8, tn), lambda j: (0, j)),
        out_shape=jax.ShapeDtypeStruct((8, n), F32),
        compiler_params=_params(("arbitrary",)),
        name="adaln_mod",
    )(c8, w_ada, b_ada.reshape(1, n))


def _two_group_specs(tm, d, first_tiles, n_grid_axes):
    if n_grid_axes == 1:
        return (pl.BlockSpec((tm, d), lambda i: (jnp.minimum(i, first_tiles - 1), 0)),
                pl.BlockSpec((tm, d), lambda i: (jnp.maximum(i - first_tiles, 0), 0)))
    return (pl.BlockSpec((tm, d), lambda i, j: (jnp.minimum(i, first_tiles - 1), 0)),
            pl.BlockSpec((tm, d), lambda i, j: (jnp.maximum(i - first_tiles, 0), 0)))


def _inproj_kernel(xa_ref, xb_ref, mod_ref, g_ref, w_ref, z_ref, h_ref, *, first_tiles):
    def normalise(x_ref):
        x = x_ref[...]
        gain = g_ref[...] * (1.0 + mod_ref[0, 1:2, :])
        h = x * lax.rsqrt(jnp.mean(x * x, axis=-1, keepdims=True) + RMS_EPS) * gain + mod_ref[0, 0:1, :]
        h_ref[...] = h.astype(BF16)

    first_col = pl.program_id(1) == 0
    in_first = pl.program_id(0) < first_tiles
    pl.when(first_col & in_first)(functools.partial(normalise, xa_ref))
    pl.when(first_col & jnp.logical_not(in_first))(functools.partial(normalise, xb_ref))

    z_ref[...] = jnp.dot(h_ref[...], w_ref[...], preferred_element_type=F32).astype(z_ref.dtype)


def _in_projection(xa, xb, mod3, g, w_in_p, seq):
    d = xa.shape[1]
    t = xa.shape[0] + xb.shape[0]
    n = w_in_p.shape[1]
    tiles_per_seq = seq // TM_IN
    first_tiles = xa.shape[0] // TM_IN
    return pl.pallas_call(
        functools.partial(_inproj_kernel, first_tiles=first_tiles),
        grid=(t // TM_IN, n // TN_IN),
        in_specs=[
            *_two_group_specs(TM_IN, d, first_tiles, 2),
            pl.BlockSpec((1, N_MOD, d), lambda i, j: (i // tiles_per_seq, 0, 0)),
            pl.BlockSpec((1, d), lambda i, j: (0, 0)),
            pl.BlockSpec((d, TN_IN), lambda i, j: (0, j)),
        ],
        out_specs=pl.BlockSpec((TM_IN, TN_IN), lambda i, j: (i, j)),
        out_shape=jax.ShapeDtypeStruct((t, n), BF16),
        scratch_shapes=[pltpu.VMEM((TM_IN, d), BF16)],
        compiler_params=_params(("arbitrary", "arbitrary")),
        name="in_projection",
    )(xa, xb, mod3, g.reshape(1, d), w_in_p)


def _postproj_kernel(cq_ref, ckv_ref, qs_ref, ks_ref, vs_ref, kr_ref, cos_ref, sin_ref,
                     gq_ref, gkv_ref, wuq_ref, wukv_ref,
                     q_ref, k_ref, v_ref, qso_ref, klo_ref, khi_ref, vlo_ref, vhi_ref):
    tm = cq_ref.shape[0]
    cos = cos_ref[...]
    sin = sin_ref[...]
    lane = lax.broadcasted_iota(jnp.int32, (tm, LANE), 1)
    first_half = (lane & 63) < 32
    low = lane < 64

    def rope(x, cos=cos, sin=sin):
        rot = jnp.where(first_half, pltpu.roll(x, LANE - 32, 1), pltpu.roll(x, 32, 1))
        return x * cos + rot * sin

    nh = MLA_HEADS
    scale = float((MLA_NOPE + MLA_ROPE) ** -0.5) * LOG2E
    cqn = (_rms(cq_ref[...].astype(F32), gq_ref[...]) * scale).astype(BF16)
    q = jnp.dot(cqn, wuq_ref[...], preferred_element_type=F32)
    for h in range(nh):
        q_ref[0, h, :, 0:LANE] = q[:, h * LANE:(h + 1) * LANE].astype(BF16)
    for m in range(nh // 2):
        pe = rope(q[:, (nh + m) * LANE:(nh + m + 1) * LANE]).astype(BF16)
        q_ref[0, 2 * m, :, LANE:2 * LANE] = pe
        q_ref[0, 2 * m + 1, :, LANE:2 * LANE] = pe

    ckvn = _rms(ckv_ref[...].astype(F32), gkv_ref[...]).astype(BF16)
    kv = jnp.dot(ckvn, wukv_ref[...], preferred_element_type=F32)
    kr = rope(kr_ref[...].astype(F32))
    kpe_lo = jnp.where(low, kr, 0.0).astype(BF16)
    kpe_hi = jnp.where(low, 0.0, kr).astype(BF16)
    for h in range(nh):
        k_ref[0, h, :, 0:LANE] = kv[:, h * LANE:(h + 1) * LANE].astype(BF16)
        k_ref[0, h, :, LANE:2 * LANE] = kpe_lo if h % 2 == 0 else kpe_hi
        v_ref[0, h, :, :] = kv[:, (nh + h) * LANE:(nh + h + 1) * LANE].astype(BF16)

    swa_scale = float(SWA_HEAD_DIM ** -0.5) * LOG2E
    cos_q, sin_q = cos * swa_scale, sin * swa_scale
    for m in range(SWA_Q_HEADS // 2):
        x = qs_ref[:, m * LANE:(m + 1) * LANE].astype(F32)
        qso_ref[:, m * LANE:(m + 1) * LANE] = rope(x, cos_q, sin_q).astype(BF16)

    for m in range(SWA_KV_HEADS // 2):
        sl = slice(m * LANE, (m + 1) * LANE)
        for src_ref, lo_ref, hi_ref, roped in ((ks_ref, klo_ref, khi_ref, True),
                                               (vs_ref, vlo_ref, vhi_ref, False)):
            a = src_ref[:, sl].astype(F32)
            if roped:
                a = rope(a)
            b = pltpu.roll(a, 64, 1)
            e0 = slice((2 * m) * LANE, (2 * m + 1) * LANE)
            e1 = slice((2 * m + 1) * LANE, (2 * m + 2) * LANE)
            lo_ref[:, e0] = jnp.where(low, a, 0.0).astype(BF16)
            hi_ref[:, e0] = jnp.where(low, 0.0, b).astype(BF16)
            lo_ref[:, e1] = jnp.where(low, b, 0.0).astype(BF16)
            hi_ref[:, e1] = jnp.where(low, 0.0, a).astype(BF16)


def _post_projection(z, cols, cos, sin, g_q, g_kv, w_uq_p, w_ukv_p, nseq, seq):
    t = z.shape[0]
    tm = TM_POST
    tps = seq // tm
    nh = MLA_HEADS
    qr, kvr = g_q.shape[0], g_kv.shape[0]
    qsw = SWA_Q_HEADS * SWA_HEAD_DIM
    ksw = SWA_KV_HEADS * SWA_HEAD_DIM

    def zspec(width, off):
        blk = off // width
        return pl.BlockSpec((tm, width), lambda i: (i, blk))

    head_map = lambda i: (i // tps, 0, i % tps, 0)
    tok_map = lambda i: (i, 0)
    out_shapes = (
        jax.ShapeDtypeStruct((nseq, nh, seq, 2 * LANE), BF16),
        jax.ShapeDtypeStruct((nseq, nh, seq, 2 * LANE), BF16),
        jax.ShapeDtypeStruct((nseq, nh, seq, LANE), BF16),
        jax.ShapeDtypeStruct((t, qsw), BF16),
        jax.ShapeDtypeStruct((t, SWA_KV_HEADS * LANE), BF16),
        jax.ShapeDtypeStruct((t, SWA_KV_HEADS * LANE), BF16),
        jax.ShapeDtypeStruct((t, SWA_KV_HEADS * LANE), BF16),
        jax.ShapeDtypeStruct((t, SWA_KV_HEADS * LANE), BF16),
    )
    out_specs = (
        pl.BlockSpec((1, nh, tm, 2 * LANE), head_map),
        pl.BlockSpec((1, nh, tm, 2 * LANE), head_map),
        pl.BlockSpec((1, nh, tm, LANE), head_map),
        pl.BlockSpec((tm, qsw), tok_map),
        pl.BlockSpec((tm, SWA_KV_HEADS * LANE), tok_map),
        pl.BlockSpec((tm, SWA_KV_HEADS * LANE), tok_map),
        pl.BlockSpec((tm, SWA_KV_HEADS * LANE), tok_map),
        pl.BlockSpec((tm, SWA_KV_HEADS * LANE), tok_map),
    )
    return pl.pallas_call(
        _postproj_kernel,
        grid=(t // tm,),
        in_specs=[
            zspec(qr, cols["c_q"]), zspec(kvr, cols["c_kv"]), zspec(qsw, cols["q_s"]),
            zspec(ksw, cols["k_s"]), zspec(ksw, cols["v_s"]), zspec(LANE, cols["k_rope"]),
            pl.BlockSpec((tm, LANE), lambda i: (i % tps, 0)),
            pl.BlockSpec((tm, LANE), lambda i: (i % tps, 0)),
            pl.BlockSpec((1, qr), lambda i: (0, 0)),
            pl.BlockSpec((1, kvr), lambda i: (0, 0)),
            pl.BlockSpec(w_uq_p.shape, lambda i: (0, 0)),
            pl.BlockSpec(w_ukv_p.shape, lambda i: (0, 0)),
        ],
        out_specs=out_specs,
        out_shape=out_shapes,
        compiler_params=_params(("arbitrary",)),
        name="post_projection",
    )(z, z, z, z, z, z, cos, sin, g_q.reshape(1, qr), g_kv.reshape(1, kvr), w_uq_p, w_ukv_p)


def _mla_kernel(q_ref, k_ref, v_ref, wgu_ref, wdn_ref, o_ref, wgu_o_ref, wdn_o_ref, *, n_gu):
    step = (pl.program_id(0) * pl.num_programs(1) + pl.program_id(1)) * pl.num_programs(2) + pl.program_id(2)

    @pl.when(step < n_gu)
    def _():
        wgu_o_ref[...] = wgu_ref[...].astype(BF16)

    @pl.when(step >= n_gu)
    def _():
        wdn_o_ref[...] = wdn_ref[...].astype(BF16)

    q = q_ref[0, 0]
    seq = k_ref.shape[2]
    m = l = acc = None
    assert sum(KV_CHUNKS_MLA) == seq
    start = 0
    for width in KV_CHUNKS_MLA:
        rows = slice(start, start + width)
        start += width
        s = lax.dot_general(q, k_ref[0, 0, rows, :], (((1,), (1,)), ((), ())),
                            preferred_element_type=F32)
        mc = jnp.max(s, axis=-1, keepdims=True)
        m_new = mc if m is None else jnp.maximum(m, mc)
        p = jnp.exp2(s - m_new)
        ps = jnp.sum(p, axis=-1, keepdims=True)
        pv = jnp.dot(p.astype(BF16), v_ref[0, 0, rows, :], preferred_element_type=F32)
        if m is None:
            l, acc = ps, pv
        else:
            alpha = jnp.exp2(m - m_new)
            l = alpha * l + ps
            acc = alpha * acc + pv
        m = m_new
    o_ref[...] = (acc / l).astype(o_ref.dtype)


def _mla_attention(q, k, v, w_gu, w_dn):
    nseq, nh, seq, dqk = q.shape
    dv = v.shape[-1]
    nq = seq // TQ_MLA
    n_steps = nseq * nh * nq
    gu2d = w_gu.reshape(-1, w_gu.shape[-1])
    dn2d = w_dn.reshape(-1, w_dn.shape[-1])
    n_gu, n_dn = 2 * n_steps // 3, n_steps // 3
    assert n_gu + n_dn == n_steps and gu2d.shape[0] % n_gu == 0 and dn2d.shape[0] % n_dn == 0
    gu_rows, dn_rows = gu2d.shape[0] // n_gu, dn2d.shape[0] // n_dn
    assert gu_rows % 16 == 0 and dn_rows % 16 == 0

    def step(b, h, i):
        return (b * nh + h) * nq + i

    gu_map = lambda b, h, i: (jnp.minimum(step(b, h, i), n_gu - 1), 0)
    dn_map = lambda b, h, i: (jnp.maximum(step(b, h, i) - n_gu, 0), 0)
    o, gu_b, dn_b = pl.pallas_call(
        functools.partial(_mla_kernel, n_gu=n_gu),
        grid=(nseq, nh, nq),
        in_specs=[
            pl.BlockSpec((1, 1, TQ_MLA, dqk), lambda b, h, i: (b, h, i, 0)),
            pl.BlockSpec((1, 1, seq, dqk), lambda b, h, i: (b, h, 0, 0)),
            pl.BlockSpec((1, 1, seq, dv), lambda b, h, i: (b, h, 0, 0)),
            pl.BlockSpec((gu_rows, gu2d.shape[1]), gu_map),
            pl.BlockSpec((dn_rows, dn2d.shape[1]), dn_map),
        ],
        out_specs=(
            pl.BlockSpec((TQ_MLA, dv), lambda b, h, i: (b * nq + i, h)),
            pl.BlockSpec((gu_rows, gu2d.shape[1]), gu_map),
            pl.BlockSpec((dn_rows, dn2d.shape[1]), dn_map),
        ),
        out_shape=(
            jax.ShapeDtypeStruct((nseq * seq, nh * dv), BF16),
            jax.ShapeDtypeStruct(gu2d.shape, BF16),
            jax.ShapeDtypeStruct(dn2d.shape, BF16),
        ),
        compiler_params=_params(("arbitrary", "arbitrary", "arbitrary")),
        name="mla_attention",
    )(q, k, v, gu2d, dn2d)
    return o, gu_b.reshape(w_gu.shape), dn_b.reshape(w_dn.shape)


def _swa_kernel(sink_ref, q_ref,
                klo_p, klo_c, klo_n, khi_p, khi_c, khi_n,
                vlo_p, vlo_c, vlo_n, vhi_p, vhi_c, vhi_n,
                o_ref, klo_w, khi_w, vlo_w, vhi_w, *, seq):
    tb = q_ref.shape[0]
    tq = SUB_SWA
    w = WINDOW
    for win, (p, c, n) in ((klo_w, (klo_p, klo_c, klo_n)), (khi_w, (khi_p, khi_c, khi_n)),
                           (vlo_w, (vlo_p, vlo_c, vlo_n)), (vhi_w, (vhi_p, vhi_c, vhi_n))):
        win[0:w, :] = p[...]
        win[w:w + tb, :] = c[...]
        win[w + tb:tb + 2 * w, :] = n[...]

    assert SWA_Q_HEADS // SWA_KV_HEADS == 4 and tq & (tq - 1) == 0 and tb % tq == 0
    nk = tq + 2 * w
    row2 = lax.broadcasted_iota(jnp.int32, (2 * tq, nk), 0)
    col = lax.broadcasted_iota(jnp.int32, (2 * tq, nk), 1)
    rel = col - w - (row2 & (tq - 1))
    band = (rel <= w) & (rel >= -w)
    top = lax.broadcasted_iota(jnp.int32, (2 * tq, 1), 0) < tq

    for sub in range(tb // tq):
        qrows = slice(sub * tq, (sub + 1) * tq)
        krows = slice(sub * tq, sub * tq + nk)
        kpos = pl.program_id(1) * tb + sub * tq - w + col
        valid = band & (kpos >= 0) & (kpos < seq)
        for g in range(SWA_KV_HEADS):
            gs = slice(g * LANE, (g + 1) * LANE)
            pa = slice((2 * g) * LANE, (2 * g + 1) * LANE)
            pb = slice((2 * g + 1) * LANE, (2 * g + 2) * LANE)
            qq = jnp.concatenate([q_ref[qrows, pa], q_ref[qrows, pb]], axis=0)
            acc = None
            for half, (kw, vw) in enumerate(((klo_w, vlo_w), (khi_w, vhi_w))):
                sink = jnp.where(top, sink_ref[4 * g + half], sink_ref[4 * g + 2 + half]) * LOG2E
                s = lax.dot_general(qq, kw[krows, gs], (((1,), (1,)), ((), ())),
                                    preferred_element_type=F32)
                s = jnp.where(valid, s, NEG_INF)
                mx = jnp.maximum(jnp.max(s, axis=-1, keepdims=True), sink)
                e = jnp.exp2(s - mx)
                den = jnp.sum(e, axis=-1, keepdims=True) + jnp.exp2(sink - mx)
                pv = jnp.dot(e.astype(BF16), vw[krows, gs], preferred_element_type=F32) / den
                acc = pv if acc is None else acc + pv
            o_ref[qrows, pa] = acc[0:tq].astype(o_ref.dtype)
            o_ref[qrows, pb] = acc[tq:2 * tq].astype(o_ref.dtype)


def _swa_attention(sinks, qs, klo, khi, vlo, vhi, nseq, seq):
    t, qw = qs.shape
    kw = klo.shape[1]
    tq = TQ_SWA
    nq = seq // tq
    r = tq // WINDOW
    nwb = seq // WINDOW

    prev = pl.BlockSpec((WINDOW, kw), lambda b, i, s: (b * nwb + jnp.maximum(i * r - 1, 0), 0))
    cur = pl.BlockSpec((tq, kw), lambda b, i, s: (b * nq + i, 0))
    nxt = pl.BlockSpec((WINDOW, kw), lambda b, i, s: (b * nwb + jnp.minimum((i + 1) * r, nwb - 1), 0))
    grid_spec = pltpu.PrefetchScalarGridSpec(
        num_scalar_prefetch=1,
        grid=(nseq, nq),
        in_specs=[pl.BlockSpec((tq, qw), lambda b, i, s: (b * nq + i, 0))] + [prev, cur, nxt] * 4,
        out_specs=pl.BlockSpec((tq, qw), lambda b, i, s: (b * nq + i, 0)),
        scratch_shapes=[pltpu.VMEM((tq + 2 * WINDOW, kw), BF16)] * 4,
    )
    return pl.pallas_call(
        functools.partial(_swa_kernel, seq=seq),
        grid_spec=grid_spec,
        out_shape=jax.ShapeDtypeStruct((t, qw), BF16),
        compiler_params=_params(("arbitrary", "arbitrary")),
        name="swa_attention",
    )(sinks, qs, klo, klo, klo, khi, khi, khi, vlo, vlo, vlo, vhi, vhi, vhi)


def _merge_kernel(oa_ref, ob_ref, ga_ref, gb_ref, wa_ref, wb_ref, y_ref):
    a = jnp.dot(oa_ref[...], wa_ref[...], preferred_element_type=F32)
    b = jnp.dot(ob_ref[...], wb_ref[...], preferred_element_type=F32)
    y = jax.nn.sigmoid(ga_ref[...].astype(F32)) * a + jax.nn.sigmoid(gb_ref[...].astype(F32)) * b
    y_ref[...] = y.astype(y_ref.dtype)


def _merge(o_a, o_b, z, cols, w_a, w_b):
    t, d = o_a.shape[0], w_a.shape[1]
    tm, tn = TM_MERGE, TN_MERGE
    ga_blk, gb_blk = cols["g_a"] // tn, cols["g_b"] // tn
    return pl.pallas_call(
        _merge_kernel,
        grid=(t // tm, d // tn),
        in_specs=[
            pl.BlockSpec((tm, o_a.shape[1]), lambda i, j: (i, 0)),
            pl.BlockSpec((tm, o_b.shape[1]), lambda i, j: (i, 0)),
            pl.BlockSpec((tm, tn), lambda i, j: (i, ga_blk + j)),
            pl.BlockSpec((tm, tn), lambda i, j: (i, gb_blk + j)),
            pl.BlockSpec((w_a.shape[0], tn), lambda i, j: (0, j)),
            pl.BlockSpec((w_b.shape[0], tn), lambda i, j: (0, j)),
        ],
        out_specs=pl.BlockSpec((tm, tn), lambda i, j: (i, j)),
        out_shape=jax.ShapeDtypeStruct((t, d), BF16),
        compiler_params=_params(("arbitrary", "arbitrary")),
        name="branch_merge",
    )(o_a, o_b, z, z, w_a, w_b)


def _outproj_kernel(y_ref, xa_ref, xb_ref, mod_ref, gpm_ref, gpf_ref, wo_ref, wr_ref, br_ref,
                    x1_ref, h2_ref, idx_ref, gate_ref, cnt_ref, pack_ref, *, n_experts, first_tiles):
    tm = y_ref.shape[0]
    u = jnp.dot(y_ref[...], wo_ref[...], preferred_element_type=F32)
    x = jnp.where(pl.program_id(0) < first_tiles, xa_ref[...], xb_ref[...])
    x1 = x + mod_ref[0, 2:3, :] * _rms(u, gpm_ref[...])
    x1_ref[...] = x1
    h2 = _rms(x1, gpf_ref[...]) * (1.0 + mod_ref[0, 4:5, :]) + mod_ref[0, 3:4, :]
    h2b = h2.astype(BF16)
    bits = pltpu.bitcast(h2b.astype(F32), jnp.uint32)
    half = bits.shape[1] // 2
    pack_ref[...] = (bits[:, :half] >> 16) | (bits[:, half:] & jnp.uint32(0xFFFF0000))
    h2_ref[...] = pack_ref[...].reshape(h2_ref.shape)

    logits = jnp.dot(h2b, wr_ref[...], preferred_element_type=F32) + br_ref[...]
    lane = lax.broadcasted_iota(jnp.int32, (tm, LANE), 1)
    lane_f = lane.astype(F32)
    cur = jnp.where(lane < n_experts, logits, NEG_INF)
    vals, idxs = [], []
    for _ in range(TOP_K):
        m = jnp.max(cur, axis=-1, keepdims=True)
        ix = jnp.min(jnp.where(cur == m, lane_f, float(LANE)), axis=-1, keepdims=True)
        vals.append(m)
        idxs.append(ix)
        cur = jnp.where(lane_f == ix, NEG_INF, cur)
    es = [jnp.exp(v - vals[0]) for v in vals]
    den = es[0]
    for e in es[1:]:
        den = den + e
    idx_out = jnp.zeros((tm, LANE), F32)
    gate_out = jnp.zeros((tm, LANE), F32)
    sel = jnp.zeros((tm, LANE), F32)
    for r in range(TOP_K):
        idx_out = jnp.where(lane == r, idxs[r], idx_out)
        gate_out = jnp.where(lane == r, es[r] / den, gate_out)
        sel = sel + jnp.where(lane_f == idxs[r], 1.0, 0.0)
    idx_ref[...] = idx_out.astype(jnp.int32)
    gate_ref[...] = gate_out

    @pl.when(pl.program_id(0) == 0)
    def _():
        cnt_ref[...] = jnp.zeros_like(cnt_ref)

    cnt_ref[0:1, :] += jnp.sum(sel, axis=0, keepdims=True)


def _out_projection(y, xa, xb, mod3, g_pm, g_pf, w_out, w_router_p, b_router_p, n_experts, seq):
    t, d = y.shape
    tm = TM_OUT
    tps = seq // tm
    first_tiles = xa.shape[0] // tm
    tok = lambda i: (i, 0)
    const = lambda i: (0, 0)
    return pl.pallas_call(
        functools.partial(_outproj_kernel, n_experts=n_experts, first_tiles=first_tiles),
        grid=(t // tm,),
        in_specs=[
            pl.BlockSpec((tm, d), tok),
            *_two_group_specs(tm, d, first_tiles, 1),
            pl.BlockSpec((1, N_MOD, d), lambda i: (i // tps, 0, 0)),
            pl.BlockSpec((1, d), const),
            pl.BlockSpec((1, d), const),
            pl.BlockSpec((d, d), const),
            pl.BlockSpec((d, LANE), const),
            pl.BlockSpec((1, LANE), const),
        ],
        out_specs=(
            pl.BlockSpec((tm, d), tok),
            pl.BlockSpec((tm, 1, d // 2), lambda i: (i, 0, 0)),
            pl.BlockSpec((tm, LANE), tok),
            pl.BlockSpec((tm, LANE), tok),
            pl.BlockSpec((8, LANE), const),
        ),
        out_shape=(
            jax.ShapeDtypeStruct((t, d), F32),
            jax.ShapeDtypeStruct((t, 1, d // 2), jnp.uint32),
            jax.ShapeDtypeStruct((t, LANE), jnp.int32),
            jax.ShapeDtypeStruct((t, LANE), F32),
            jax.ShapeDtypeStruct((8, LANE), F32),
        ),
        scratch_shapes=[pltpu.VMEM((tm, d // 2), jnp.uint32)],
        compiler_params=_params(("arbitrary",)),
        name="out_projection_router",
    )(y, xa, xb, mod3, g_pm.reshape(1, d), g_pf.reshape(1, d), w_out, w_router_p, b_router_p)


def _positions_kernel(idx_ref, cnt_ref, dest_ref, meta_ref, zst_ref, carry_ref, ltri_ref, pst_ref,
                      *, n_experts, bm):
    tm = idx_ref.shape[0]
    nbp = meta_ref.shape[0]

    @pl.when(pl.program_id(0) == 0)
    def _():
        r = lax.broadcasted_iota(jnp.int32, (tm, tm), 0)
        c = lax.broadcasted_iota(jnp.int32, (tm, tm), 1)
        ltri_ref[...] = jnp.where(c < r, 1.0, 0.0).astype(BF16)
        carry_ref[...] = jnp.zeros_like(carry_ref)
        nblk = jnp.floor((cnt_ref[...] + float(bm - 1)) * (1.0 / bm))
        ur = lax.broadcasted_iota(jnp.int32, (LANE, LANE), 0)
        uc = lax.broadcasted_iota(jnp.int32, (LANE, LANE), 1)
        upper = jnp.where(ur <= uc, 1.0, 0.0).astype(BF16)
        pend = jnp.dot(nblk.astype(BF16), upper, preferred_element_type=F32)
        pst_ref[...] = (pend - nblk) * float(bm)
        lane8 = lax.broadcasted_iota(jnp.int32, (8, LANE), 1)
        zst_ref[...] = jnp.maximum(pend * float(bm) - float(bm), 0.0).astype(jnp.int32)
        pend0 = pend[0:1, :]
        n_used = jnp.sum(jnp.where(lane8[0:1, :] == n_experts - 1, pend0, 0.0), axis=-1, keepdims=True)
        blk = lax.broadcasted_iota(jnp.int32, (nbp, LANE), 0).astype(F32)
        lane = lax.broadcasted_iota(jnp.int32, (nbp, LANE), 1)
        passed = jnp.where((lane < n_experts) & (pend0 <= blk), 1.0, 0.0)
        blk_e = jnp.minimum(jnp.sum(passed, axis=-1, keepdims=True), float(n_experts - 1))
        mine = lane.astype(F32) == blk_e
        e_count = jnp.sum(jnp.where(mine, cnt_ref[0:1, :], 0.0), axis=-1, keepdims=True)
        e_first = jnp.sum(jnp.where(mine, pend0 - nblk[0:1, :], 0.0), axis=-1, keepdims=True)
        n_valid = jnp.clip(e_count - (blk[:, 0:1] - e_first) * float(bm), 0.0, float(bm))
        meta = jnp.where(lane == 0, blk_e, jnp.where(lane == 1, n_used, jnp.where(lane == 2, n_valid, 0.0)))
        meta_ref[...] = meta.astype(jnp.int32)

    lane = lax.broadcasted_iota(jnp.int32, (tm, LANE), 1)
    idx = idx_ref[...]
    hots = []
    sel = jnp.zeros((tm, LANE), F32)
    for r in range(TOP_K):
        col = jnp.sum(jnp.where(lane == r, idx, 0).astype(F32), axis=-1, keepdims=True)
        hot = jnp.where(lane.astype(F32) == col, 1.0, 0.0)
        hots.append(hot)
        sel = sel + hot
    rank = jnp.dot(ltri_ref[...], sel.astype(BF16), preferred_element_type=F32) + carry_ref[0:1, :]
    pos = pst_ref[0:1, :] + rank
    dest = jnp.zeros((tm, LANE), F32)
    for r in range(TOP_K):
        d = jnp.sum(hots[r] * pos, axis=-1, keepdims=True)
        dest = jnp.where(lane == r, d, dest)
    dest_ref[...] = dest.T[0:8, :].astype(jnp.int32)
    carry_ref[0:1, :] += jnp.sum(sel, axis=0, keepdims=True)


def _positions(idx, cnt, n_experts, n_blocks):
    t = idx.shape[0]
    tm = TM_POS
    nbp = -(-n_blocks // 8) * 8
    return pl.pallas_call(
        functools.partial(_positions_kernel, n_experts=n_experts, bm=BM_EXP),
        grid=(t // tm,),
        in_specs=[pl.BlockSpec((tm, LANE), lambda i: (i, 0)),
                  pl.BlockSpec((8, LANE), lambda i: (0, 0))],
        out_specs=(pl.BlockSpec((8, tm), lambda i: (0, i)),
                   pl.BlockSpec((nbp, LANE), lambda i: (0, 0)),
                   pl.BlockSpec((8, LANE), lambda i: (0, 0))),
        out_shape=(jax.ShapeDtypeStruct((8, t), jnp.int32),
                   jax.ShapeDtypeStruct((nbp, LANE), jnp.int32),
                   jax.ShapeDtypeStruct((8, LANE), jnp.int32)),
        scratch_shapes=[pltpu.VMEM((8, LANE), F32), pltpu.VMEM((tm, tm), BF16), pltpu.VMEM((8, LANE), F32)],
        compiler_params=_params(("arbitrary",)),
        name="dispatch_positions",
    )(idx, cnt)


def _dispatch_kernel(zst_ref, dest_ref, h2_ref, xs_ref, zero_ref, zsem, sem, *, n_experts, bm):
    i = pl.program_id(0)
    tm = dest_ref.shape[1]

    def zero_copy(e):
        return pltpu.make_async_copy(zero_ref, xs_ref.at[pl.ds(zst_ref[e], bm)], zsem)

    def tail_copy(j):
        return pltpu.make_async_copy(zero_ref, xs_ref.at[pl.ds(j * bm, bm)], zsem)

    @pl.when(i == 0)
    def _():
        zero_ref[...] = jnp.zeros_like(zero_ref)
        n_used = zst_ref[n_experts]
        n_blocks = xs_ref.shape[0] // bm
        for e in range(n_experts):
            zero_copy(e).start()
        lax.fori_loop(n_used, n_blocks, lambda j, c: (tail_copy(j).start(), c)[1], 0)
        for e in range(n_experts):
            zero_copy(e).wait()
        lax.fori_loop(n_used, n_blocks, lambda j, c: (tail_copy(j).wait(), c)[1], 0)

    def body(j, carry):
        for r in range(TOP_K):
            d = dest_ref[r, j]
            pltpu.make_async_copy(h2_ref.at[j], xs_ref.at[d], sem).start(priority=r % 2)
        return carry

    lax.fori_loop(0, tm, body, 0, unroll=8)
    for _ in range(TOP_K):
        pltpu.make_async_copy(h2_ref, xs_ref.at[pl.ds(0, tm)], sem).wait()


def _dispatch(zst, dest, h2, n_experts, n_blocks):
    t, _, d = h2.shape
    grid_spec = pltpu.PrefetchScalarGridSpec(
        num_scalar_prefetch=1,
        grid=(t // TM_DISPATCH,),
        in_specs=[pl.BlockSpec((8, TM_DISPATCH), lambda i, z: (0, i), memory_space=pltpu.SMEM),
                  pl.BlockSpec((TM_DISPATCH, 1, d), lambda i, z: (i, 0, 0))],
        out_specs=pl.BlockSpec(memory_space=pl.ANY),
        scratch_shapes=[pltpu.VMEM((BM_EXP, 1, d), h2.dtype),
                        pltpu.SemaphoreType.DMA(()), pltpu.SemaphoreType.DMA(())],
    )
    return pl.pallas_call(
        functools.partial(_dispatch_kernel, n_experts=n_experts, bm=BM_EXP),
        grid_spec=grid_spec,
        out_shape=jax.ShapeDtypeStruct((n_blocks * BM_EXP, 1, d), h2.dtype),
        compiler_params=_params(("arbitrary",)),
        name="row_dispatch",
    )(zst, dest, h2)


def _expert_kernel(sp_ref, xs_ref, wg_ref, wl_ref, bg_ref, bl_ref, wd_ref, bd_ref, ys_ref,
                   x2d_ref, xb_ref, acc2_ref, in_sems, sems, *, n_blocks):
    j = pl.program_id(0)
    f = pl.program_id(1)
    nf = pl.num_programs(1)
    n_used = sp_ref[n_blocks]
    used = j < n_used
    bm = acc2_ref.shape[1]
    slot = j % 2
    acc_ref = acc2_ref.at[slot]

    def in_copy(blk, s):
        src = jnp.minimum(blk, n_used - 1)
        return pltpu.make_async_copy(xs_ref.at[pl.ds(src * bm, bm), 0, :], x2d_ref.at[s], in_sems.at[s])

    def out_copy(blk, s):
        return pltpu.make_async_copy(acc2_ref.at[s], ys_ref.at[pl.ds(blk * bm, bm), 0, :], sems.at[s])

    @pl.when((f == 0) & (j == 0))
    def _():
        in_copy(0, 0).start()

    @pl.when((f == 0) & (j + 1 < n_blocks))
    def _():
        in_copy(j + 1, 1 - slot).start()

    @pl.when(f == 0)
    def _():
        in_copy(j, slot).wait()

    @pl.when((f == 0) & (j >= 2))
    def _():
        out_copy(j - 2, slot).wait()

    @pl.when(used & (f == 0))
    def _():
        words = x2d_ref[slot]
        half = words.shape[1]
        xb_ref[:, :half] = pltpu.bitcast(words << 16, F32).astype(BF16)
        xb_ref[:, half:] = pltpu.bitcast(words & jnp.uint32(0xFFFF0000), F32).astype(BF16)

    @pl.when(jnp.logical_not(used) & (f == 0))
    def _():
        acc_ref[...] = jnp.zeros((bm, acc2_ref.shape[2]), F32)

    def mlp(rows):
        xb = xb_ref[0:rows]
        glu = jnp.dot(xb, wg_ref[0], preferred_element_type=F32) + bg_ref[0]
        lin = jnp.dot(xb, wl_ref[0], preferred_element_type=F32) + bl_ref[0]
        glu = jnp.minimum(glu, SWIGLU_LIMIT)
        lin = jnp.clip(lin, -SWIGLU_LIMIT, SWIGLU_LIMIT)
        act = glu * jax.nn.sigmoid(SWIGLU_ALPHA * glu) * (lin + 1.0)
        prev = jnp.where(f == 0, jnp.broadcast_to(bd_ref[0], (rows, acc2_ref.shape[2])), acc_ref[0:rows])
        acc_ref[0:rows] = prev + jnp.dot(act.astype(BF16), wd_ref[0], preferred_element_type=F32)

    sparse = sp_ref[n_blocks + 1 + jnp.minimum(j, n_blocks - 1)] <= bm // 2

    @pl.when(used & jnp.logical_not(sparse))
    def _():
        mlp(bm)

    @pl.when(used & sparse)
    def _():
        mlp(bm // 2)
        acc_ref[bm // 2:bm] = jnp.zeros((bm - bm // 2, acc2_ref.shape[2]), F32)

    @pl.when(f == nf - 1)
    def _():
        out_copy(j, slot).start()

    @pl.when((f == nf - 1) & (j == n_blocks - 1))
    def _():
        out_copy(j - 1, 1 - slot).wait()
        out_copy(j, slot).wait()


def _experts(sp, xs, w_gu, b_gu, w_dn, b_dn, n_blocks):
    p, _, dh = xs.shape
    n_experts, d, ff2 = w_gu.shape
    assert d == 2 * dh
    ff = ff2 // 2
    nf = ff // TF_EXP
    bm = BM_EXP

    def blk(j, s):
        return jnp.minimum(j, s[n_blocks] - 1)

    def fch(j, f, s):
        return jnp.where(j < s[n_blocks], f, nf - 1)

    grid_spec = pltpu.PrefetchScalarGridSpec(
        num_scalar_prefetch=1,
        grid=(n_blocks, nf),
        in_specs=[
            pl.BlockSpec(memory_space=pl.ANY),
            pl.BlockSpec((1, d, TF_EXP), lambda j, f, s: (s[blk(j, s)], 0, fch(j, f, s))),
            pl.BlockSpec((1, d, TF_EXP), lambda j, f, s: (s[blk(j, s)], 0, nf + fch(j, f, s))),
            pl.BlockSpec((1, 1, TF_EXP), lambda j, f, s: (s[blk(j, s)], 0, fch(j, f, s))),
            pl.BlockSpec((1, 1, TF_EXP), lambda j, f, s: (s[blk(j, s)], 0, nf + fch(j, f, s))),
            pl.BlockSpec((1, TF_EXP, d), lambda j, f, s: (s[blk(j, s)], fch(j, f, s), 0)),
            pl.BlockSpec((1, 1, d), lambda j, f, s: (s[blk(j, s)], 0, 0)),
        ],
        out_specs=pl.BlockSpec(memory_space=pl.ANY),
        scratch_shapes=[pltpu.VMEM((2, bm, dh), jnp.uint32), pltpu.VMEM((bm, d), BF16),
                        pltpu.VMEM((2, bm, d), F32),
                        pltpu.SemaphoreType.DMA((2,)), pltpu.SemaphoreType.DMA((2,))],
    )
    return pl.pallas_call(
        functools.partial(_expert_kernel, n_blocks=n_blocks),
        grid_spec=grid_spec,
        out_shape=jax.ShapeDtypeStruct((p, 1, d), F32),
        compiler_params=_params(("arbitrary", "arbitrary")),
        name="expert_mlp",
    )(sp, xs, w_gu, w_gu, b_gu.reshape(n_experts, 1, ff2), b_gu.reshape(n_experts, 1, ff2),
      w_dn, b_dn.reshape(n_experts, 1, d))


def _combine_kernel(dest_ref, dest_next_ref, gate_ref, x1_ref, mod_ref, g_ref, ys_ref, oa_ref, ob_ref,
                    buf_ref, row_ref, sems, *, first_tiles):
    tm = x1_ref.shape[0]
    i = pl.program_id(0)
    slot = i % 2

    def gather(table_ref, s):
        def body(j, carry):
            for r in range(TOP_K):
                d = table_ref[r, j]
                pltpu.make_async_copy(ys_ref.at[d], buf_ref.at[s, r, j], sems.at[s]).start(priority=r % 2)
            return carry

        lax.fori_loop(0, tm, body, 0, unroll=8)

    @pl.when(i == 0)
    def _():
        gather(dest_ref, 0)

    @pl.when(i + 1 < pl.num_programs(0))
    def _():
        gather(dest_next_ref, 1 - slot)

    for r in range(TOP_K):
        pltpu.make_async_copy(ys_ref.at[pl.ds(0, tm)], buf_ref.at[slot, r], sems.at[slot]).wait()

    gates = gate_ref[...]
    lane = lax.broadcasted_iota(jnp.int32, gates.shape, 1)
    f = None
    for r in range(TOP_K):
        g = jnp.sum(jnp.where(lane == r, gates, 0.0), axis=-1, keepdims=True)
        row_ref[...] = buf_ref[slot, r].reshape(row_ref.shape)
        term = row_ref[...] * g
        f = term if f is None else f + term
    out = x1_ref[...] + mod_ref[0, 5:6, :] * _rms(f, g_ref[...])

    @pl.when(pl.program_id(0) < first_tiles)
    def _():
        oa_ref[...] = out

    @pl.when(pl.program_id(0) >= first_tiles)
    def _():
        ob_ref[...] = out


def _combine(dest, gates, x1, mod3, g_post, ys, seq, n_first):
    t, d = x1.shape
    tm = TM_DMA
    tps = seq // tm
    first_tiles = n_first * tps
    n_tiles = t // tm
    return pl.pallas_call(
        functools.partial(_combine_kernel, first_tiles=first_tiles),
        grid=(n_tiles,),
        in_specs=[
            pl.BlockSpec((8, tm), lambda i: (0, i), memory_space=pltpu.SMEM),
            pl.BlockSpec((8, tm), lambda i: (0, jnp.minimum(i + 1, n_tiles - 1)), memory_space=pltpu.SMEM),
            pl.BlockSpec((tm, LANE), lambda i: (i, 0)),
            pl.BlockSpec((tm, d), lambda i: (i, 0)),
            pl.BlockSpec((1, N_MOD, d), lambda i: (i // tps, 0, 0)),
            pl.BlockSpec((1, d), lambda i: (0, 0)),
            pl.BlockSpec(memory_space=pl.ANY),
        ],
        out_specs=(pl.BlockSpec((tm, d), lambda i: (jnp.minimum(i, first_tiles - 1), 0)),
                   pl.BlockSpec((tm, d), lambda i: (jnp.maximum(i - first_tiles, 0), 0))),
        out_shape=(jax.ShapeDtypeStruct((first_tiles * tm, d), F32),
                   jax.ShapeDtypeStruct((t - first_tiles * tm, d), F32)),
        scratch_shapes=[pltpu.VMEM((2, TOP_K, tm, 1, d), F32), pltpu.VMEM((tm, d), F32),
                        pltpu.SemaphoreType.DMA((2,))],
        compiler_params=_params(("arbitrary",)),
        name="expert_combine",
    )(dest, dest, gates, x1, mod3, g_post.reshape(1, d), ys)


def _layout_w_in(w_in, qr, kvr):
    qsw = SWA_Q_HEADS * SWA_HEAD_DIM
    ksw = SWA_KV_HEADS * SWA_HEAD_DIM
    d = w_in.shape[0]
    sizes = (qr, kvr, MLA_ROPE, qsw, ksw, ksw, d, d)
    offs = [0]
    for s in sizes:
        offs.append(offs[-1] + s)
    c_q, c_kv, k_rope, q_s, k_s, v_s, g_a, g_b = [w_in[:, offs[i]:offs[i + 1]] for i in range(8)]
    parts = (("q_s", q_s), ("g_a", g_a), ("g_b", g_b), ("c_q", c_q), ("c_kv", c_kv),
             ("k_s", k_s), ("v_s", v_s), ("k_rope", jnp.concatenate([k_rope, k_rope], axis=1)))
    cols, off = {}, 0
    for name, p in parts:
        cols[name] = off
        off += p.shape[1]
    return jnp.concatenate([p for _, p in parts], axis=1).astype(BF16), cols


def _layout_w_uq(w_uq):
    r = w_uq.shape[0]
    w = w_uq.reshape(r, MLA_HEADS, MLA_NOPE + MLA_ROPE)
    nope = w[:, :, :MLA_NOPE].reshape(r, MLA_HEADS * MLA_NOPE)
    pe = w[:, :, MLA_NOPE:].reshape(r, MLA_HEADS * MLA_ROPE)
    return jnp.concatenate([nope, pe], axis=1).astype(BF16)


def _layout_w_ukv(w_ukv):
    r = w_ukv.shape[0]
    w = w_ukv.reshape(r, MLA_HEADS, MLA_NOPE + MLA_V)
    kn = w[:, :, :MLA_NOPE].reshape(r, MLA_HEADS * MLA_NOPE)
    v = w[:, :, MLA_NOPE:].reshape(r, MLA_HEADS * MLA_V)
    return jnp.concatenate([kn, v], axis=1).astype(BF16)


def _rope_tables(seq):
    half = MLA_ROPE // 2
    freqs = jnp.power(ROPE_THETA, -2.0 * jnp.arange(half, dtype=F32) / MLA_ROPE)
    ang = jnp.arange(seq, dtype=F32)[:, None] * freqs[None, :]
    cos, sin = jnp.cos(ang), jnp.sin(ang)
    return (jnp.concatenate([cos, cos, cos, cos], axis=1),
            jnp.concatenate([-sin, sin, -sin, sin], axis=1))


def kernel(x_prompt, x_sample, c_prompt, c_sample, w_ada, b_ada, g_pre_mix, w_in, g_q_lat, w_uq,
           g_kv_lat, w_ukv, attn_sinks, w_br_mla, w_br_swa, w_out, g_post_mix, g_pre_ffn,
           w_router, b_router, w_gu, b_gu, w_dn, b_dn, g_post_ffn):
    assert MLA_ROPE == SWA_HEAD_DIM == 64 and MLA_NOPE == MLA_V == LANE
    nb_p, seq, d = x_prompt.shape
    nb_s = x_sample.shape[0]
    assert x_sample.shape[1] == seq
    nseq = nb_p + nb_s
    t = nseq * seq
    n_experts = w_router.shape[-1]
    n_blocks = t * TOP_K // BM_EXP + n_experts

    xa, xb = x_prompt.reshape(nb_p * seq, d), x_sample.reshape(nb_s * seq, d)
    c8 = jnp.concatenate([c_prompt, c_sample, jnp.zeros((8 - nseq, d), F32)], axis=0)
    cos, sin = _rope_tables(seq)

    for l in range(w_ada.shape[0]):
        w_in_p, cols = _layout_w_in(w_in[l], g_q_lat.shape[-1], g_kv_lat.shape[-1])
        w_uq_p = _layout_w_uq(w_uq[l])
        w_ukv_p = _layout_w_ukv(w_ukv[l])
        w_router_p = jnp.pad(w_router[l], ((0, 0), (0, LANE - n_experts))).astype(BF16)
        b_router_p = jnp.pad(b_router[l], (0, LANE - n_experts)).reshape(1, LANE)

        mod = _modulation(c8, w_ada[l], b_ada[l])
        mod3 = mod[:nseq].reshape(nseq, N_MOD, d)
        z = _in_projection(xa, xb, mod3, g_pre_mix[l], w_in_p, seq)
        q, k, v, qs, klo, khi, vlo, vhi = _post_projection(
            z, cols, cos, sin, g_q_lat[l], g_kv_lat[l], w_uq_p, w_ukv_p, nseq, seq)
        o_a, w_gu_b, w_dn_b = _mla_attention(q, k, v, w_gu[l], w_dn[l])
        o_b = _swa_attention(attn_sinks[l], qs, klo, khi, vlo, vhi, nseq, seq)
        y = _merge(o_a, o_b, z, cols, w_br_mla[l].astype(BF16), w_br_swa[l].astype(BF16))
        x1, h2, idx, gates, cnt = _out_projection(
            y, xa, xb, mod3, g_post_mix[l], g_pre_ffn[l], w_out[l].astype(BF16), w_router_p, b_router_p,
            n_experts, seq)
        dest, meta, zst = _positions(idx, cnt, n_experts, n_blocks)
        sp = jnp.concatenate([meta[:n_blocks, 0], meta[0:1, 1], meta[:n_blocks, 2]])
        zmeta = jnp.concatenate([zst[0, :n_experts], meta[0:1, 1]])
        xs = _dispatch(zmeta, dest, h2, n_experts, n_blocks)
        ys = _experts(sp, xs, w_gu_b, b_gu[l], w_dn_b, b_dn[l], n_blocks)
        xa, xb = _combine(dest, gates, x1, mod3, g_post_ffn[l], ys, seq, nb_p)

    return (xa.reshape(nb_p, seq, d), xb.reshape(nb_s, seq, d))
```

```python
import functools

import jax
import jax.numpy as jnp
from jax import lax
from jax.experimental import pallas as pl
from jax.experimental.pallas import tpu as pltpu

MLA_HEADS = 16
MLA_NOPE = 128
MLA_ROPE = 64
MLA_V = 128
SWA_Q_HEADS = 32
SWA_KV_HEADS = 8
SWA_HEAD_DIM = 64
WINDOW = 128
TOP_K = 4
SWIGLU_LIMIT = 7.0
SWIGLU_ALPHA = 1.702
ROPE_THETA = 10000.0
RMS_EPS = 1e-6
N_MOD = 6

LANE = 128
VMEM_LIMIT = 56 << 20

TM_IN = 512
TN_IN = 1664
TM_POST = 256
TQ_MLA = 1024
KV_CHUNKS_MLA = (1024, 1024, 1024, 1024)
TQ_SWA = 512
SUB_SWA = 256
TM_MERGE = 1024
TN_MERGE = 1024
TM_OUT = 512
OUT_ROW_SPLITS = 2
TM_POS = 512
TM_DISPATCH = 512
TM_DMA = 256
BM_EXP = 512
TF_EXP = 1024

F32 = jnp.float32
BF16 = jnp.bfloat16
NEG_INF = float("-inf")
LOG2E = 1.4426950408889634


def _params(sem):
    return pltpu.CompilerParams(dimension_semantics=sem, vmem_limit_bytes=VMEM_LIMIT)


def _rms(x, g):
    return x * lax.rsqrt(jnp.mean(x * x, axis=-1, keepdims=True) + RMS_EPS) * g


def _mod_kernel(c_ref, w_ref, b_ref, win_ref, o_ref, wout_ref, *, plan, n_slabs):
    c = c_ref[...]
    a = (c * jax.nn.sigmoid(c)).astype(BF16)
    o_ref[...] = jnp.dot(a, w_ref[...].astype(BF16), preferred_element_type=F32) + b_ref[...]

    @pl.when(pl.program_id(0) < n_slabs)
    def _():
        for src, width, dst in plan:
            wout_ref[:, dst:dst + width] = win_ref[:, src:src + width].astype(BF16)


def _modulation(c8, w_ada, b_ada, w_in, plan, n_cols):
    d, n = w_ada.shape
    tn = 1024
    n_steps = n // tn
    n_slabs = 8
    assert n_slabs <= n_steps and d % (8 * n_slabs) == 0
    rows = d // n_slabs
    slab = lambda j: (jnp.minimum(j, n_slabs - 1), 0)
    return pl.pallas_call(
        functools.partial(_mod_kernel, plan=plan, n_slabs=n_slabs),
        grid=(n_steps,),
        in_specs=[
            pl.BlockSpec((8, d), lambda j: (0, 0)),
            pl.BlockSpec((d, tn), lambda j: (0, j)),
            pl.BlockSpec((1, tn), lambda j: (0, j)),
            pl.BlockSpec((rows, w_in.shape[1]), slab),
        ],
        out_specs=(pl.BlockSpec((8, tn), lambda j: (0, j)),
                   pl.BlockSpec((rows, n_cols), slab)),
        out_shape=(jax.ShapeDtypeStruct((8, n), F32),
                   jax.ShapeDtypeStruct((d, n_cols), BF16)),
        compiler_params=_params(("arbitrary",)),
        name="adaln_mod",
    )(c8, w_ada, b_ada.reshape(1, n), w_in)


def _two_group_specs(tm, d, first_tiles, n_grid_axes):
    if n_grid_axes == 1:
        return (pl.BlockSpec((tm, d), lambda i: (jnp.minimum(i, first_tiles - 1), 0)),
                pl.BlockSpec((tm, d), lambda i: (jnp.maximum(i - first_tiles, 0), 0)))
    return (pl.BlockSpec((tm, d), lambda i, j: (jnp.minimum(i, first_tiles - 1), 0)),
            pl.BlockSpec((tm, d), lambda i, j: (jnp.maximum(i - first_tiles, 0), 0)))


def _inproj_kernel(xa_ref, xb_ref, mod_ref, g_ref, w_ref, z_ref, h_ref, *, first_tiles):
    def normalise(x_ref):
        x = x_ref[...]
        gain = g_ref[...] * (1.0 + mod_ref[0, 1:2, :])
        h = x * lax.rsqrt(jnp.mean(x * x, axis=-1, keepdims=True) + RMS_EPS) * gain + mod_ref[0, 0:1, :]
        h_ref[...] = h.astype(BF16)

    first_col = pl.program_id(1) == 0
    in_first = pl.program_id(0) < first_tiles
    pl.when(first_col & in_first)(functools.partial(normalise, xa_ref))
    pl.when(first_col & jnp.logical_not(in_first))(functools.partial(normalise, xb_ref))

    z_ref[...] = jnp.dot(h_ref[...], w_ref[...], preferred_element_type=F32).astype(z_ref.dtype)


def _in_projection(xa, xb, mod3, g, w_in_p, seq):
    d = xa.shape[1]
    t = xa.shape[0] + xb.shape[0]
    n = w_in_p.shape[1]
    tiles_per_seq = seq // TM_IN
    first_tiles = xa.shape[0] // TM_IN
    return pl.pallas_call(
        functools.partial(_inproj_kernel, first_tiles=first_tiles),
        grid=(t // TM_IN, n // TN_IN),
        in_specs=[
            *_two_group_specs(TM_IN, d, first_tiles, 2),
            pl.BlockSpec((1, N_MOD, d), lambda i, j: (i // tiles_per_seq, 0, 0)),
            pl.BlockSpec((1, d), lambda i, j: (0, 0)),
            pl.BlockSpec((d, TN_IN), lambda i, j: (0, j)),
        ],
        out_specs=pl.BlockSpec((TM_IN, TN_IN), lambda i, j: (i, j)),
        out_shape=jax.ShapeDtypeStruct((t, n), BF16),
        scratch_shapes=[pltpu.VMEM((TM_IN, d), BF16)],
        compiler_params=_params(("arbitrary", "arbitrary")),
        name="in_projection",
    )(xa, xb, mod3, g.reshape(1, d), w_in_p)


def _postproj_kernel(cq_ref, ckv_ref, qs_ref, ks_ref, vs_ref, kr_ref, cos_ref, sin_ref,
                     gq_ref, gkv_ref, wuq_ref, wukv_ref,
                     q_ref, k_ref, v_ref, qso_ref, klo_ref, khi_ref, vlo_ref, vhi_ref):
    tm = cq_ref.shape[0]
    cos = cos_ref[...]
    sin = sin_ref[...]
    lane = lax.broadcasted_iota(jnp.int32, (tm, LANE), 1)
    first_half = (lane & 63) < 32
    low = lane < 64

    def rope(x, cos=cos, sin=sin):
        rot = jnp.where(first_half, pltpu.roll(x, LANE - 32, 1), pltpu.roll(x, 32, 1))
        return x * cos + rot * sin

    nh = MLA_HEADS
    scale = float((MLA_NOPE + MLA_ROPE) ** -0.5) * LOG2E
    cqn = (_rms(cq_ref[...].astype(F32), gq_ref[...]) * scale).astype(BF16)
    q = jnp.dot(cqn, wuq_ref[...], preferred_element_type=F32)
    for h in range(nh):
        q_ref[0, h, :, 0:LANE] = q[:, h * LANE:(h + 1) * LANE].astype(BF16)
    for m in range(nh // 2):
        pe = rope(q[:, (nh + m) * LANE:(nh + m + 1) * LANE]).astype(BF16)
        q_ref[0, 2 * m, :, LANE:2 * LANE] = pe
        q_ref[0, 2 * m + 1, :, LANE:2 * LANE] = pe

    ckvn = _rms(ckv_ref[...].astype(F32), gkv_ref[...]).astype(BF16)
    kv = jnp.dot(ckvn, wukv_ref[...], preferred_element_type=F32)
    kr = rope(kr_ref[...].astype(F32))
    kpe_lo = jnp.where(low, kr, 0.0).astype(BF16)
    kpe_hi = jnp.where(low, 0.0, kr).astype(BF16)
    for h in range(nh):
        k_ref[0, h, :, 0:LANE] = kv[:, h * LANE:(h + 1) * LANE].astype(BF16)
        k_ref[0, h, :, LANE:2 * LANE] = kpe_lo if h % 2 == 0 else kpe_hi
        v_ref[0, h, :, :] = kv[:, (nh + h) * LANE:(nh + h + 1) * LANE].astype(BF16)

    swa_scale = float(SWA_HEAD_DIM ** -0.5) * LOG2E
    cos_q, sin_q = cos * swa_scale, sin * swa_scale
    for m in range(SWA_Q_HEADS // 2):
        x = qs_ref[:, m * LANE:(m + 1) * LANE].astype(F32)
        qso_ref[:, m * LANE:(m + 1) * LANE] = rope(x, cos_q, sin_q).astype(BF16)

    for m in range(SWA_KV_HEADS // 2):
        sl = slice(m * LANE, (m + 1) * LANE)
        for src_ref, lo_ref, hi_ref, roped in ((ks_ref, klo_ref, khi_ref, True),
                                               (vs_ref, vlo_ref, vhi_ref, False)):
            a = src_ref[:, sl].astype(F32)
            if roped:
                a = rope(a)
            b = pltpu.roll(a, 64, 1)
            e0 = slice((2 * m) * LANE, (2 * m + 1) * LANE)
            e1 = slice((2 * m + 1) * LANE, (2 * m + 2) * LANE)
            lo_ref[:, e0] = jnp.where(low, a, 0.0).astype(BF16)
            hi_ref[:, e0] = jnp.where(low, 0.0, b).astype(BF16)
            lo_ref[:, e1] = jnp.where(low, b, 0.0).astype(BF16)
            hi_ref[:, e1] = jnp.where(low, 0.0, a).astype(BF16)


def _post_projection(z, cols, cos, sin, g_q, g_kv, w_uq_p, w_ukv_p, nseq, seq):
    t = z.shape[0]
    tm = TM_POST
    tps = seq // tm
    nh = MLA_HEADS
    qr, kvr = g_q.shape[0], g_kv.shape[0]
    qsw = SWA_Q_HEADS * SWA_HEAD_DIM
    ksw = SWA_KV_HEADS * SWA_HEAD_DIM

    def zspec(width, off):
        blk = off // width
        return pl.BlockSpec((tm, width), lambda i: (i, blk))

    head_map = lambda i: (i // tps, 0, i % tps, 0)
    tok_map = lambda i: (i, 0)
    out_shapes = (
        jax.ShapeDtypeStruct((nseq, nh, seq, 2 * LANE), BF16),
        jax.ShapeDtypeStruct((nseq, nh, seq, 2 * LANE), BF16),
        jax.ShapeDtypeStruct((nseq, nh, seq, LANE), BF16),
        jax.ShapeDtypeStruct((t, qsw), BF16),
        jax.ShapeDtypeStruct((t, SWA_KV_HEADS * LANE), BF16),
        jax.ShapeDtypeStruct((t, SWA_KV_HEADS * LANE), BF16),
        jax.ShapeDtypeStruct((t, SWA_KV_HEADS * LANE), BF16),
        jax.ShapeDtypeStruct((t, SWA_KV_HEADS * LANE), BF16),
    )
    out_specs = (
        pl.BlockSpec((1, nh, tm, 2 * LANE), head_map),
        pl.BlockSpec((1, nh, tm, 2 * LANE), head_map),
        pl.BlockSpec((1, nh, tm, LANE), head_map),
        pl.BlockSpec((tm, qsw), tok_map),
        pl.BlockSpec((tm, SWA_KV_HEADS * LANE), tok_map),
        pl.BlockSpec((tm, SWA_KV_HEADS * LANE), tok_map),
        pl.BlockSpec((tm, SWA_KV_HEADS * LANE), tok_map),
        pl.BlockSpec((tm, SWA_KV_HEADS * LANE), tok_map),
    )
    return pl.pallas_call(
        _postproj_kernel,
        grid=(t // tm,),
        in_specs=[
            zspec(qr, cols["c_q"]), zspec(kvr, cols["c_kv"]), zspec(qsw, cols["q_s"]),
            zspec(ksw, cols["k_s"]), zspec(ksw, cols["v_s"]), zspec(LANE, cols["k_rope"]),
            pl.BlockSpec((tm, LANE), lambda i: (i % tps, 0)),
            pl.BlockSpec((tm, LANE), lambda i: (i % tps, 0)),
            pl.BlockSpec((1, qr), lambda i: (0, 0)),
            pl.BlockSpec((1, kvr), lambda i: (0, 0)),
            pl.BlockSpec(w_uq_p.shape, lambda i: (0, 0)),
            pl.BlockSpec(w_ukv_p.shape, lambda i: (0, 0)),
        ],
        out_specs=out_specs,
        out_shape=out_shapes,
        compiler_params=_params(("arbitrary",)),
        name="post_projection",
    )(z, z, z, z, z, z, cos, sin, g_q.reshape(1, qr), g_kv.reshape(1, kvr), w_uq_p, w_ukv_p)


def _mla_kernel(q_ref, k_ref, v_ref, wgu_ref, wdn_ref, o_ref, wgu_o_ref, wdn_o_ref, *, n_gu):
    step = (pl.program_id(0) * pl.num_programs(1) + pl.program_id(1)) * pl.num_programs(2) + pl.program_id(2)

    @pl.when(step < n_gu)
    def _():
        wgu_o_ref[...] = wgu_ref[...].astype(BF16)

    @pl.when(step >= n_gu)
    def _():
        wdn_o_ref[...] = wdn_ref[...].astype(BF16)

    q = q_ref[0, 0]
    seq = k_ref.shape[2]
    m = l = acc = None
    assert sum(KV_CHUNKS_MLA) == seq
    start = 0
    for width in KV_CHUNKS_MLA:
        rows = slice(start, start + width)
        start += width
        s = lax.dot_general(q, k_ref[0, 0, rows, :], (((1,), (1,)), ((), ())),
                            preferred_element_type=F32)
        mc = jnp.max(s, axis=-1, keepdims=True)
        m_new = mc if m is None else jnp.maximum(m, mc)
        p = jnp.exp2(s - m_new)
        ps = jnp.sum(p, axis=-1, keepdims=True)
        pv = jnp.dot(p.astype(BF16), v_ref[0, 0, rows, :], preferred_element_type=F32)
        if m is None:
            l, acc = ps, pv
        else:
            alpha = jnp.exp2(m - m_new)
            l = alpha * l + ps
            acc = alpha * acc + pv
        m = m_new
    o_ref[...] = (acc / l).astype(o_ref.dtype)


def _mla_attention(q, k, v, w_gu, w_dn):
    nseq, nh, seq, dqk = q.shape
    dv = v.shape[-1]
    nq = seq // TQ_MLA
    n_steps = nseq * nh * nq
    gu2d = w_gu.reshape(-1, w_gu.shape[-1])
    dn2d = w_dn.reshape(-1, w_dn.shape[-1])
    n_gu, n_dn = 2 * n_steps // 3, n_steps // 3
    assert n_gu + n_dn == n_steps and gu2d.shape[0] % n_gu == 0 and dn2d.shape[0] % n_dn == 0
    gu_rows, dn_rows = gu2d.shape[0] // n_gu, dn2d.shape[0] // n_dn
    assert gu_rows % 16 == 0 and dn_rows % 16 == 0

    def step(b, h, i):
        return (b * nh + h) * nq + i

    gu_map = lambda b, h, i: (jnp.minimum(step(b, h, i), n_gu - 1), 0)
    dn_map = lambda b, h, i: (jnp.maximum(step(b, h, i) - n_gu, 0), 0)
    o, gu_b, dn_b = pl.pallas_call(
        functools.partial(_mla_kernel, n_gu=n_gu),
        grid=(nseq, nh, nq),
        in_specs=[
            pl.BlockSpec((1, 1, TQ_MLA, dqk), lambda b, h, i: (b, h, i, 0)),
            pl.BlockSpec((1, 1, seq, dqk), lambda b, h, i: (b, h, 0, 0)),
            pl.BlockSpec((1, 1, seq, dv), lambda b, h, i: (b, h, 0, 0)),
            pl.BlockSpec((gu_rows, gu2d.shape[1]), gu_map),
            pl.BlockSpec((dn_rows, dn2d.shape[1]), dn_map),
        ],
        out_specs=(
            pl.BlockSpec((TQ_MLA, dv), lambda b, h, i: (b * nq + i, h)),
            pl.BlockSpec((gu_rows, gu2d.shape[1]), gu_map),
            pl.BlockSpec((dn_rows, dn2d.shape[1]), dn_map),
        ),
        out_shape=(
            jax.ShapeDtypeStruct((nseq * seq, nh * dv), BF16),
            jax.ShapeDtypeStruct(gu2d.shape, BF16),
            jax.ShapeDtypeStruct(dn2d.shape, BF16),
        ),
        compiler_params=_params(("arbitrary", "arbitrary", "arbitrary")),
        name="mla_attention",
    )(q, k, v, gu2d, dn2d)
    return o, gu_b.reshape(w_gu.shape), dn_b.reshape(w_dn.shape)


def _swa_kernel(sink_ref, q_ref,
                klo_p, klo_c, klo_n, khi_p, khi_c, khi_n,
                vlo_p, vlo_c, vlo_n, vhi_p, vhi_c, vhi_n,
                o_ref, klo_w, khi_w, vlo_w, vhi_w, *, seq):
    tb = q_ref.shape[0]
    tq = SUB_SWA
    w = WINDOW
    for win, (p, c, n) in ((klo_w, (klo_p, klo_c, klo_n)), (khi_w, (khi_p, khi_c, khi_n)),
                           (vlo_w, (vlo_p, vlo_c, vlo_n)), (vhi_w, (vhi_p, vhi_c, vhi_n))):
        win[0:w, :] = p[...]
        win[w:w + tb, :] = c[...]
        win[w + tb:tb + 2 * w, :] = n[...]

    assert SWA_Q_HEADS // SWA_KV_HEADS == 4 and tq & (tq - 1) == 0 and tb % tq == 0
    nk = tq + 2 * w
    row2 = lax.broadcasted_iota(jnp.int32, (2 * tq, nk), 0)
    col = lax.broadcasted_iota(jnp.int32, (2 * tq, nk), 1)
    rel = col - w - (row2 & (tq - 1))
    band = (rel <= w) & (rel >= -w)
    top = lax.broadcasted_iota(jnp.int32, (2 * tq, 1), 0) < tq

    for sub in range(tb // tq):
        qrows = slice(sub * tq, (sub + 1) * tq)
        krows = slice(sub * tq, sub * tq + nk)
        kpos = pl.program_id(1) * tb + sub * tq - w + col
        valid = band & (kpos >= 0) & (kpos < seq)
        for g in range(SWA_KV_HEADS):
            gs = slice(g * LANE, (g + 1) * LANE)
            pa = slice((2 * g) * LANE, (2 * g + 1) * LANE)
            pb = slice((2 * g + 1) * LANE, (2 * g + 2) * LANE)
            qq = jnp.concatenate([q_ref[qrows, pa], q_ref[qrows, pb]], axis=0)
            acc = None
            for half, (kw, vw) in enumerate(((klo_w, vlo_w), (khi_w, vhi_w))):
                sink = jnp.where(top, sink_ref[4 * g + half], sink_ref[4 * g + 2 + half]) * LOG2E
                s = lax.dot_general(qq, kw[krows, gs], (((1,), (1,)), ((), ())),
                                    preferred_element_type=F32)
                s = jnp.where(valid, s, NEG_INF)
                mx = jnp.maximum(jnp.max(s, axis=-1, keepdims=True), sink)
                e = jnp.exp2(s - mx)
                den = jnp.sum(e, axis=-1, keepdims=True) + jnp.exp2(sink - mx)
                pv = jnp.dot(e.astype(BF16), vw[krows, gs], preferred_element_type=F32) / den
                acc = pv if acc is None else acc + pv
            o_ref[qrows, pa] = acc[0:tq].astype(o_ref.dtype)
            o_ref[qrows, pb] = acc[tq:2 * tq].astype(o_ref.dtype)


def _swa_attention(sinks, qs, klo, khi, vlo, vhi, nseq, seq):
    t, qw = qs.shape
    kw = klo.shape[1]
    tq = TQ_SWA
    nq = seq // tq
    r = tq // WINDOW
    nwb = seq // WINDOW

    prev = pl.BlockSpec((WINDOW, kw), lambda b, i, s: (b * nwb + jnp.maximum(i * r - 1, 0), 0))
    cur = pl.BlockSpec((tq, kw), lambda b, i, s: (b * nq + i, 0))
    nxt = pl.BlockSpec((WINDOW, kw), lambda b, i, s: (b * nwb + jnp.minimum((i + 1) * r, nwb - 1), 0))
    grid_spec = pltpu.PrefetchScalarGridSpec(
        num_scalar_prefetch=1,
        grid=(nseq, nq),
        in_specs=[pl.BlockSpec((tq, qw), lambda b, i, s: (b * nq + i, 0))] + [prev, cur, nxt] * 4,
        out_specs=pl.BlockSpec((tq, qw), lambda b, i, s: (b * nq + i, 0)),
        scratch_shapes=[pltpu.VMEM((tq + 2 * WINDOW, kw), BF16)] * 4,
    )
    return pl.pallas_call(
        functools.partial(_swa_kernel, seq=seq),
        grid_spec=grid_spec,
        out_shape=jax.ShapeDtypeStruct((t, qw), BF16),
        compiler_params=_params(("arbitrary", "arbitrary")),
        name="swa_attention",
    )(sinks, qs, klo, klo, klo, khi, khi, khi, vlo, vlo, vlo, vhi, vhi, vhi)


def _merge_kernel(oa_ref, ob_ref, ga_ref, gb_ref, wa_ref, wb_ref, y_ref):
    a = jnp.dot(oa_ref[...], wa_ref[...], preferred_element_type=F32)
    b = jnp.dot(ob_ref[...], wb_ref[...], preferred_element_type=F32)
    y = jax.nn.sigmoid(ga_ref[...].astype(F32)) * a + jax.nn.sigmoid(gb_ref[...].astype(F32)) * b
    y_ref[...] = y.astype(y_ref.dtype)


def _merge(o_a, o_b, z, cols, w_a, w_b):
    t, d = o_a.shape[0], w_a.shape[1]
    tm, tn = TM_MERGE, TN_MERGE
    ga_blk, gb_blk = cols["g_a"] // tn, cols["g_b"] // tn
    return pl.pallas_call(
        _merge_kernel,
        grid=(t // tm, d // tn),
        in_specs=[
            pl.BlockSpec((tm, o_a.shape[1]), lambda i, j: (i, 0)),
            pl.BlockSpec((tm, o_b.shape[1]), lambda i, j: (i, 0)),
            pl.BlockSpec((tm, tn), lambda i, j: (i, ga_blk + j)),
            pl.BlockSpec((tm, tn), lambda i, j: (i, gb_blk + j)),
            pl.BlockSpec((w_a.shape[0], tn), lambda i, j: (0, j)),
            pl.BlockSpec((w_b.shape[0], tn), lambda i, j: (0, j)),
        ],
        out_specs=pl.BlockSpec((tm, tn), lambda i, j: (i, j)),
        out_shape=jax.ShapeDtypeStruct((t, d), BF16),
        compiler_params=_params(("arbitrary", "arbitrary")),
        name="branch_merge",
    )(o_a, o_b, z, z, w_a, w_b)


def _outproj_kernel(y_ref, xa_ref, xb_ref, mod_ref, gpm_ref, gpf_ref, wo_ref, wr_ref, br_ref,
                    x1_ref, h2_ref, idx_ref, gate_ref, cnt_ref, pack_ref, *, n_experts, first_tiles):
    tm = y_ref.shape[0]
    u_all = jnp.dot(y_ref[...], wo_ref[...], preferred_element_type=F32)
    in_first = pl.program_id(0) < first_tiles
    hm = tm // OUT_ROW_SPLITS
    lane = lax.broadcasted_iota(jnp.int32, (hm, LANE), 1)
    lane_f = lane.astype(F32)
    counts = None
    for part in range(OUT_ROW_SPLITS):
        rows = slice(part * hm, (part + 1) * hm)
        x = jnp.where(in_first, xa_ref[rows, :], xb_ref[rows, :])
        x1 = x + mod_ref[0, 2:3, :] * _rms(u_all[rows], gpm_ref[...])
        x1_ref[rows, :] = x1
        h2 = _rms(x1, gpf_ref[...]) * (1.0 + mod_ref[0, 4:5, :]) + mod_ref[0, 3:4, :]
        h2b = h2.astype(BF16)
        bits = pltpu.bitcast(h2b.astype(F32), jnp.uint32)
        half = bits.shape[1] // 2
        pack_ref[rows, :] = (bits[:, :half] >> 16) | (bits[:, half:] & jnp.uint32(0xFFFF0000))
        h2_ref[rows] = pack_ref[rows, :].reshape(hm, 1, half)

        logits = jnp.dot(h2b, wr_ref[...], preferred_element_type=F32) + br_ref[...]
        cur = jnp.where(lane < n_experts, logits, NEG_INF)
        vals, idxs = [], []
        for _ in range(TOP_K):
            m = jnp.max(cur, axis=-1, keepdims=True)
            ix = jnp.min(jnp.where(cur == m, lane_f, float(LANE)), axis=-1, keepdims=True)
            vals.append(m)
            idxs.append(ix)
            cur = jnp.where(lane_f == ix, NEG_INF, cur)
        es = [jnp.exp(v - vals[0]) for v in vals]
        den = es[0]
        for e in es[1:]:
            den = den + e
        idx_out = jnp.zeros((hm, LANE), F32)
        gate_out = jnp.zeros((hm, LANE), F32)
        sel = jnp.zeros((hm, LANE), F32)
        for r in range(TOP_K):
            idx_out = jnp.where(lane == r, idxs[r], idx_out)
            gate_out = jnp.where(lane == r, es[r] / den, gate_out)
            sel = sel + jnp.where(lane_f == idxs[r], 1.0, 0.0)
        idx_ref[rows, :] = idx_out.astype(jnp.int32)
        gate_ref[rows, :] = gate_out
        part_counts = jnp.sum(sel, axis=0, keepdims=True)
        counts = part_counts if counts is None else counts + part_counts

    @pl.when(pl.program_id(0) == 0)
    def _():
        cnt_ref[...] = jnp.zeros_like(cnt_ref)

    cnt_ref[0:1, :] += counts


def _out_projection(y, xa, xb, mod3, g_pm, g_pf, w_out, w_router_p, b_router_p, n_experts, seq):
    t, d = y.shape
    tm = TM_OUT
    tps = seq // tm
    first_tiles = xa.shape[0] // tm
    tok = lambda i: (i, 0)
    const = lambda i: (0, 0)
    return pl.pallas_call(
        functools.partial(_outproj_kernel, n_experts=n_experts, first_tiles=first_tiles),
        grid=(t // tm,),
        in_specs=[
            pl.BlockSpec((tm, d), tok),
            *_two_group_specs(tm, d, first_tiles, 1),
            pl.BlockSpec((1, N_MOD, d), lambda i: (i // tps, 0, 0)),
            pl.BlockSpec((1, d), const),
            pl.BlockSpec((1, d), const),
            pl.BlockSpec((d, d), const),
            pl.BlockSpec((d, LANE), const),
            pl.BlockSpec((1, LANE), const),
        ],
        out_specs=(
            pl.BlockSpec((tm, d), tok),
            pl.BlockSpec((tm, 1, d // 2), lambda i: (i, 0, 0)),
            pl.BlockSpec((tm, LANE), tok),
            pl.BlockSpec((tm, LANE), tok),
            pl.BlockSpec((8, LANE), const),
        ),
        out_shape=(
            jax.ShapeDtypeStruct((t, d), F32),
            jax.ShapeDtypeStruct((t, 1, d // 2), jnp.uint32),
            jax.ShapeDtypeStruct((t, LANE), jnp.int32),
            jax.ShapeDtypeStruct((t, LANE), F32),
            jax.ShapeDtypeStruct((8, LANE), F32),
        ),
        scratch_shapes=[pltpu.VMEM((tm, d // 2), jnp.uint32)],
        compiler_params=_params(("arbitrary",)),
        name="out_projection_router",
    )(y, xa, xb, mod3, g_pm.reshape(1, d), g_pf.reshape(1, d), w_out, w_router_p, b_router_p)


def _positions_kernel(idx_ref, cnt_ref, dest_ref, meta_ref, zst_ref, carry_ref, ltri_ref, pst_ref,
                      *, n_experts, bm):
    tm = idx_ref.shape[0]
    nbp = meta_ref.shape[0]

    @pl.when(pl.program_id(0) == 0)
    def _():
        r = lax.broadcasted_iota(jnp.int32, (tm, tm), 0)
        c = lax.broadcasted_iota(jnp.int32, (tm, tm), 1)
        ltri_ref[...] = jnp.where(c < r, 1.0, 0.0).astype(BF16)
        carry_ref[...] = jnp.zeros_like(carry_ref)
        nblk = jnp.floor((cnt_ref[...] + float(bm - 1)) * (1.0 / bm))
        ur = lax.broadcasted_iota(jnp.int32, (LANE, LANE), 0)
        uc = lax.broadcasted_iota(jnp.int32, (LANE, LANE), 1)
        upper = jnp.where(ur <= uc, 1.0, 0.0).astype(BF16)
        pend = jnp.dot(nblk.astype(BF16), upper, preferred_element_type=F32)
        pst_ref[...] = (pend - nblk) * float(bm)
        lane8 = lax.broadcasted_iota(jnp.int32, (8, LANE), 1)
        zst_ref[...] = jnp.maximum(pend * float(bm) - float(bm), 0.0).astype(jnp.int32)
        pend0 = pend[0:1, :]
        n_used = jnp.sum(jnp.where(lane8[0:1, :] == n_experts - 1, pend0, 0.0), axis=-1, keepdims=True)
        blk = lax.broadcasted_iota(jnp.int32, (nbp, LANE), 0).astype(F32)
        lane = lax.broadcasted_iota(jnp.int32, (nbp, LANE), 1)
        passed = jnp.where((lane < n_experts) & (pend0 <= blk), 1.0, 0.0)
        blk_e = jnp.minimum(jnp.sum(passed, axis=-1, keepdims=True), float(n_experts - 1))
        mine = lane.astype(F32) == blk_e
        e_count = jnp.sum(jnp.where(mine, cnt_ref[0:1, :], 0.0), axis=-1, keepdims=True)
        e_first = jnp.sum(jnp.where(mine, pend0 - nblk[0:1, :], 0.0), axis=-1, keepdims=True)
        n_valid = jnp.clip(e_count - (blk[:, 0:1] - e_first) * float(bm), 0.0, float(bm))
        meta = jnp.where(lane == 0, blk_e, jnp.where(lane == 1, n_used, jnp.where(lane == 2, n_valid, 0.0)))
        meta_ref[...] = meta.astype(jnp.int32)

    lane = lax.broadcasted_iota(jnp.int32, (tm, LANE), 1)
    idx = idx_ref[...]
    hots = []
    sel = jnp.zeros((tm, LANE), F32)
    for r in range(TOP_K):
        col = jnp.sum(jnp.where(lane == r, idx, 0).astype(F32), axis=-1, keepdims=True)
        hot = jnp.where(lane.astype(F32) == col, 1.0, 0.0)
        hots.append(hot)
        sel = sel + hot
    rank = jnp.dot(ltri_ref[...], sel.astype(BF16), preferred_element_type=F32) + carry_ref[0:1, :]
    pos = pst_ref[0:1, :] + rank
    dest = jnp.zeros((tm, LANE), F32)
    for r in range(TOP_K):
        d = jnp.sum(hots[r] * pos, axis=-1, keepdims=True)
        dest = jnp.where(lane == r, d, dest)
    dest_ref[...] = dest.T[0:8, :].astype(jnp.int32)
    carry_ref[0:1, :] += jnp.sum(sel, axis=0, keepdims=True)


def _positions(idx, cnt, n_experts, n_blocks):
    t = idx.shape[0]
    tm = TM_POS
    nbp = -(-n_blocks // 8) * 8
    return pl.pallas_call(
        functools.partial(_positions_kernel, n_experts=n_experts, bm=BM_EXP),
        grid=(t // tm,),
        in_specs=[pl.BlockSpec((tm, LANE), lambda i: (i, 0)),
                  pl.BlockSpec((8, LANE), lambda i: (0, 0))],
        out_specs=(pl.BlockSpec((8, tm), lambda i: (0, i)),
                   pl.BlockSpec((nbp, LANE), lambda i: (0, 0)),
                   pl.BlockSpec((8, LANE), lambda i: (0, 0))),
        out_shape=(jax.ShapeDtypeStruct((8, t), jnp.int32),
                   jax.ShapeDtypeStruct((nbp, LANE), jnp.int32),
                   jax.ShapeDtypeStruct((8, LANE), jnp.int32)),
        scratch_shapes=[pltpu.VMEM((8, LANE), F32), pltpu.VMEM((tm, tm), BF16), pltpu.VMEM((8, LANE), F32)],
        compiler_params=_params(("arbitrary",)),
        name="dispatch_positions",
    )(idx, cnt)


def _dispatch_kernel(zst_ref, dest_ref, h2_ref, xs_ref, zero_ref, zsem, sem, *, n_experts, bm):
    i = pl.program_id(0)
    tm = dest_ref.shape[1]

    def zero_copy(e):
        return pltpu.make_async_copy(zero_ref, xs_ref.at[pl.ds(zst_ref[e], bm)], zsem)

    def tail_copy(j):
        return pltpu.make_async_copy(zero_ref, xs_ref.at[pl.ds(j * bm, bm)], zsem)

    @pl.when(i == 0)
    def _():
        zero_ref[...] = jnp.zeros_like(zero_ref)
        n_used = zst_ref[n_experts]
        n_blocks = xs_ref.shape[0] // bm
        for e in range(n_experts):
            zero_copy(e).start()
        lax.fori_loop(n_used, n_blocks, lambda j, c: (tail_copy(j).start(), c)[1], 0)
        for e in range(n_experts):
            zero_copy(e).wait()
        lax.fori_loop(n_used, n_blocks, lambda j, c: (tail_copy(j).wait(), c)[1], 0)

    def body(j, carry):
        for r in range(TOP_K):
            d = dest_ref[r, j]
            pltpu.make_async_copy(h2_ref.at[j], xs_ref.at[d], sem).start(priority=r % 2)
        return carry

    lax.fori_loop(0, tm, body, 0, unroll=8)
    for _ in range(TOP_K):
        pltpu.make_async_copy(h2_ref, xs_ref.at[pl.ds(0, tm)], sem).wait()


def _dispatch(zst, dest, h2, n_experts, n_blocks):
    t, _, d = h2.shape
    grid_spec = pltpu.PrefetchScalarGridSpec(
        num_scalar_prefetch=1,
        grid=(t // TM_DISPATCH,),
        in_specs=[pl.BlockSpec((8, TM_DISPATCH), lambda i, z: (0, i), memory_space=pltpu.SMEM),
                  pl.BlockSpec((TM_DISPATCH, 1, d), lambda i, z: (i, 0, 0))],
        out_specs=pl.BlockSpec(memory_space=pl.ANY),
        scratch_shapes=[pltpu.VMEM((BM_EXP, 1, d), h2.dtype),
                        pltpu.SemaphoreType.DMA(()), pltpu.SemaphoreType.DMA(())],
    )
    return pl.pallas_call(
        functools.partial(_dispatch_kernel, n_experts=n_experts, bm=BM_EXP),
        grid_spec=grid_spec,
        out_shape=jax.ShapeDtypeStruct((n_blocks * BM_EXP, 1, d), h2.dtype),
        compiler_params=_params(("arbitrary",)),
        name="row_dispatch",
    )(zst, dest, h2)


def _expert_kernel(sp_ref, xs_ref, wg_ref, wl_ref, bg_ref, bl_ref, wd_ref, bd_ref, ys_ref,
                   x2d_ref, xb_ref, acc2_ref, in_sems, sems, *, n_blocks):
    j = pl.program_id(0)
    f = pl.program_id(1)
    nf = pl.num_programs(1)
    n_used = sp_ref[n_blocks]
    used = j < n_used
    bm = acc2_ref.shape[1]
    slot = j % 2
    acc_ref = acc2_ref.at[slot]

    def in_copy(blk, s):
        src = jnp.minimum(blk, n_used - 1)
        return pltpu.make_async_copy(xs_ref.at[pl.ds(src * bm, bm), 0, :], x2d_ref.at[s], in_sems.at[s])

    def out_copy(blk, s):
        return pltpu.make_async_copy(acc2_ref.at[s], ys_ref.at[pl.ds(blk * bm, bm), 0, :], sems.at[s])

    @pl.when((f == 0) & (j == 0))
    def _():
        in_copy(0, 0).start()

    @pl.when((f == 0) & (j + 1 < n_blocks))
    def _():
        in_copy(j + 1, 1 - slot).start()

    @pl.when(f == 0)
    def _():
        in_copy(j, slot).wait()

    @pl.when((f == 0) & (j >= 2))
    def _():
        out_copy(j - 2, slot).wait()

    @pl.when(used & (f == 0))
    def _():
        words = x2d_ref[slot]
        half = words.shape[1]
        xb_ref[:, :half] = pltpu.bitcast(words << 16, F32).astype(BF16)
        xb_ref[:, half:] = pltpu.bitcast(words & jnp.uint32(0xFFFF0000), F32).astype(BF16)

    @pl.when(jnp.logical_not(used) & (f == 0))
    def _():
        acc_ref[...] = jnp.zeros((bm, acc2_ref.shape[2]), F32)

    def mlp(rows):
        xb = xb_ref[0:rows]
        glu = jnp.dot(xb, wg_ref[0], preferred_element_type=F32) + bg_ref[0]
        lin = jnp.dot(xb, wl_ref[0], preferred_element_type=F32) + bl_ref[0]
        glu = jnp.minimum(glu, SWIGLU_LIMIT)
        lin = jnp.clip(lin, -SWIGLU_LIMIT, SWIGLU_LIMIT)
        act = glu * jax.nn.sigmoid(SWIGLU_ALPHA * glu) * (lin + 1.0)
        prev = jnp.where(f == 0, jnp.broadcast_to(bd_ref[0], (rows, acc2_ref.shape[2])), acc_ref[0:rows])
        acc_ref[0:rows] = prev + jnp.dot(act.astype(BF16), wd_ref[0], preferred_element_type=F32)

    sparse = sp_ref[n_blocks + 1 + jnp.minimum(j, n_blocks - 1)] <= bm // 2

    @pl.when(used & jnp.logical_not(sparse))
    def _():
        mlp(bm)

    @pl.when(used & sparse)
    def _():
        mlp(bm // 2)
        acc_ref[bm // 2:bm] = jnp.zeros((bm - bm // 2, acc2_ref.shape[2]), F32)

    @pl.when(f == nf - 1)
    def _():
        out_copy(j, slot).start()

    @pl.when((f == nf - 1) & (j == n_blocks - 1))
    def _():
        out_copy(j - 1, 1 - slot).wait()
        out_copy(j, slot).wait()


def _experts(sp, xs, w_gu, b_gu, w_dn, b_dn, n_blocks):
    p, _, dh = xs.shape
    n_experts, d, ff2 = w_gu.shape
    assert d == 2 * dh
    ff = ff2 // 2
    nf = ff // TF_EXP
    bm = BM_EXP

    def blk(j, s):
        return jnp.minimum(j, s[n_blocks] - 1)

    def fch(j, f, s):
        return jnp.where(j < s[n_blocks], f, nf - 1)

    grid_spec = pltpu.PrefetchScalarGridSpec(
        num_scalar_prefetch=1,
        grid=(n_blocks, nf),
        in_specs=[
            pl.BlockSpec(memory_space=pl.ANY),
            pl.BlockSpec((1, d, TF_EXP), lambda j, f, s: (s[blk(j, s)], 0, fch(j, f, s))),
            pl.BlockSpec((1, d, TF_EXP), lambda j, f, s: (s[blk(j, s)], 0, nf + fch(j, f, s))),
            pl.BlockSpec((1, 1, TF_EXP), lambda j, f, s: (s[blk(j, s)], 0, fch(j, f, s))),
            pl.BlockSpec((1, 1, TF_EXP), lambda j, f, s: (s[blk(j, s)], 0, nf + fch(j, f, s))),
            pl.BlockSpec((1, TF_EXP, d), lambda j, f, s: (s[blk(j, s)], fch(j, f, s), 0)),
            pl.BlockSpec((1, 1, d), lambda j, f, s: (s[blk(j, s)], 0, 0)),
        ],
        out_specs=pl.BlockSpec(memory_space=pl.ANY),
        scratch_shapes=[pltpu.VMEM((2, bm, dh), jnp.uint32), pltpu.VMEM((bm, d), BF16),
                        pltpu.VMEM((2, bm, d), F32),
                        pltpu.SemaphoreType.DMA((2,)), pltpu.SemaphoreType.DMA((2,))],
    )
    return pl.pallas_call(
        functools.partial(_expert_kernel, n_blocks=n_blocks),
        grid_spec=grid_spec,
        out_shape=jax.ShapeDtypeStruct((p, 1, d), F32),
        compiler_params=_params(("arbitrary", "arbitrary")),
        name="expert_mlp",
    )(sp, xs, w_gu, w_gu, b_gu.reshape(n_experts, 1, ff2), b_gu.reshape(n_experts, 1, ff2),
      w_dn, b_dn.reshape(n_experts, 1, d))


def _combine_kernel(dest_ref, dest_next_ref, gate_ref, x1_ref, mod_ref, g_ref, ys_ref, oa_ref, ob_ref,
                    buf_ref, row_ref, sems, *, first_tiles):
    tm = x1_ref.shape[0]
    i = pl.program_id(0)
    slot = i % 2

    def gather(table_ref, s):
        def body(j, carry):
            for r in range(TOP_K):
                d = table_ref[r, j]
                pltpu.make_async_copy(ys_ref.at[d], buf_ref.at[s, r, j], sems.at[s]).start(priority=r % 2)
            return carry

        lax.fori_loop(0, tm, body, 0, unroll=8)

    @pl.when(i == 0)
    def _():
        gather(dest_ref, 0)

    @pl.when(i + 1 < pl.num_programs(0))
    def _():
        gather(dest_next_ref, 1 - slot)

    for r in range(TOP_K):
        pltpu.make_async_copy(ys_ref.at[pl.ds(0, tm)], buf_ref.at[slot, r], sems.at[slot]).wait()

    gates = gate_ref[...]
    lane = lax.broadcasted_iota(jnp.int32, gates.shape, 1)
    f = None
    for r in range(TOP_K):
        g = jnp.sum(jnp.where(lane == r, gates, 0.0), axis=-1, keepdims=True)
        row_ref[...] = buf_ref[slot, r].reshape(row_ref.shape)
        term = row_ref[...] * g
        f = term if f is None else f + term
    out = x1_ref[...] + mod_ref[0, 5:6, :] * _rms(f, g_ref[...])

    @pl.when(pl.program_id(0) < first_tiles)
    def _():
        oa_ref[...] = out

    @pl.when(pl.program_id(0) >= first_tiles)
    def _():
        ob_ref[...] = out


def _combine(dest, gates, x1, mod3, g_post, ys, seq, n_first):
    t, d = x1.shape
    tm = TM_DMA
    tps = seq // tm
    first_tiles = n_first * tps
    n_tiles = t // tm
    return pl.pallas_call(
        functools.partial(_combine_kernel, first_tiles=first_tiles),
        grid=(n_tiles,),
        in_specs=[
            pl.BlockSpec((8, tm), lambda i: (0, i), memory_space=pltpu.SMEM),
            pl.BlockSpec((8, tm), lambda i: (0, jnp.minimum(i + 1, n_tiles - 1)), memory_space=pltpu.SMEM),
            pl.BlockSpec((tm, LANE), lambda i: (i, 0)),
            pl.BlockSpec((tm, d), lambda i: (i, 0)),
            pl.BlockSpec((1, N_MOD, d), lambda i: (i // tps, 0, 0)),
            pl.BlockSpec((1, d), lambda i: (0, 0)),
            pl.BlockSpec(memory_space=pl.ANY),
        ],
        out_specs=(pl.BlockSpec((tm, d), lambda i: (jnp.minimum(i, first_tiles - 1), 0)),
                   pl.BlockSpec((tm, d), lambda i: (jnp.maximum(i - first_tiles, 0), 0))),
        out_shape=(jax.ShapeDtypeStruct((first_tiles * tm, d), F32),
                   jax.ShapeDtypeStruct((t - first_tiles * tm, d), F32)),
        scratch_shapes=[pltpu.VMEM((2, TOP_K, tm, 1, d), F32), pltpu.VMEM((tm, d), F32),
                        pltpu.SemaphoreType.DMA((2,))],
        compiler_params=_params(("arbitrary",)),
        name="expert_combine",
    )(dest, dest, gates, x1, mod3, g_post.reshape(1, d), ys)


def _w_in_plan(d, qr, kvr):
    qsw = SWA_Q_HEADS * SWA_HEAD_DIM
    ksw = SWA_KV_HEADS * SWA_HEAD_DIM
    names = ("c_q", "c_kv", "k_rope", "q_s", "k_s", "v_s", "g_a", "g_b")
    sizes = (qr, kvr, MLA_ROPE, qsw, ksw, ksw, d, d)
    src = {}
    off = 0
    for name, s in zip(names, sizes):
        src[name] = (off, s)
        off += s
    order = ("q_s", "g_a", "g_b", "c_q", "c_kv", "k_s", "v_s", "k_rope", "k_rope")
    plan, cols, dst = [], {}, 0
    for name in order:
        s_off, width = src[name]
        cols.setdefault(name, dst)
        plan.append((s_off, width, dst))
        dst += width
    return tuple(plan), cols, dst


def _layout_w_uq(w_uq):
    r = w_uq.shape[0]
    w = w_uq.reshape(r, MLA_HEADS, MLA_NOPE + MLA_ROPE)
    nope = w[:, :, :MLA_NOPE].reshape(r, MLA_HEADS * MLA_NOPE)
    pe = w[:, :, MLA_NOPE:].reshape(r, MLA_HEADS * MLA_ROPE)
    return jnp.concatenate([nope, pe], axis=1).astype(BF16)


def _layout_w_ukv(w_ukv):
    r = w_ukv.shape[0]
    w = w_ukv.reshape(r, MLA_HEADS, MLA_NOPE + MLA_V)
    kn = w[:, :, :MLA_NOPE].reshape(r, MLA_HEADS * MLA_NOPE)
    v = w[:, :, MLA_NOPE:].reshape(r, MLA_HEADS * MLA_V)
    return jnp.concatenate([kn, v], axis=1).astype(BF16)


def _rope_tables(seq):
    half = MLA_ROPE // 2
    freqs = jnp.power(ROPE_THETA, -2.0 * jnp.arange(half, dtype=F32) / MLA_ROPE)
    ang = jnp.arange(seq, dtype=F32)[:, None] * freqs[None, :]
    cos, sin = jnp.cos(ang), jnp.sin(ang)
    return (jnp.concatenate([cos, cos, cos, cos], axis=1),
            jnp.concatenate([-sin, sin, -sin, sin], axis=1))


def kernel(x_prompt, x_sample, c_prompt, c_sample, w_ada, b_ada, g_pre_mix, w_in, g_q_lat, w_uq,
           g_kv_lat, w_ukv, attn_sinks, w_br_mla, w_br_swa, w_out, g_post_mix, g_pre_ffn,
           w_router, b_router, w_gu, b_gu, w_dn, b_dn, g_post_ffn):
    assert MLA_ROPE == SWA_HEAD_DIM == 64 and MLA_NOPE == MLA_V == LANE
    nb_p, seq, d = x_prompt.shape
    nb_s = x_sample.shape[0]
    assert x_sample.shape[1] == seq
    nseq = nb_p + nb_s
    t = nseq * seq
    n_experts = w_router.shape[-1]
    n_blocks = t * TOP_K // BM_EXP + n_experts

    xa, xb = x_prompt.reshape(nb_p * seq, d), x_sample.reshape(nb_s * seq, d)
    c8 = jnp.concatenate([c_prompt, c_sample, jnp.zeros((8 - nseq, d), F32)], axis=0)
    cos, sin = _rope_tables(seq)

    for l in range(w_ada.shape[0]):
        plan, cols, n_cols = _w_in_plan(d, g_q_lat.shape[-1], g_kv_lat.shape[-1])
        w_uq_p = _layout_w_uq(w_uq[l])
        w_ukv_p = _layout_w_ukv(w_ukv[l])
        w_router_p = jnp.pad(w_router[l], ((0, 0), (0, LANE - n_experts))).astype(BF16)
        b_router_p = jnp.pad(b_router[l], (0, LANE - n_experts)).reshape(1, LANE)

        mod, w_in_p = _modulation(c8, w_ada[l], b_ada[l], w_in[l], plan, n_cols)
        mod3 = mod[:nseq].reshape(nseq, N_MOD, d)
        z = _in_projection(xa, xb, mod3, g_pre_mix[l], w_in_p, seq)
        q, k, v, qs, klo, khi, vlo, vhi = _post_projection(
            z, cols, cos, sin, g_q_lat[l], g_kv_lat[l], w_uq_p, w_ukv_p, nseq, seq)
        o_a, w_gu_b, w_dn_b = _mla_attention(q, k, v, w_gu[l], w_dn[l])
        o_b = _swa_attention(attn_sinks[l], qs, klo, khi, vlo, vhi, nseq, seq)
        y = _merge(o_a, o_b, z, cols, w_br_mla[l].astype(BF16), w_br_swa[l].astype(BF16))
        x1, h2, idx, gates, cnt = _out_projection(
            y, xa, xb, mod3, g_post_mix[l], g_pre_ffn[l], w_out[l].astype(BF16), w_router_p, b_router_p,
            n_experts, seq)
        dest, meta, zst = _positions(idx, cnt, n_experts, n_blocks)
        sp = jnp.concatenate([meta[:n_blocks, 0], meta[0:1, 1], meta[:n_blocks, 2]])
        zmeta = jnp.concatenate([zst[0, :n_experts], meta[0:1, 1]])
        xs = _dispatch(zmeta, dest, h2, n_experts, n_blocks)
        ys = _experts(sp, xs, w_gu_b, b_gu[l], w_dn_b, b_dn[l], n_blocks)
        xa, xb = _combine(dest, gates, x1, mod3, g_post_ffn[l], ys, seq, nb_p)

    return (xa.reshape(nb_p, seq, d), xb.reshape(nb_s, seq, d))
```

```python
import functools

import jax
import jax.numpy as jnp
from jax import lax
from jax.experimental import pallas as pl
from jax.experimental.pallas import tpu as pltpu

MLA_HEADS = 16
MLA_NOPE = 128
MLA_ROPE = 64
MLA_V = 128
SWA_Q_HEADS = 32
SWA_KV_HEADS = 8
SWA_HEAD_DIM = 64
WINDOW = 128
TOP_K = 4
SWIGLU_LIMIT = 7.0
SWIGLU_ALPHA = 1.702
ROPE_THETA = 10000.0
RMS_EPS = 1e-6
N_MOD = 6

LANE = 128
VMEM_LIMIT = 56 << 20

TM_IN = 512
TN_IN = 1664
TM_POST = 256
TQ_MLA = 1024
KV_CHUNKS_MLA = (1024, 1024, 1024, 1024)
TQ_SWA = 512
SUB_SWA = 256
TM_MERGE = 1024
TN_MERGE = 1024
TM_OUT = 512
OUT_ROW_SPLITS = 2
TM_POS = 512
TM_DISPATCH = 512
TM_DMA = 256
BM_EXP = 512
TF_EXP = 1024

F32 = jnp.float32
BF16 = jnp.bfloat16
NEG_INF = float("-inf")
LOG2E = 1.4426950408889634


def _params(sem):
    return pltpu.CompilerParams(dimension_semantics=sem, vmem_limit_bytes=VMEM_LIMIT)


def _rms(x, g):
    return x * lax.rsqrt(jnp.mean(x * x, axis=-1, keepdims=True) + RMS_EPS) * g


def _mod_kernel(c_ref, w_ref, b_ref, win_ref, o_ref, wout_ref, *, plan, n_slabs):
    c = c_ref[...]
    a = (c * jax.nn.sigmoid(c)).astype(BF16)
    o_ref[...] = jnp.dot(a, w_ref[...].astype(BF16), preferred_element_type=F32) + b_ref[...]

    @pl.when(pl.program_id(0) < n_slabs)
    def _():
        for src, width, dst in plan:
            wout_ref[:, dst:dst + width] = win_ref[0, :, src:src + width].astype(BF16)


def _modulation(c8, w_ada, b_ada, w_in, layer, plan, n_cols):
    d, n = w_ada.shape
    tn = 1024
    n_steps = n // tn
    n_slabs = 8
    assert n_slabs <= n_steps and d % (8 * n_slabs) == 0
    rows = d // n_slabs
    slab = lambda j: (jnp.minimum(j, n_slabs - 1), 0)
    return pl.pallas_call(
        functools.partial(_mod_kernel, plan=plan, n_slabs=n_slabs),
        grid=(n_steps,),
        in_specs=[
            pl.BlockSpec((8, d), lambda j: (0, 0)),
            pl.BlockSpec((d, tn), lambda j: (0, j)),
            pl.BlockSpec((1, tn), lambda j: (0, j)),
            pl.BlockSpec((1, rows, w_in.shape[2]), lambda j: (layer, jnp.minimum(j, n_slabs - 1), 0)),
        ],
        out_specs=(pl.BlockSpec((8, tn), lambda j: (0, j)),
                   pl.BlockSpec((rows, n_cols), slab)),
        out_shape=(jax.ShapeDtypeStruct((8, n), F32),
                   jax.ShapeDtypeStruct((d, n_cols), BF16)),
        compiler_params=_params(("arbitrary",)),
        name="adaln_mod",
    )(c8, w_ada, b_ada.reshape(1, n), w_in)


def _two_group_specs(tm, d, first_tiles, n_grid_axes):
    if n_grid_axes == 1:
        return (pl.BlockSpec((tm, d), lambda i: (jnp.minimum(i, first_tiles - 1), 0)),
                pl.BlockSpec((tm, d), lambda i: (jnp.maximum(i - first_tiles, 0), 0)))
    return (pl.BlockSpec((tm, d), lambda i, j: (jnp.minimum(i, first_tiles - 1), 0)),
            pl.BlockSpec((tm, d), lambda i, j: (jnp.maximum(i - first_tiles, 0), 0)))


def _inproj_kernel(xa_ref, xb_ref, mod_ref, g_ref, w_ref, z_ref, h_ref, *, first_tiles):
    def normalise(x_ref):
        x = x_ref[...]
        gain = g_ref[...] * (1.0 + mod_ref[0, 1:2, :])
        h = x * lax.rsqrt(jnp.mean(x * x, axis=-1, keepdims=True) + RMS_EPS) * gain + mod_ref[0, 0:1, :]
        h_ref[...] = h.astype(BF16)

    first_col = pl.program_id(1) == 0
    in_first = pl.program_id(0) < first_tiles
    pl.when(first_col & in_first)(functools.partial(normalise, xa_ref))
    pl.when(first_col & jnp.logical_not(in_first))(functools.partial(normalise, xb_ref))

    z_ref[...] = jnp.dot(h_ref[...], w_ref[...], preferred_element_type=F32).astype(z_ref.dtype)


def _in_projection(xa, xb, mod3, g, w_in_p, seq):
    d = xa.shape[1]
    t = xa.shape[0] + xb.shape[0]
    n = w_in_p.shape[1]
    tiles_per_seq = seq // TM_IN
    first_tiles = xa.shape[0] // TM_IN
    return pl.pallas_call(
        functools.partial(_inproj_kernel, first_tiles=first_tiles),
        grid=(t // TM_IN, n // TN_IN),
        in_specs=[
            *_two_group_specs(TM_IN, d, first_tiles, 2),
            pl.BlockSpec((1, N_MOD, d), lambda i, j: (i // tiles_per_seq, 0, 0)),
            pl.BlockSpec((1, d), lambda i, j: (0, 0)),
            pl.BlockSpec((d, TN_IN), lambda i, j: (0, j)),
        ],
        out_specs=pl.BlockSpec((TM_IN, TN_IN), lambda i, j: (i, j)),
        out_shape=jax.ShapeDtypeStruct((t, n), BF16),
        scratch_shapes=[pltpu.VMEM((TM_IN, d), BF16)],
        compiler_params=_params(("arbitrary", "arbitrary")),
        name="in_projection",
    )(xa, xb, mod3, g.reshape(1, d), w_in_p)


def _postproj_kernel(cq_ref, ckv_ref, qs_ref, ks_ref, vs_ref, kr_ref, cos_ref, sin_ref,
                     gq_ref, gkv_ref, wuq_ref, wukv_ref,
                     q_ref, k_ref, v_ref, qso_ref, klo_ref, khi_ref, vlo_ref, vhi_ref):
    tm = cq_ref.shape[0]
    cos = cos_ref[...]
    sin = sin_ref[...]
    lane = lax.broadcasted_iota(jnp.int32, (tm, LANE), 1)
    first_half = (lane & 63) < 32
    low = lane < 64

    def rope(x, cos=cos, sin=sin):
        rot = jnp.where(first_half, pltpu.roll(x, LANE - 32, 1), pltpu.roll(x, 32, 1))
        return x * cos + rot * sin

    nh = MLA_HEADS
    scale = float((MLA_NOPE + MLA_ROPE) ** -0.5) * LOG2E
    cqn = (_rms(cq_ref[...].astype(F32), gq_ref[...]) * scale).astype(BF16)
    q = jnp.dot(cqn, wuq_ref[...], preferred_element_type=F32)
    for h in range(nh):
        q_ref[0, h, :, 0:LANE] = q[:, h * LANE:(h + 1) * LANE].astype(BF16)
    for m in range(nh // 2):
        pe = rope(q[:, (nh + m) * LANE:(nh + m + 1) * LANE]).astype(BF16)
        q_ref[0, 2 * m, :, LANE:2 * LANE] = pe
        q_ref[0, 2 * m + 1, :, LANE:2 * LANE] = pe

    ckvn = _rms(ckv_ref[...].astype(F32), gkv_ref[...]).astype(BF16)
    kv = jnp.dot(ckvn, wukv_ref[...], preferred_element_type=F32)
    kr = rope(kr_ref[...].astype(F32))
    kpe_lo = jnp.where(low, kr, 0.0).astype(BF16)
    kpe_hi = jnp.where(low, 0.0, kr).astype(BF16)
    for h in range(nh):
        k_ref[0, h, :, 0:LANE] = kv[:, h * LANE:(h + 1) * LANE].astype(BF16)
        k_ref[0, h, :, LANE:2 * LANE] = kpe_lo if h % 2 == 0 else kpe_hi
        v_ref[0, h, :, :] = kv[:, (nh + h) * LANE:(nh + h + 1) * LANE].astype(BF16)

    swa_scale = float(SWA_HEAD_DIM ** -0.5) * LOG2E
    cos_q, sin_q = cos * swa_scale, sin * swa_scale
    for m in range(SWA_Q_HEADS // 2):
        x = qs_ref[:, m * LANE:(m + 1) * LANE].astype(F32)
        qso_ref[:, m * LANE:(m + 1) * LANE] = rope(x, cos_q, sin_q).astype(BF16)

    for m in range(SWA_KV_HEADS // 2):
        sl = slice(m * LANE, (m + 1) * LANE)
        for src_ref, lo_ref, hi_ref, roped in ((ks_ref, klo_ref, khi_ref, True),
                                               (vs_ref, vlo_ref, vhi_ref, False)):
            a = src_ref[:, sl].astype(F32)
            if roped:
                a = rope(a)
            b = pltpu.roll(a, 64, 1)
            e0 = slice((2 * m) * LANE, (2 * m + 1) * LANE)
            e1 = slice((2 * m + 1) * LANE, (2 * m + 2) * LANE)
            lo_ref[:, e0] = jnp.where(low, a, 0.0).astype(BF16)
            hi_ref[:, e0] = jnp.where(low, 0.0, b).astype(BF16)
            lo_ref[:, e1] = jnp.where(low, b, 0.0).astype(BF16)
            hi_ref[:, e1] = jnp.where(low, 0.0, a).astype(BF16)


def _post_projection(z, cols, cos, sin, g_q, g_kv, w_uq_p, w_ukv_p, nseq, seq):
    t = z.shape[0]
    tm = TM_POST
    tps = seq // tm
    nh = MLA_HEADS
    qr, kvr = g_q.shape[0], g_kv.shape[0]
    qsw = SWA_Q_HEADS * SWA_HEAD_DIM
    ksw = SWA_KV_HEADS * SWA_HEAD_DIM

    def zspec(width, off):
        blk = off // width
        return pl.BlockSpec((tm, width), lambda i: (i, blk))

    head_map = lambda i: (i // tps, 0, i % tps, 0)
    tok_map = lambda i: (i, 0)
    out_shapes = (
        jax.ShapeDtypeStruct((nseq, nh, seq, 2 * LANE), BF16),
        jax.ShapeDtypeStruct((nseq, nh, seq, 2 * LANE), BF16),
        jax.ShapeDtypeStruct((nseq, nh, seq, LANE), BF16),
        jax.ShapeDtypeStruct((t, qsw), BF16),
        jax.ShapeDtypeStruct((t, SWA_KV_HEADS * LANE), BF16),
        jax.ShapeDtypeStruct((t, SWA_KV_HEADS * LANE), BF16),
        jax.ShapeDtypeStruct((t, SWA_KV_HEADS * LANE), BF16),
        jax.ShapeDtypeStruct((t, SWA_KV_HEADS * LANE), BF16),
    )
    out_specs = (
        pl.BlockSpec((1, nh, tm, 2 * LANE), head_map),
        pl.BlockSpec((1, nh, tm, 2 * LANE), head_map),
        pl.BlockSpec((1, nh, tm, LANE), head_map),
        pl.BlockSpec((tm, qsw), tok_map),
        pl.BlockSpec((tm, SWA_KV_HEADS * LANE), tok_map),
        pl.BlockSpec((tm, SWA_KV_HEADS * LANE), tok_map),
        pl.BlockSpec((tm, SWA_KV_HEADS * LANE), tok_map),
        pl.BlockSpec((tm, SWA_KV_HEADS * LANE), tok_map),
    )
    return pl.pallas_call(
        _postproj_kernel,
        grid=(t // tm,),
        in_specs=[
            zspec(qr, cols["c_q"]), zspec(kvr, cols["c_kv"]), zspec(qsw, cols["q_s"]),
            zspec(ksw, cols["k_s"]), zspec(ksw, cols["v_s"]), zspec(LANE, cols["k_rope"]),
            pl.BlockSpec((tm, LANE), lambda i: (i % tps, 0)),
            pl.BlockSpec((tm, LANE), lambda i: (i % tps, 0)),
            pl.BlockSpec((1, qr), lambda i: (0, 0)),
            pl.BlockSpec((1, kvr), lambda i: (0, 0)),
            pl.BlockSpec(w_uq_p.shape, lambda i: (0, 0)),
            pl.BlockSpec(w_ukv_p.shape, lambda i: (0, 0)),
        ],
        out_specs=out_specs,
        out_shape=out_shapes,
        compiler_params=_params(("arbitrary",)),
        name="post_projection",
    )(z, z, z, z, z, z, cos, sin, g_q.reshape(1, qr), g_kv.reshape(1, kvr), w_uq_p, w_ukv_p)


def _mla_kernel(q_ref, k_ref, v_ref, wgu_ref, wdn_ref, o_ref, wgu_o_ref, wdn_o_ref, *, n_gu):
    step = (pl.program_id(0) * pl.num_programs(1) + pl.program_id(1)) * pl.num_programs(2) + pl.program_id(2)

    @pl.when(step < n_gu)
    def _():
        wgu_o_ref[...] = wgu_ref[...].astype(BF16)

    @pl.when(step >= n_gu)
    def _():
        wdn_o_ref[...] = wdn_ref[...].astype(BF16)

    q = q_ref[0, 0]
    seq = k_ref.shape[2]
    m = l = acc = None
    assert sum(KV_CHUNKS_MLA) == seq
    start = 0
    for width in KV_CHUNKS_MLA:
        rows = slice(start, start + width)
        start += width
        s = lax.dot_general(q, k_ref[0, 0, rows, :], (((1,), (1,)), ((), ())),
                            preferred_element_type=F32)
        mc = jnp.max(s, axis=-1, keepdims=True)
        m_new = mc if m is None else jnp.maximum(m, mc)
        p = jnp.exp2(s - m_new)
        ps = jnp.sum(p, axis=-1, keepdims=True)
        pv = jnp.dot(p.astype(BF16), v_ref[0, 0, rows, :], preferred_element_type=F32)
        if m is None:
            l, acc = ps, pv
        else:
            alpha = jnp.exp2(m - m_new)
            l = alpha * l + ps
            acc = alpha * acc + pv
        m = m_new
    o_ref[...] = (acc / l).astype(o_ref.dtype)


def _mla_attention(q, k, v, w_gu, w_dn):
    nseq, nh, seq, dqk = q.shape
    dv = v.shape[-1]
    nq = seq // TQ_MLA
    n_steps = nseq * nh * nq
    gu2d = w_gu.reshape(-1, w_gu.shape[-1])
    dn2d = w_dn.reshape(-1, w_dn.shape[-1])
    n_gu, n_dn = 2 * n_steps // 3, n_steps // 3
    assert n_gu + n_dn == n_steps and gu2d.shape[0] % n_gu == 0 and dn2d.shape[0] % n_dn == 0
    gu_rows, dn_rows = gu2d.shape[0] // n_gu, dn2d.shape[0] // n_dn
    assert gu_rows % 16 == 0 and dn_rows % 16 == 0

    def step(b, h, i):
        return (b * nh + h) * nq + i

    gu_map = lambda b, h, i: (jnp.minimum(step(b, h, i), n_gu - 1), 0)
    dn_map = lambda b, h, i: (jnp.maximum(step(b, h, i) - n_gu, 0), 0)
    o, gu_b, dn_b = pl.pallas_call(
        functools.partial(_mla_kernel, n_gu=n_gu),
        grid=(nseq, nh, nq),
        in_specs=[
            pl.BlockSpec((1, 1, TQ_MLA, dqk), lambda b, h, i: (b, h, i, 0)),
            pl.BlockSpec((1, 1, seq, dqk), lambda b, h, i: (b, h, 0, 0)),
            pl.BlockSpec((1, 1, seq, dv), lambda b, h, i: (b, h, 0, 0)),
            pl.BlockSpec((gu_rows, gu2d.shape[1]), gu_map),
            pl.BlockSpec((dn_rows, dn2d.shape[1]), dn_map),
        ],
        out_specs=(
            pl.BlockSpec((TQ_MLA, dv), lambda b, h, i: (b * nq + i, h)),
            pl.BlockSpec((gu_rows, gu2d.shape[1]), gu_map),
            pl.BlockSpec((dn_rows, dn2d.shape[1]), dn_map),
        ),
        out_shape=(
            jax.ShapeDtypeStruct((nseq * seq, nh * dv), BF16),
            jax.ShapeDtypeStruct(gu2d.shape, BF16),
            jax.ShapeDtypeStruct(dn2d.shape, BF16),
        ),
        compiler_params=_params(("arbitrary", "arbitrary", "arbitrary")),
        name="mla_attention",
    )(q, k, v, gu2d, dn2d)
    return o, gu_b.reshape(w_gu.shape), dn_b.reshape(w_dn.shape)


def _swa_kernel(sink_ref, q_ref,
                klo_p, klo_c, klo_n, khi_p, khi_c, khi_n,
                vlo_p, vlo_c, vlo_n, vhi_p, vhi_c, vhi_n,
                o_ref, klo_w, khi_w, vlo_w, vhi_w, *, seq):
    tb = q_ref.shape[0]
    tq = SUB_SWA
    w = WINDOW
    for win, (p, c, n) in ((klo_w, (klo_p, klo_c, klo_n)), (khi_w, (khi_p, khi_c, khi_n)),
                           (vlo_w, (vlo_p, vlo_c, vlo_n)), (vhi_w, (vhi_p, vhi_c, vhi_n))):
        win[0:w, :] = p[...]
        win[w:w + tb, :] = c[...]
        win[w + tb:tb + 2 * w, :] = n[...]

    assert SWA_Q_HEADS // SWA_KV_HEADS == 4 and tq & (tq - 1) == 0 and tb % tq == 0
    nk = tq + 2 * w
    row2 = lax.broadcasted_iota(jnp.int32, (2 * tq, nk), 0)
    col = lax.broadcasted_iota(jnp.int32, (2 * tq, nk), 1)
    rel = col - w - (row2 & (tq - 1))
    band = (rel <= w) & (rel >= -w)
    top = lax.broadcasted_iota(jnp.int32, (2 * tq, 1), 0) < tq

    for sub in range(tb // tq):
        qrows = slice(sub * tq, (sub + 1) * tq)
        krows = slice(sub * tq, sub * tq + nk)
        kpos = pl.program_id(1) * tb + sub * tq - w + col
        valid = band & (kpos >= 0) & (kpos < seq)
        for g in range(SWA_KV_HEADS):
            gs = slice(g * LANE, (g + 1) * LANE)
            pa = slice((2 * g) * LANE, (2 * g + 1) * LANE)
            pb = slice((2 * g + 1) * LANE, (2 * g + 2) * LANE)
            qq = jnp.concatenate([q_ref[qrows, pa], q_ref[qrows, pb]], axis=0)
            acc = None
            for half, (kw, vw) in enumerate(((klo_w, vlo_w), (khi_w, vhi_w))):
                sink = jnp.where(top, sink_ref[4 * g + half], sink_ref[4 * g + 2 + half]) * LOG2E
                s = lax.dot_general(qq, kw[krows, gs], (((1,), (1,)), ((), ())),
                                    preferred_element_type=F32)
                s = jnp.where(valid, s, NEG_INF)
                mx = jnp.maximum(jnp.max(s, axis=-1, keepdims=True), sink)
                e = jnp.exp2(s - mx)
                den = jnp.sum(e, axis=-1, keepdims=True) + jnp.exp2(sink - mx)
                pv = jnp.dot(e.astype(BF16), vw[krows, gs], preferred_element_type=F32) / den
                acc = pv if acc is None else acc + pv
            o_ref[qrows, pa] = acc[0:tq].astype(o_ref.dtype)
            o_ref[qrows, pb] = acc[tq:2 * tq].astype(o_ref.dtype)


def _swa_attention(sinks, qs, klo, khi, vlo, vhi, nseq, seq):
    t, qw = qs.shape
    kw = klo.shape[1]
    tq = TQ_SWA
    nq = seq // tq
    r = tq // WINDOW
    nwb = seq // WINDOW

    prev = pl.BlockSpec((WINDOW, kw), lambda b, i, s: (b * nwb + jnp.maximum(i * r - 1, 0), 0))
    cur = pl.BlockSpec((tq, kw), lambda b, i, s: (b * nq + i, 0))
    nxt = pl.BlockSpec((WINDOW, kw), lambda b, i, s: (b * nwb + jnp.minimum((i + 1) * r, nwb - 1), 0))
    grid_spec = pltpu.PrefetchScalarGridSpec(
        num_scalar_prefetch=1,
        grid=(nseq, nq),
        in_specs=[pl.BlockSpec((tq, qw), lambda b, i, s: (b * nq + i, 0))] + [prev, cur, nxt] * 4,
        out_specs=pl.BlockSpec((tq, qw), lambda b, i, s: (b * nq + i, 0)),
        scratch_shapes=[pltpu.VMEM((tq + 2 * WINDOW, kw), BF16)] * 4,
    )
    return pl.pallas_call(
        functools.partial(_swa_kernel, seq=seq),
        grid_spec=grid_spec,
        out_shape=jax.ShapeDtypeStruct((t, qw), BF16),
        compiler_params=_params(("arbitrary", "arbitrary")),
        name="swa_attention",
    )(sinks, qs, klo, klo, klo, khi, khi, khi, vlo, vlo, vlo, vhi, vhi, vhi)


def _merge_kernel(oa_ref, ob_ref, ga_ref, gb_ref, wa_ref, wb_ref, y_ref):
    a = jnp.dot(oa_ref[...], wa_ref[...], preferred_element_type=F32)
    b = jnp.dot(ob_ref[...], wb_ref[...], preferred_element_type=F32)
    y = jax.nn.sigmoid(ga_ref[...].astype(F32)) * a + jax.nn.sigmoid(gb_ref[...].astype(F32)) * b
    y_ref[...] = y.astype(y_ref.dtype)


def _merge(o_a, o_b, z, cols, w_a, w_b):
    t, d = o_a.shape[0], w_a.shape[1]
    tm, tn = TM_MERGE, TN_MERGE
    ga_blk, gb_blk = cols["g_a"] // tn, cols["g_b"] // tn
    return pl.pallas_call(
        _merge_kernel,
        grid=(t // tm, d // tn),
        in_specs=[
            pl.BlockSpec((tm, o_a.shape[1]), lambda i, j: (i, 0)),
            pl.BlockSpec((tm, o_b.shape[1]), lambda i, j: (i, 0)),
            pl.BlockSpec((tm, tn), lambda i, j: (i, ga_blk + j)),
            pl.BlockSpec((tm, tn), lambda i, j: (i, gb_blk + j)),
            pl.BlockSpec((w_a.shape[0], tn), lambda i, j: (0, j)),
            pl.BlockSpec((w_b.shape[0], tn), lambda i, j: (0, j)),
        ],
        out_specs=pl.BlockSpec((tm, tn), lambda i, j: (i, j)),
        out_shape=jax.ShapeDtypeStruct((t, d), BF16),
        compiler_params=_params(("arbitrary", "arbitrary")),
        name="branch_merge",
    )(o_a, o_b, z, z, w_a, w_b)


def _outproj_kernel(y_ref, xa_ref, xb_ref, mod_ref, gpm_ref, gpf_ref, wo_ref, wr_ref, br_ref,
                    x1_ref, h2_ref, idx_ref, gate_ref, cnt_ref, pack_ref, *, n_experts, first_tiles):
    tm = y_ref.shape[0]
    u_all = jnp.dot(y_ref[...], wo_ref[...], preferred_element_type=F32)
    in_first = pl.program_id(0) < first_tiles
    hm = tm // OUT_ROW_SPLITS
    lane = lax.broadcasted_iota(jnp.int32, (hm, LANE), 1)
    lane_f = lane.astype(F32)
    counts = None
    for part in range(OUT_ROW_SPLITS):
        rows = slice(part * hm, (part + 1) * hm)
        x = jnp.where(in_first, xa_ref[rows, :], xb_ref[rows, :])
        x1 = x + mod_ref[0, 2:3, :] * _rms(u_all[rows], gpm_ref[...])
        x1_ref[rows, :] = x1
        h2 = _rms(x1, gpf_ref[...]) * (1.0 + mod_ref[0, 4:5, :]) + mod_ref[0, 3:4, :]
        h2b = h2.astype(BF16)
        bits = pltpu.bitcast(h2b.astype(F32), jnp.uint32)
        half = bits.shape[1] // 2
        pack_ref[rows, :] = (bits[:, :half] >> 16) | (bits[:, half:] & jnp.uint32(0xFFFF0000))
        h2_ref[rows] = pack_ref[rows, :].reshape(hm, 1, half)

        logits = jnp.dot(h2b, wr_ref[...], preferred_element_type=F32) + br_ref[...]
        cur = jnp.where(lane < n_experts, logits, NEG_INF)
        vals, idxs = [], []
        for _ in range(TOP_K):
            m = jnp.max(cur, axis=-1, keepdims=True)
            ix = jnp.min(jnp.where(cur == m, lane_f, float(LANE)), axis=-1, keepdims=True)
            vals.append(m)
            idxs.append(ix)
            cur = jnp.where(lane_f == ix, NEG_INF, cur)
        es = [jnp.exp(v - vals[0]) for v in vals]
        den = es[0]
        for e in es[1:]:
            den = den + e
        idx_out = jnp.zeros((hm, LANE), F32)
        gate_out = jnp.zeros((hm, LANE), F32)
        sel = jnp.zeros((hm, LANE), F32)
        for r in range(TOP_K):
            idx_out = jnp.where(lane == r, idxs[r], idx_out)
            gate_out = jnp.where(lane == r, es[r] / den, gate_out)
            sel = sel + jnp.where(lane_f == idxs[r], 1.0, 0.0)
        idx_ref[rows, :] = idx_out.astype(jnp.int32)
        gate_ref[rows, :] = gate_out
        part_counts = jnp.sum(sel, axis=0, keepdims=True)
        counts = part_counts if counts is None else counts + part_counts

    @pl.when(pl.program_id(0) == 0)
    def _():
        cnt_ref[...] = jnp.zeros_like(cnt_ref)

    cnt_ref[0:1, :] += counts


def _out_projection(y, xa, xb, mod3, g_pm, g_pf, w_out, w_router_p, b_router_p, n_experts, seq):
    t, d = y.shape
    tm = TM_OUT
    tps = seq // tm
    first_tiles = xa.shape[0] // tm
    tok = lambda i: (i, 0)
    const = lambda i: (0, 0)
    return pl.pallas_call(
        functools.partial(_outproj_kernel, n_experts=n_experts, first_tiles=first_tiles),
        grid=(t // tm,),
        in_specs=[
            pl.BlockSpec((tm, d), tok),
            *_two_group_specs(tm, d, first_tiles, 1),
            pl.BlockSpec((1, N_MOD, d), lambda i: (i // tps, 0, 0)),
            pl.BlockSpec((1, d), const),
            pl.BlockSpec((1, d), const),
            pl.BlockSpec((d, d), const),
            pl.BlockSpec((d, LANE), const),
            pl.BlockSpec((1, LANE), const),
        ],
        out_specs=(
            pl.BlockSpec((tm, d), tok),
            pl.BlockSpec((tm, 1, d // 2), lambda i: (i, 0, 0)),
            pl.BlockSpec((tm, LANE), tok),
            pl.BlockSpec((tm, LANE), tok),
            pl.BlockSpec((8, LANE), const),
        ),
        out_shape=(
            jax.ShapeDtypeStruct((t, d), F32),
            jax.ShapeDtypeStruct((t, 1, d // 2), jnp.uint32),
            jax.ShapeDtypeStruct((t, LANE), jnp.int32),
            jax.ShapeDtypeStruct((t, LANE), F32),
            jax.ShapeDtypeStruct((8, LANE), F32),
        ),
        scratch_shapes=[pltpu.VMEM((tm, d // 2), jnp.uint32)],
        compiler_params=_params(("arbitrary",)),
        name="out_projection_router",
    )(y, xa, xb, mod3, g_pm.reshape(1, d), g_pf.reshape(1, d), w_out, w_router_p, b_router_p)


def _positions_kernel(idx_ref, cnt_ref, dest_ref, meta_ref, zst_ref, carry_ref, ltri_ref, pst_ref,
                      *, n_experts, bm):
    tm = idx_ref.shape[0]
    nbp = meta_ref.shape[0]

    @pl.when(pl.program_id(0) == 0)
    def _():
        r = lax.broadcasted_iota(jnp.int32, (tm, tm), 0)
        c = lax.broadcasted_iota(jnp.int32, (tm, tm), 1)
        ltri_ref[...] = jnp.where(c < r, 1.0, 0.0).astype(BF16)
        carry_ref[...] = jnp.zeros_like(carry_ref)
        nblk = jnp.floor((cnt_ref[...] + float(bm - 1)) * (1.0 / bm))
        ur = lax.broadcasted_iota(jnp.int32, (LANE, LANE), 0)
        uc = lax.broadcasted_iota(jnp.int32, (LANE, LANE), 1)
        upper = jnp.where(ur <= uc, 1.0, 0.0).astype(BF16)
        pend = jnp.dot(nblk.astype(BF16), upper, preferred_element_type=F32)
        pst_ref[...] = (pend - nblk) * float(bm)
        lane8 = lax.broadcasted_iota(jnp.int32, (8, LANE), 1)
        zst_ref[...] = jnp.maximum(pend * float(bm) - float(bm), 0.0).astype(jnp.int32)
        pend0 = pend[0:1, :]
        n_used = jnp.sum(jnp.where(lane8[0:1, :] == n_experts - 1, pend0, 0.0), axis=-1, keepdims=True)
        blk = lax.broadcasted_iota(jnp.int32, (nbp, LANE), 0).astype(F32)
        lane = lax.broadcasted_iota(jnp.int32, (nbp, LANE), 1)
        passed = jnp.where((lane < n_experts) & (pend0 <= blk), 1.0, 0.0)
        blk_e = jnp.minimum(jnp.sum(passed, axis=-1, keepdims=True), float(n_experts - 1))
        mine = lane.astype(F32) == blk_e
        e_count = jnp.sum(jnp.where(mine, cnt_ref[0:1, :], 0.0), axis=-1, keepdims=True)
        e_first = jnp.sum(jnp.where(mine, pend0 - nblk[0:1, :], 0.0), axis=-1, keepdims=True)
        n_valid = jnp.clip(e_count - (blk[:, 0:1] - e_first) * float(bm), 0.0, float(bm))
        meta = jnp.where(lane == 0, blk_e, jnp.where(lane == 1, n_used, jnp.where(lane == 2, n_valid, 0.0)))
        meta_ref[...] = meta.astype(jnp.int32)

    lane = lax.broadcasted_iota(jnp.int32, (tm, LANE), 1)
    idx = idx_ref[...]
    hots = []
    sel = jnp.zeros((tm, LANE), F32)
    for r in range(TOP_K):
        col = jnp.sum(jnp.where(lane == r, idx, 0).astype(F32), axis=-1, keepdims=True)
        hot = jnp.where(lane.astype(F32) == col, 1.0, 0.0)
        hots.append(hot)
        sel = sel + hot
    rank = jnp.dot(ltri_ref[...], sel.astype(BF16), preferred_element_type=F32) + carry_ref[0:1, :]
    pos = pst_ref[0:1, :] + rank
    dest = jnp.zeros((tm, LANE), F32)
    for r in range(TOP_K):
        d = jnp.sum(hots[r] * pos, axis=-1, keepdims=True)
        dest = jnp.where(lane == r, d, dest)
    dest_ref[...] = dest.T[0:8, :].astype(jnp.int32)
    carry_ref[0:1, :] += jnp.sum(sel, axis=0, keepdims=True)


def _positions(idx, cnt, n_experts, n_blocks):
    t = idx.shape[0]
    tm = TM_POS
    nbp = -(-n_blocks // 8) * 8
    return pl.pallas_call(
        functools.partial(_positions_kernel, n_experts=n_experts, bm=BM_EXP),
        grid=(t // tm,),
        in_specs=[pl.BlockSpec((tm, LANE), lambda i: (i, 0)),
                  pl.BlockSpec((8, LANE), lambda i: (0, 0))],
        out_specs=(pl.BlockSpec((8, tm), lambda i: (0, i)),
                   pl.BlockSpec((nbp, LANE), lambda i: (0, 0)),
                   pl.BlockSpec((8, LANE), lambda i: (0, 0))),
        out_shape=(jax.ShapeDtypeStruct((8, t), jnp.int32),
                   jax.ShapeDtypeStruct((nbp, LANE), jnp.int32),
                   jax.ShapeDtypeStruct((8, LANE), jnp.int32)),
        scratch_shapes=[pltpu.VMEM((8, LANE), F32), pltpu.VMEM((tm, tm), BF16), pltpu.VMEM((8, LANE), F32)],
        compiler_params=_params(("arbitrary",)),
        name="dispatch_positions",
    )(idx, cnt)


def _dispatch_kernel(zst_ref, dest_ref, h2_ref, xs_ref, zero_ref, zsem, sem, *, n_experts, bm):
    i = pl.program_id(0)
    tm = dest_ref.shape[1]

    def zero_copy(e):
        return pltpu.make_async_copy(zero_ref, xs_ref.at[pl.ds(zst_ref[e], bm)], zsem)

    def tail_copy(j):
        return pltpu.make_async_copy(zero_ref, xs_ref.at[pl.ds(j * bm, bm)], zsem)

    @pl.when(i == 0)
    def _():
        zero_ref[...] = jnp.zeros_like(zero_ref)
        n_used = zst_ref[n_experts]
        n_blocks = xs_ref.shape[0] // bm
        for e in range(n_experts):
            zero_copy(e).start()
        lax.fori_loop(n_used, n_blocks, lambda j, c: (tail_copy(j).start(), c)[1], 0)
        for e in range(n_experts):
            zero_copy(e).wait()
        lax.fori_loop(n_used, n_blocks, lambda j, c: (tail_copy(j).wait(), c)[1], 0)

    def body(j, carry):
        for r in range(TOP_K):
            d = dest_ref[r, j]
            pltpu.make_async_copy(h2_ref.at[j], xs_ref.at[d], sem).start(priority=r % 2)
        return carry

    lax.fori_loop(0, tm, body, 0, unroll=8)
    for _ in range(TOP_K):
        pltpu.make_async_copy(h2_ref, xs_ref.at[pl.ds(0, tm)], sem).wait()


def _dispatch(zst, dest, h2, n_experts, n_blocks):
    t, _, d = h2.shape
    grid_spec = pltpu.PrefetchScalarGridSpec(
        num_scalar_prefetch=1,
        grid=(t // TM_DISPATCH,),
        in_specs=[pl.BlockSpec((8, TM_DISPATCH), lambda i, z: (0, i), memory_space=pltpu.SMEM),
                  pl.BlockSpec((TM_DISPATCH, 1, d), lambda i, z: (i, 0, 0))],
        out_specs=pl.BlockSpec(memory_space=pl.ANY),
        scratch_shapes=[pltpu.VMEM((BM_EXP, 1, d), h2.dtype),
                        pltpu.SemaphoreType.DMA(()), pltpu.SemaphoreType.DMA(())],
    )
    return pl.pallas_call(
        functools.partial(_dispatch_kernel, n_experts=n_experts, bm=BM_EXP),
        grid_spec=grid_spec,
        out_shape=jax.ShapeDtypeStruct((n_blocks * BM_EXP, 1, d), h2.dtype),
        compiler_params=_params(("arbitrary",)),
        name="row_dispatch",
    )(zst, dest, h2)


def _expert_kernel(sp_ref, xs_ref, wg_ref, wl_ref, bg_ref, bl_ref, wd_ref, bd_ref, ys_ref,
                   x2d_ref, xb_ref, acc2_ref, in_sems, sems, *, n_blocks):
    j = pl.program_id(0)
    f = pl.program_id(1)
    nf = pl.num_programs(1)
    n_used = sp_ref[n_blocks]
    used = j < n_used
    bm = acc2_ref.shape[1]
    slot = j % 2
    acc_ref = acc2_ref.at[slot]

    def in_copy(blk, s):
        src = jnp.minimum(blk, n_used - 1)
        return pltpu.make_async_copy(xs_ref.at[pl.ds(src * bm, bm), 0, :], x2d_ref.at[s], in_sems.at[s])

    def out_copy(blk, s):
        return pltpu.make_async_copy(acc2_ref.at[s], ys_ref.at[pl.ds(blk * bm, bm), 0, :], sems.at[s])

    @pl.when((f == 0) & (j == 0))
    def _():
        in_copy(0, 0).start()

    @pl.when((f == 0) & (j + 1 < n_blocks))
    def _():
        in_copy(j + 1, 1 - slot).start()

    @pl.when(f == 0)
    def _():
        in_copy(j, slot).wait()

    @pl.when((f == 0) & (j >= 2))
    def _():
        out_copy(j - 2, slot).wait()

    @pl.when(used & (f == 0))
    def _():
        words = x2d_ref[slot]
        half = words.shape[1]
        xb_ref[:, :half] = pltpu.bitcast(words << 16, F32).astype(BF16)
        xb_ref[:, half:] = pltpu.bitcast(words & jnp.uint32(0xFFFF0000), F32).astype(BF16)

    @pl.when(jnp.logical_not(used) & (f == 0))
    def _():
        acc_ref[...] = jnp.zeros((bm, acc2_ref.shape[2]), F32)

    def mlp(rows):
        xb = xb_ref[0:rows]
        glu = jnp.dot(xb, wg_ref[0], preferred_element_type=F32) + bg_ref[0]
        lin = jnp.dot(xb, wl_ref[0], preferred_element_type=F32) + bl_ref[0]
        glu = jnp.minimum(glu, SWIGLU_LIMIT)
        lin = jnp.clip(lin, -SWIGLU_LIMIT, SWIGLU_LIMIT)
        act = glu * jax.nn.sigmoid(SWIGLU_ALPHA * glu) * (lin + 1.0)
        prev = jnp.where(f == 0, jnp.broadcast_to(bd_ref[0], (rows, acc2_ref.shape[2])), acc_ref[0:rows])
        acc_ref[0:rows] = prev + jnp.dot(act.astype(BF16), wd_ref[0], preferred_element_type=F32)

    sparse = sp_ref[n_blocks + 1 + jnp.minimum(j, n_blocks - 1)] <= bm // 2

    @pl.when(used & jnp.logical_not(sparse))
    def _():
        mlp(bm)

    @pl.when(used & sparse)
    def _():
        mlp(bm // 2)
        acc_ref[bm // 2:bm] = jnp.zeros((bm - bm // 2, acc2_ref.shape[2]), F32)

    @pl.when(f == nf - 1)
    def _():
        out_copy(j, slot).start()

    @pl.when((f == nf - 1) & (j == n_blocks - 1))
    def _():
        out_copy(j - 1, 1 - slot).wait()
        out_copy(j, slot).wait()


def _experts(sp, xs, w_gu, b_gu, w_dn, b_dn, n_blocks):
    p, _, dh = xs.shape
    n_experts, d, ff2 = w_gu.shape
    assert d == 2 * dh
    ff = ff2 // 2
    nf = ff // TF_EXP
    bm = BM_EXP

    def blk(j, s):
        return jnp.minimum(j, s[n_blocks] - 1)

    def fch(j, f, s):
        return jnp.where(j < s[n_blocks], f, nf - 1)

    grid_spec = pltpu.PrefetchScalarGridSpec(
        num_scalar_prefetch=1,
        grid=(n_blocks, nf),
        in_specs=[
            pl.BlockSpec(memory_space=pl.ANY),
            pl.BlockSpec((1, d, TF_EXP), lambda j, f, s: (s[blk(j, s)], 0, fch(j, f, s))),
            pl.BlockSpec((1, d, TF_EXP), lambda j, f, s: (s[blk(j, s)], 0, nf + fch(j, f, s))),
            pl.BlockSpec((1, 1, TF_EXP), lambda j, f, s: (s[blk(j, s)], 0, fch(j, f, s))),
            pl.BlockSpec((1, 1, TF_EXP), lambda j, f, s: (s[blk(j, s)], 0, nf + fch(j, f, s))),
            pl.BlockSpec((1, TF_EXP, d), lambda j, f, s: (s[blk(j, s)], fch(j, f, s), 0)),
            pl.BlockSpec((1, 1, d), lambda j, f, s: (s[blk(j, s)], 0, 0)),
        ],
        out_specs=pl.BlockSpec(memory_space=pl.ANY),
        scratch_shapes=[pltpu.VMEM((2, bm, dh), jnp.uint32), pltpu.VMEM((bm, d), BF16),
                        pltpu.VMEM((2, bm, d), F32),
                        pltpu.SemaphoreType.DMA((2,)), pltpu.SemaphoreType.DMA((2,))],
    )
    return pl.pallas_call(
        functools.partial(_expert_kernel, n_blocks=n_blocks),
        grid_spec=grid_spec,
        out_shape=jax.ShapeDtypeStruct((p, 1, d), F32),
        compiler_params=_params(("arbitrary", "arbitrary")),
        name="expert_mlp",
    )(sp, xs, w_gu, w_gu, b_gu.reshape(n_experts, 1, ff2), b_gu.reshape(n_experts, 1, ff2),
      w_dn, b_dn.reshape(n_experts, 1, d))


def _combine_kernel(dest_ref, dest_next_ref, gate_ref, x1_ref, mod_ref, g_ref, ys_ref, oa_ref, ob_ref,
                    buf_ref, row_ref, sems, *, first_tiles):
    tm = x1_ref.shape[0]
    i = pl.program_id(0)
    slot = i % 2

    def gather(table_ref, s):
        def body(j, carry):
            for r in range(TOP_K):
                d = table_ref[r, j]
                pltpu.make_async_copy(ys_ref.at[d], buf_ref.at[s, r, j], sems.at[s]).start(priority=r % 2)
            return carry

        lax.fori_loop(0, tm, body, 0, unroll=8)

    @pl.when(i == 0)
    def _():
        gather(dest_ref, 0)

    @pl.when(i + 1 < pl.num_programs(0))
    def _():
        gather(dest_next_ref, 1 - slot)

    for r in range(TOP_K):
        pltpu.make_async_copy(ys_ref.at[pl.ds(0, tm)], buf_ref.at[slot, r], sems.at[slot]).wait()

    gates = gate_ref[...]
    lane = lax.broadcasted_iota(jnp.int32, gates.shape, 1)
    f = None
    for r in range(TOP_K):
        g = jnp.sum(jnp.where(lane == r, gates, 0.0), axis=-1, keepdims=True)
        row_ref[...] = buf_ref[slot, r].reshape(row_ref.shape)
        term = row_ref[...] * g
        f = term if f is None else f + term
    out = x1_ref[...] + mod_ref[0, 5:6, :] * _rms(f, g_ref[...])

    @pl.when(pl.program_id(0) < first_tiles)
    def _():
        oa_ref[...] = out

    @pl.when(pl.program_id(0) >= first_tiles)
    def _():
        ob_ref[...] = out


def _combine(dest, gates, x1, mod3, g_post, ys, seq, n_first):
    t, d = x1.shape
    tm = TM_DMA
    tps = seq // tm
    first_tiles = n_first * tps
    n_tiles = t // tm
    return pl.pallas_call(
        functools.partial(_combine_kernel, first_tiles=first_tiles),
        grid=(n_tiles,),
        in_specs=[
            pl.BlockSpec((8, tm), lambda i: (0, i), memory_space=pltpu.SMEM),
            pl.BlockSpec((8, tm), lambda i: (0, jnp.minimum(i + 1, n_tiles - 1)), memory_space=pltpu.SMEM),
            pl.BlockSpec((tm, LANE), lambda i: (i, 0)),
            pl.BlockSpec((tm, d), lambda i: (i, 0)),
            pl.BlockSpec((1, N_MOD, d), lambda i: (i // tps, 0, 0)),
            pl.BlockSpec((1, d), lambda i: (0, 0)),
            pl.BlockSpec(memory_space=pl.ANY),
        ],
        out_specs=(pl.BlockSpec((tm, d), lambda i: (jnp.minimum(i, first_tiles - 1), 0)),
                   pl.BlockSpec((tm, d), lambda i: (jnp.maximum(i - first_tiles, 0), 0))),
        out_shape=(jax.ShapeDtypeStruct((first_tiles * tm, d), F32),
                   jax.ShapeDtypeStruct((t - first_tiles * tm, d), F32)),
        scratch_shapes=[pltpu.VMEM((2, TOP_K, tm, 1, d), F32), pltpu.VMEM((tm, d), F32),
                        pltpu.SemaphoreType.DMA((2,))],
        compiler_params=_params(("arbitrary",)),
        name="expert_combine",
    )(dest, dest, gates, x1, mod3, g_post.reshape(1, d), ys)


def _w_in_plan(d, qr, kvr):
    qsw = SWA_Q_HEADS * SWA_HEAD_DIM
    ksw = SWA_KV_HEADS * SWA_HEAD_DIM
    names = ("c_q", "c_kv", "k_rope", "q_s", "k_s", "v_s", "g_a", "g_b")
    sizes = (qr, kvr, MLA_ROPE, qsw, ksw, ksw, d, d)
    src = {}
    off = 0
    for name, s in zip(names, sizes):
        src[name] = (off, s)
        off += s
    order = ("q_s", "g_a", "g_b", "c_q", "c_kv", "k_s", "v_s", "k_rope", "k_rope")
    plan, cols, dst = [], {}, 0
    for name in order:
        s_off, width = src[name]
        cols.setdefault(name, dst)
        plan.append((s_off, width, dst))
        dst += width
    return tuple(plan), cols, dst


def _layout_w_uq(w_uq):
    r = w_uq.shape[0]
    w = w_uq.reshape(r, MLA_HEADS, MLA_NOPE + MLA_ROPE)
    nope = w[:, :, :MLA_NOPE].reshape(r, MLA_HEADS * MLA_NOPE)
    pe = w[:, :, MLA_NOPE:].reshape(r, MLA_HEADS * MLA_ROPE)
    return jnp.concatenate([nope, pe], axis=1).astype(BF16)


def _layout_w_ukv(w_ukv):
    r = w_ukv.shape[0]
    w = w_ukv.reshape(r, MLA_HEADS, MLA_NOPE + MLA_V)
    kn = w[:, :, :MLA_NOPE].reshape(r, MLA_HEADS * MLA_NOPE)
    v = w[:, :, MLA_NOPE:].reshape(r, MLA_HEADS * MLA_V)
    return jnp.concatenate([kn, v], axis=1).astype(BF16)


def _rope_tables(seq):
    half = MLA_ROPE // 2
    freqs = jnp.power(ROPE_THETA, -2.0 * jnp.arange(half, dtype=F32) / MLA_ROPE)
    ang = jnp.arange(seq, dtype=F32)[:, None] * freqs[None, :]
    cos, sin = jnp.cos(ang), jnp.sin(ang)
    return (jnp.concatenate([cos, cos, cos, cos], axis=1),
            jnp.concatenate([-sin, sin, -sin, sin], axis=1))


def kernel(x_prompt, x_sample, c_prompt, c_sample, w_ada, b_ada, g_pre_mix, w_in, g_q_lat, w_uq,
           g_kv_lat, w_ukv, attn_sinks, w_br_mla, w_br_swa, w_out, g_post_mix, g_pre_ffn,
           w_router, b_router, w_gu, b_gu, w_dn, b_dn, g_post_ffn):
    assert MLA_ROPE == SWA_HEAD_DIM == 64 and MLA_NOPE == MLA_V == LANE
    nb_p, seq, d = x_prompt.shape
    nb_s = x_sample.shape[0]
    assert x_sample.shape[1] == seq
    nseq = nb_p + nb_s
    t = nseq * seq
    n_experts = w_router.shape[-1]
    n_blocks = t * TOP_K // BM_EXP + n_experts

    xa, xb = x_prompt.reshape(nb_p * seq, d), x_sample.reshape(nb_s * seq, d)
    c8 = jnp.concatenate([c_prompt, c_sample, jnp.zeros((8 - nseq, d), F32)], axis=0)
    cos, sin = _rope_tables(seq)

    for l in range(w_ada.shape[0]):
        plan, cols, n_cols = _w_in_plan(d, g_q_lat.shape[-1], g_kv_lat.shape[-1])
        w_uq_p = _layout_w_uq(w_uq[l])
        w_ukv_p = _layout_w_ukv(w_ukv[l])
        w_router_p = jnp.pad(w_router[l], ((0, 0), (0, LANE - n_experts))).astype(BF16)
        b_router_p = jnp.pad(b_router[l], (0, LANE - n_experts)).reshape(1, LANE)

        mod, w_in_p = _modulation(c8, w_ada[l], b_ada[l], w_in, l, plan, n_cols)
        mod3 = mod[:nseq].reshape(nseq, N_MOD, d)
        z = _in_projection(xa, xb, mod3, g_pre_mix[l], w_in_p, seq)
        q, k, v, qs, klo, khi, vlo, vhi = _post_projection(
            z, cols, cos, sin, g_q_lat[l], g_kv_lat[l], w_uq_p, w_ukv_p, nseq, seq)
        o_a, w_gu_b, w_dn_b = _mla_attention(q, k, v, w_gu[l], w_dn[l])
        o_b = _swa_attention(attn_sinks[l], qs, klo, khi, vlo, vhi, nseq, seq)
        y = _merge(o_a, o_b, z, cols, w_br_mla[l].astype(BF16), w_br_swa[l].astype(BF16))
        x1, h2, idx, gates, cnt = _out_projection(
            y, xa, xb, mod3, g_post_mix[l], g_pre_ffn[l], w_out[l].astype(BF16), w_router_p, b_router_p,
            n_experts, seq)
        dest, meta, zst = _positions(idx, cnt, n_experts, n_blocks)
        sp = jnp.concatenate([meta[:n_blocks, 0], meta[0:1, 1], meta[:n_blocks, 2]])
        zmeta = jnp.concatenate([zst[0, :n_experts], meta[0:1, 1]])
        xs = _dispatch(zmeta, dest, h2, n_experts, n_blocks)
        ys = _experts(sp, xs, w_gu_b, b_gu[l], w_dn_b, b_dn[l], n_blocks)
        xa, xb = _combine(dest, gates, x1, mod3, g_post_ffn[l], ys, seq, nb_p)

    return (xa.reshape(nb_p, seq, d), xb.reshape(nb_s, seq, d))
```

```python
import functools

import jax
import jax.numpy as jnp
from jax import lax
from jax.experimental import pallas as pl
from jax.experimental.pallas import tpu as pltpu

MLA_HEADS = 16
MLA_NOPE = 128
MLA_ROPE = 64
MLA_V = 128
SWA_Q_HEADS = 32
SWA_KV_HEADS = 8
SWA_HEAD_DIM = 64
WINDOW = 128
TOP_K = 4
SWIGLU_LIMIT = 7.0
SWIGLU_ALPHA = 1.702
ROPE_THETA = 10000.0
RMS_EPS = 1e-6
N_MOD = 6

LANE = 128
VMEM_LIMIT = 56 << 20

TM_IN = 512
TN_IN = 1664
TM_POST = 256
TQ_MLA = 1024
KV_CHUNKS_MLA = (1024, 1024, 1024, 1024)
TQ_SWA = 512
SUB_SWA = 256
TM_MERGE = 1024
TN_MERGE = 1024
TM_OUT = 512
OUT_ROW_SPLITS = 2
TM_POS = 1024
TM_DISPATCH = 1024
TM_DMA = 256
BM_EXP = 512
TF_EXP = 1024

F32 = jnp.float32
BF16 = jnp.bfloat16
NEG_INF = float("-inf")
LOG2E = 1.4426950408889634


def _params(sem):
    return pltpu.CompilerParams(dimension_semantics=sem, vmem_limit_bytes=VMEM_LIMIT)


def _rms(x, g):
    return x * lax.rsqrt(jnp.mean(x * x, axis=-1, keepdims=True) + RMS_EPS) * g


def _mod_kernel(c_ref, w_ref, b_ref, win_ref, o_ref, wout_ref, *, plan, n_slabs):
    c = c_ref[...]
    a = (c * jax.nn.sigmoid(c)).astype(BF16)
    o_ref[...] = jnp.dot(a, w_ref[...].astype(BF16), preferred_element_type=F32) + b_ref[...]

    @pl.when(pl.program_id(0) < n_slabs)
    def _():
        for src, width, dst in plan:
            wout_ref[:, dst:dst + width] = win_ref[:, src:src + width].astype(BF16)


def _modulation(c8, w_ada, b_ada, w_in, plan, n_cols):
    d, n = w_ada.shape
    tn = 1024
    n_steps = n // tn
    n_slabs = 8
    assert n_slabs <= n_steps and d % (8 * n_slabs) == 0
    rows = d // n_slabs
    slab = lambda j: (jnp.minimum(j, n_slabs - 1), 0)
    return pl.pallas_call(
        functools.partial(_mod_kernel, plan=plan, n_slabs=n_slabs),
        grid=(n_steps,),
        in_specs=[
            pl.BlockSpec((8, d), lambda j: (0, 0)),
            pl.BlockSpec((d, tn), lambda j: (0, j)),
            pl.BlockSpec((1, tn), lambda j: (0, j)),
            pl.BlockSpec((rows, w_in.shape[1]), slab),
        ],
        out_specs=(pl.BlockSpec((8, tn), lambda j: (0, j)),
                   pl.BlockSpec((rows, n_cols), slab)),
        out_shape=(jax.ShapeDtypeStruct((8, n), F32),
                   jax.ShapeDtypeStruct((d, n_cols), BF16)),
        compiler_params=_params(("arbitrary",)),
        name="adaln_mod",
    )(c8, w_ada, b_ada.reshape(1, n), w_in)


def _two_group_specs(tm, d, first_tiles, n_grid_axes):
    if n_grid_axes == 1:
        return (pl.BlockSpec((tm, d), lambda i: (jnp.minimum(i, first_tiles - 1), 0)),
                pl.BlockSpec((tm, d), lambda i: (jnp.maximum(i - first_tiles, 0), 0)))
    return (pl.BlockSpec((tm, d), lambda i, j: (jnp.minimum(i, first_tiles - 1), 0)),
            pl.BlockSpec((tm, d), lambda i, j: (jnp.maximum(i - first_tiles, 0), 0)))


def _inproj_kernel(xa_ref, xb_ref, mod_ref, g_ref, w_ref, z_ref, h_ref, *, first_tiles):
    def normalise(x_ref):
        x = x_ref[...]
        gain = g_ref[...] * (1.0 + mod_ref[0, 1:2, :])
        h = x * lax.rsqrt(jnp.mean(x * x, axis=-1, keepdims=True) + RMS_EPS) * gain + mod_ref[0, 0:1, :]
        h_ref[...] = h.astype(BF16)

    first_col = pl.program_id(1) == 0
    in_first = pl.program_id(0) < first_tiles
    pl.when(first_col & in_first)(functools.partial(normalise, xa_ref))
    pl.when(first_col & jnp.logical_not(in_first))(functools.partial(normalise, xb_ref))

    z_ref[...] = jnp.dot(h_ref[...], w_ref[...], preferred_element_type=F32).astype(z_ref.dtype)


def _in_projection(xa, xb, mod3, g, w_in_p, seq):
    d = xa.shape[1]
    t = xa.shape[0] + xb.shape[0]
    n = w_in_p.shape[1]
    tiles_per_seq = seq // TM_IN
    first_tiles = xa.shape[0] // TM_IN
    return pl.pallas_call(
        functools.partial(_inproj_kernel, first_tiles=first_tiles),
        grid=(t // TM_IN, n // TN_IN),
        in_specs=[
            *_two_group_specs(TM_IN, d, first_tiles, 2),
            pl.BlockSpec((1, N_MOD, d), lambda i, j: (i // tiles_per_seq, 0, 0)),
            pl.BlockSpec((1, d), lambda i, j: (0, 0)),
            pl.BlockSpec((d, TN_IN), lambda i, j: (0, j)),
        ],
        out_specs=pl.BlockSpec((TM_IN, TN_IN), lambda i, j: (i, j)),
        out_shape=jax.ShapeDtypeStruct((t, n), BF16),
        scratch_shapes=[pltpu.VMEM((TM_IN, d), BF16)],
        compiler_params=_params(("arbitrary", "arbitrary")),
        name="in_projection",
    )(xa, xb, mod3, g.reshape(1, d), w_in_p)


def _postproj_kernel(cq_ref, ckv_ref, qs_ref, ks_ref, vs_ref, kr_ref, cos_ref, sin_ref,
                     gq_ref, gkv_ref, wuq_ref, wukv_ref,
                     q_ref, k_ref, v_ref, qso_ref, klo_ref, khi_ref, vlo_ref, vhi_ref):
    tm = cq_ref.shape[0]
    cos = cos_ref[...]
    sin = sin_ref[...]
    lane = lax.broadcasted_iota(jnp.int32, (tm, LANE), 1)
    first_half = (lane & 63) < 32
    low = lane < 64

    def rope(x, cos=cos, sin=sin):
        rot = jnp.where(first_half, pltpu.roll(x, LANE - 32, 1), pltpu.roll(x, 32, 1))
        return x * cos + rot * sin

    nh = MLA_HEADS
    scale = float((MLA_NOPE + MLA_ROPE) ** -0.5) * LOG2E
    cqn = (_rms(cq_ref[...].astype(F32), gq_ref[...]) * scale).astype(BF16)
    q = jnp.dot(cqn, wuq_ref[...], preferred_element_type=F32)
    for h in range(nh):
        q_ref[0, h, :, 0:LANE] = q[:, h * LANE:(h + 1) * LANE].astype(BF16)
    for m in range(nh // 2):
        pe = rope(q[:, (nh + m) * LANE:(nh + m + 1) * LANE]).astype(BF16)
        q_ref[0, 2 * m, :, LANE:2 * LANE] = pe
        q_ref[0, 2 * m + 1, :, LANE:2 * LANE] = pe

    ckvn = _rms(ckv_ref[...].astype(F32), gkv_ref[...]).astype(BF16)
    kv = jnp.dot(ckvn, wukv_ref[...], preferred_element_type=F32)
    kr = rope(kr_ref[...].astype(F32))
    kpe_lo = jnp.where(low, kr, 0.0).astype(BF16)
    kpe_hi = jnp.where(low, 0.0, kr).astype(BF16)
    for h in range(nh):
        k_ref[0, h, :, 0:LANE] = kv[:, h * LANE:(h + 1) * LANE].astype(BF16)
        k_ref[0, h, :, LANE:2 * LANE] = kpe_lo if h % 2 == 0 else kpe_hi
        v_ref[0, h, :, :] = kv[:, (nh + h) * LANE:(nh + h + 1) * LANE].astype(BF16)

    swa_scale = float(SWA_HEAD_DIM ** -0.5) * LOG2E
    cos_q, sin_q = cos * swa_scale, sin * swa_scale
    for m in range(SWA_Q_HEADS // 2):
        x = qs_ref[:, m * LANE:(m + 1) * LANE].astype(F32)
        qso_ref[:, m * LANE:(m + 1) * LANE] = rope(x, cos_q, sin_q).astype(BF16)

    for m in range(SWA_KV_HEADS // 2):
        sl = slice(m * LANE, (m + 1) * LANE)
        for src_ref, lo_ref, hi_ref, roped in ((ks_ref, klo_ref, khi_ref, True),
                                               (vs_ref, vlo_ref, vhi_ref, False)):
            a = src_ref[:, sl].astype(F32)
            if roped:
                a = rope(a)
            b = pltpu.roll(a, 64, 1)
            e0 = slice((2 * m) * LANE, (2 * m + 1) * LANE)
            e1 = slice((2 * m + 1) * LANE, (2 * m + 2) * LANE)
            lo_ref[:, e0] = jnp.where(low, a, 0.0).astype(BF16)
            hi_ref[:, e0] = jnp.where(low, 0.0, b).astype(BF16)
            lo_ref[:, e1] = jnp.where(low, b, 0.0).astype(BF16)
            hi_ref[:, e1] = jnp.where(low, 0.0, a).astype(BF16)


def _post_projection(z, cols, cos, sin, g_q, g_kv, w_uq_p, w_ukv_p, nseq, seq):
    t = z.shape[0]
    tm = TM_POST
    tps = seq // tm
    nh = MLA_HEADS
    qr, kvr = g_q.shape[0], g_kv.shape[0]
    qsw = SWA_Q_HEADS * SWA_HEAD_DIM
    ksw = SWA_KV_HEADS * SWA_HEAD_DIM

    def zspec(width, off):
        blk = off // width
        return pl.BlockSpec((tm, width), lambda i: (i, blk))

    head_map = lambda i: (i // tps, 0, i % tps, 0)
    tok_map = lambda i: (i, 0)
    out_shapes = (
        jax.ShapeDtypeStruct((nseq, nh, seq, 2 * LANE), BF16),
        jax.ShapeDtypeStruct((nseq, nh, seq, 2 * LANE), BF16),
        jax.ShapeDtypeStruct((nseq, nh, seq, LANE), BF16),
        jax.ShapeDtypeStruct((t, qsw), BF16),
        jax.ShapeDtypeStruct((t, SWA_KV_HEADS * LANE), BF16),
        jax.ShapeDtypeStruct((t, SWA_KV_HEADS * LANE), BF16),
        jax.ShapeDtypeStruct((t, SWA_KV_HEADS * LANE), BF16),
        jax.ShapeDtypeStruct((t, SWA_KV_HEADS * LANE), BF16),
    )
    out_specs = (
        pl.BlockSpec((1, nh, tm, 2 * LANE), head_map),
        pl.BlockSpec((1, nh, tm, 2 * LANE), head_map),
        pl.BlockSpec((1, nh, tm, LANE), head_map),
        pl.BlockSpec((tm, qsw), tok_map),
        pl.BlockSpec((tm, SWA_KV_HEADS * LANE), tok_map),
        pl.BlockSpec((tm, SWA_KV_HEADS * LANE), tok_map),
        pl.BlockSpec((tm, SWA_KV_HEADS * LANE), tok_map),
        pl.BlockSpec((tm, SWA_KV_HEADS * LANE), tok_map),
    )
    return pl.pallas_call(
        _postproj_kernel,
        grid=(t // tm,),
        in_specs=[
            zspec(qr, cols["c_q"]), zspec(kvr, cols["c_kv"]), zspec(qsw, cols["q_s"]),
            zspec(ksw, cols["k_s"]), zspec(ksw, cols["v_s"]), zspec(LANE, cols["k_rope"]),
            pl.BlockSpec((tm, LANE), lambda i: (i % tps, 0)),
            pl.BlockSpec((tm, LANE), lambda i: (i % tps, 0)),
            pl.BlockSpec((1, qr), lambda i: (0, 0)),
            pl.BlockSpec((1, kvr), lambda i: (0, 0)),
            pl.BlockSpec(w_uq_p.shape, lambda i: (0, 0)),
            pl.BlockSpec(w_ukv_p.shape, lambda i: (0, 0)),
        ],
        out_specs=out_specs,
        out_shape=out_shapes,
        compiler_params=_params(("arbitrary",)),
        name="post_projection",
    )(z, z, z, z, z, z, cos, sin, g_q.reshape(1, qr), g_kv.reshape(1, kvr), w_uq_p, w_ukv_p)


def _mla_kernel(q_ref, k_ref, v_ref, wgu_ref, wdn_ref, o_ref, wgu_o_ref, wdn_o_ref, *, n_gu):
    step = (pl.program_id(0) * pl.num_programs(1) + pl.program_id(1)) * pl.num_programs(2) + pl.program_id(2)

    @pl.when(step < n_gu)
    def _():
        wgu_o_ref[...] = wgu_ref[...].astype(BF16)

    @pl.when(step >= n_gu)
    def _():
        wdn_o_ref[...] = wdn_ref[...].astype(BF16)

    q = q_ref[0, 0]
    seq = k_ref.shape[2]
    m = l = acc = None
    assert sum(KV_CHUNKS_MLA) == seq
    start = 0
    for width in KV_CHUNKS_MLA:
        rows = slice(start, start + width)
        start += width
        s = lax.dot_general(q, k_ref[0, 0, rows, :], (((1,), (1,)), ((), ())),
                            preferred_element_type=F32)
        mc = jnp.max(s, axis=-1, keepdims=True)
        m_new = mc if m is None else jnp.maximum(m, mc)
        p = jnp.exp2(s - m_new)
        ps = jnp.sum(p, axis=-1, keepdims=True)
        pv = jnp.dot(p.astype(BF16), v_ref[0, 0, rows, :], preferred_element_type=F32)
        if m is None:
            l, acc = ps, pv
        else:
            alpha = jnp.exp2(m - m_new)
            l = alpha * l + ps
            acc = alpha * acc + pv
        m = m_new
    o_ref[...] = (acc / l).astype(o_ref.dtype)


def _mla_attention(q, k, v, w_gu, w_dn):
    nseq, nh, seq, dqk = q.shape
    dv = v.shape[-1]
    nq = seq // TQ_MLA
    n_steps = nseq * nh * nq
    gu2d = w_gu.reshape(-1, w_gu.shape[-1])
    dn2d = w_dn.reshape(-1, w_dn.shape[-1])
    n_gu, n_dn = 2 * n_steps // 3, n_steps // 3
    assert n_gu + n_dn == n_steps and gu2d.shape[0] % n_gu == 0 and dn2d.shape[0] % n_dn == 0
    gu_rows, dn_rows = gu2d.shape[0] // n_gu, dn2d.shape[0] // n_dn
    assert gu_rows % 16 == 0 and dn_rows % 16 == 0

    def step(b, h, i):
        return (b * nh + h) * nq + i

    gu_map = lambda b, h, i: (jnp.minimum(step(b, h, i), n_gu - 1), 0)
    dn_map = lambda b, h, i: (jnp.maximum(step(b, h, i) - n_gu, 0), 0)
    o, gu_b, dn_b = pl.pallas_call(
        functools.partial(_mla_kernel, n_gu=n_gu),
        grid=(nseq, nh, nq),
        in_specs=[
            pl.BlockSpec((1, 1, TQ_MLA, dqk), lambda b, h, i: (b, h, i, 0)),
            pl.BlockSpec((1, 1, seq, dqk), lambda b, h, i: (b, h, 0, 0)),
            pl.BlockSpec((1, 1, seq, dv), lambda b, h, i: (b, h, 0, 0)),
            pl.BlockSpec((gu_rows, gu2d.shape[1]), gu_map),
            pl.BlockSpec((dn_rows, dn2d.shape[1]), dn_map),
        ],
        out_specs=(
            pl.BlockSpec((TQ_MLA, dv), lambda b, h, i: (b * nq + i, h)),
            pl.BlockSpec((gu_rows, gu2d.shape[1]), gu_map),
            pl.BlockSpec((dn_rows, dn2d.shape[1]), dn_map),
        ),
        out_shape=(
            jax.ShapeDtypeStruct((nseq * seq, nh * dv), BF16),
            jax.ShapeDtypeStruct(gu2d.shape, BF16),
            jax.ShapeDtypeStruct(dn2d.shape, BF16),
        ),
        compiler_params=_params(("arbitrary", "arbitrary", "arbitrary")),
        name="mla_attention",
    )(q, k, v, gu2d, dn2d)
    return o, gu_b.reshape(w_gu.shape), dn_b.reshape(w_dn.shape)


def _swa_kernel(sink_ref, q_ref,
                klo_p, klo_c, klo_n, khi_p, khi_c, khi_n,
                vlo_p, vlo_c, vlo_n, vhi_p, vhi_c, vhi_n,
                o_ref, klo_w, khi_w, vlo_w, vhi_w, *, seq):
    tb = q_ref.shape[0]
    tq = SUB_SWA
    w = WINDOW
    for win, (p, c, n) in ((klo_w, (klo_p, klo_c, klo_n)), (khi_w, (khi_p, khi_c, khi_n)),
                           (vlo_w, (vlo_p, vlo_c, vlo_n)), (vhi_w, (vhi_p, vhi_c, vhi_n))):
        win[0:w, :] = p[...]
        win[w:w + tb, :] = c[...]
        win[w + tb:tb + 2 * w, :] = n[...]

    assert SWA_Q_HEADS // SWA_KV_HEADS == 4 and tq & (tq - 1) == 0 and tb % tq == 0
    nk = tq + 2 * w
    row2 = lax.broadcasted_iota(jnp.int32, (2 * tq, nk), 0)
    col = lax.broadcasted_iota(jnp.int32, (2 * tq, nk), 1)
    rel = col - w - (row2 & (tq - 1))
    band = (rel <= w) & (rel >= -w)
    top = lax.broadcasted_iota(jnp.int32, (2 * tq, 1), 0) < tq

    for sub in range(tb // tq):
        qrows = slice(sub * tq, (sub + 1) * tq)
        krows = slice(sub * tq, sub * tq + nk)
        kpos = pl.program_id(1) * tb + sub * tq - w + col
        valid = band & (kpos >= 0) & (kpos < seq)
        for g in range(SWA_KV_HEADS):
            gs = slice(g * LANE, (g + 1) * LANE)
            pa = slice((2 * g) * LANE, (2 * g + 1) * LANE)
            pb = slice((2 * g + 1) * LANE, (2 * g + 2) * LANE)
            qq = jnp.concatenate([q_ref[qrows, pa], q_ref[qrows, pb]], axis=0)
            acc = None
            for half, (kw, vw) in enumerate(((klo_w, vlo_w), (khi_w, vhi_w))):
                sink = jnp.where(top, sink_ref[4 * g + half], sink_ref[4 * g + 2 + half]) * LOG2E
                s = lax.dot_general(qq, kw[krows, gs], (((1,), (1,)), ((), ())),
                                    preferred_element_type=F32)
                s = jnp.where(valid, s, NEG_INF)
                mx = jnp.maximum(jnp.max(s, axis=-1, keepdims=True), sink)
                e = jnp.exp2(s - mx)
                den = jnp.sum(e, axis=-1, keepdims=True) + jnp.exp2(sink - mx)
                pv = jnp.dot(e.astype(BF16), vw[krows, gs], preferred_element_type=F32) / den
                acc = pv if acc is None else acc + pv
            o_ref[qrows, pa] = acc[0:tq].astype(o_ref.dtype)
            o_ref[qrows, pb] = acc[tq:2 * tq].astype(o_ref.dtype)


def _swa_attention(sinks, qs, klo, khi, vlo, vhi, nseq, seq):
    t, qw = qs.shape
    kw = klo.shape[1]
    tq = TQ_SWA
    nq = seq // tq
    r = tq // WINDOW
    nwb = seq // WINDOW

    prev = pl.BlockSpec((WINDOW, kw), lambda b, i, s: (b * nwb + jnp.maximum(i * r - 1, 0), 0))
    cur = pl.BlockSpec((tq, kw), lambda b, i, s: (b * nq + i, 0))
    nxt = pl.BlockSpec((WINDOW, kw), lambda b, i, s: (b * nwb + jnp.minimum((i + 1) * r, nwb - 1), 0))
    grid_spec = pltpu.PrefetchScalarGridSpec(
        num_scalar_prefetch=1,
        grid=(nseq, nq),
        in_specs=[pl.BlockSpec((tq, qw), lambda b, i, s: (b * nq + i, 0))] + [prev, cur, nxt] * 4,
        out_specs=pl.BlockSpec((tq, qw), lambda b, i, s: (b * nq + i, 0)),
        scratch_shapes=[pltpu.VMEM((tq + 2 * WINDOW, kw), BF16)] * 4,
    )
    return pl.pallas_call(
        functools.partial(_swa_kernel, seq=seq),
        grid_spec=grid_spec,
        out_shape=jax.ShapeDtypeStruct((t, qw), BF16),
        compiler_params=_params(("arbitrary", "arbitrary")),
        name="swa_attention",
    )(sinks, qs, klo, klo, klo, khi, khi, khi, vlo, vlo, vlo, vhi, vhi, vhi)


def _merge_kernel(oa_ref, ob_ref, ga_ref, gb_ref, wa_ref, wb_ref, y_ref):
    a = jnp.dot(oa_ref[...], wa_ref[...], preferred_element_type=F32)
    b = jnp.dot(ob_ref[...], wb_ref[...], preferred_element_type=F32)
    y = jax.nn.sigmoid(ga_ref[...].astype(F32)) * a + jax.nn.sigmoid(gb_ref[...].astype(F32)) * b
    y_ref[...] = y.astype(y_ref.dtype)


def _merge(o_a, o_b, z, cols, w_a, w_b):
    t, d = o_a.shape[0], w_a.shape[1]
    tm, tn = TM_MERGE, TN_MERGE
    ga_blk, gb_blk = cols["g_a"] // tn, cols["g_b"] // tn
    return pl.pallas_call(
        _merge_kernel,
        grid=(t // tm, d // tn),
        in_specs=[
            pl.BlockSpec((tm, o_a.shape[1]), lambda i, j: (i, 0)),
            pl.BlockSpec((tm, o_b.shape[1]), lambda i, j: (i, 0)),
            pl.BlockSpec((tm, tn), lambda i, j: (i, ga_blk + j)),
            pl.BlockSpec((tm, tn), lambda i, j: (i, gb_blk + j)),
            pl.BlockSpec((w_a.shape[0], tn), lambda i, j: (0, j)),
            pl.BlockSpec((w_b.shape[0], tn), lambda i, j: (0, j)),
        ],
        out_specs=pl.BlockSpec((tm, tn), lambda i, j: (i, j)),
        out_shape=jax.ShapeDtypeStruct((t, d), BF16),
        compiler_params=_params(("arbitrary", "arbitrary")),
        name="branch_merge",
    )(o_a, o_b, z, z, w_a, w_b)


def _outproj_kernel(y_ref, xa_ref, xb_ref, mod_ref, gpm_ref, gpf_ref, wo_ref, wr_ref, br_ref,
                    x1_ref, h2_ref, idx_ref, gate_ref, cnt_ref, pack_ref, *, n_experts, first_tiles):
    tm = y_ref.shape[0]
    u_all = jnp.dot(y_ref[...], wo_ref[...], preferred_element_type=F32)
    in_first = pl.program_id(0) < first_tiles
    hm = tm // OUT_ROW_SPLITS
    lane = lax.broadcasted_iota(jnp.int32, (hm, LANE), 1)
    lane_f = lane.astype(F32)
    counts = None
    for part in range(OUT_ROW_SPLITS):
        rows = slice(part * hm, (part + 1) * hm)
        x = jnp.where(in_first, xa_ref[rows, :], xb_ref[rows, :])
        x1 = x + mod_ref[0, 2:3, :] * _rms(u_all[rows], gpm_ref[...])
        x1_ref[rows, :] = x1
        h2 = _rms(x1, gpf_ref[...]) * (1.0 + mod_ref[0, 4:5, :]) + mod_ref[0, 3:4, :]
        h2b = h2.astype(BF16)
        bits = pltpu.bitcast(h2b.astype(F32), jnp.uint32)
        half = bits.shape[1] // 2
        pack_ref[rows, :] = (bits[:, :half] >> 16) | (bits[:, half:] & jnp.uint32(0xFFFF0000))
        h2_ref[rows] = pack_ref[rows, :].reshape(hm, 1, half)

        logits = jnp.dot(h2b, wr_ref[...], preferred_element_type=F32) + br_ref[...]
        cur = jnp.where(lane < n_experts, logits, NEG_INF)
        vals, idxs = [], []
        for _ in range(TOP_K):
            m = jnp.max(cur, axis=-1, keepdims=True)
            ix = jnp.min(jnp.where(cur == m, lane_f, float(LANE)), axis=-1, keepdims=True)
            vals.append(m)
            idxs.append(ix)
            cur = jnp.where(lane_f == ix, NEG_INF, cur)
        es = [jnp.exp(v - vals[0]) for v in vals]
        den = es[0]
        for e in es[1:]:
            den = den + e
        idx_out = jnp.zeros((hm, LANE), F32)
        gate_out = jnp.zeros((hm, LANE), F32)
        sel = jnp.zeros((hm, LANE), F32)
        for r in range(TOP_K):
            idx_out = jnp.where(lane == r, idxs[r], idx_out)
            gate_out = jnp.where(lane == r, es[r] / den, gate_out)
            sel = sel + jnp.where(lane_f == idxs[r], 1.0, 0.0)
        idx_ref[rows, :] = idx_out.astype(jnp.int32)
        gate_ref[rows, :] = gate_out
        part_counts = jnp.sum(sel, axis=0, keepdims=True)
        counts = part_counts if counts is None else counts + part_counts

    @pl.when(pl.program_id(0) == 0)
    def _():
        cnt_ref[...] = jnp.zeros_like(cnt_ref)

    cnt_ref[0:1, :] += counts


def _out_projection(y, xa, xb, mod3, g_pm, g_pf, w_out, w_router_p, b_router_p, n_experts, seq):
    t, d = y.shape
    tm = TM_OUT
    tps = seq // tm
    first_tiles = xa.shape[0] // tm
    tok = lambda i: (i, 0)
    const = lambda i: (0, 0)
    return pl.pallas_call(
        functools.partial(_outproj_kernel, n_experts=n_experts, first_tiles=first_tiles),
        grid=(t // tm,),
        in_specs=[
            pl.BlockSpec((tm, d), tok),
            *_two_group_specs(tm, d, first_tiles, 1),
            pl.BlockSpec((1, N_MOD, d), lambda i: (i // tps, 0, 0)),
            pl.BlockSpec((1, d), const),
            pl.BlockSpec((1, d), const),
            pl.BlockSpec((d, d), const),
            pl.BlockSpec((d, LANE), const),
            pl.BlockSpec((1, LANE), const),
        ],
        out_specs=(
            pl.BlockSpec((tm, d), tok),
            pl.BlockSpec((tm, 1, d // 2), lambda i: (i, 0, 0)),
            pl.BlockSpec((tm, LANE), tok),
            pl.BlockSpec((tm, LANE), tok),
            pl.BlockSpec((8, LANE), const),
        ),
        out_shape=(
            jax.ShapeDtypeStruct((t, d), F32),
            jax.ShapeDtypeStruct((t, 1, d // 2), jnp.uint32),
            jax.ShapeDtypeStruct((t, LANE), jnp.int32),
            jax.ShapeDtypeStruct((t, LANE), F32),
            jax.ShapeDtypeStruct((8, LANE), F32),
        ),
        scratch_shapes=[pltpu.VMEM((tm, d // 2), jnp.uint32)],
        compiler_params=_params(("arbitrary",)),
        name="out_projection_router",
    )(y, xa, xb, mod3, g_pm.reshape(1, d), g_pf.reshape(1, d), w_out, w_router_p, b_router_p)


def _positions_kernel(idx_ref, cnt_ref, dest_ref, meta_ref, zst_ref, carry_ref, ltri_ref, pst_ref,
                      *, n_experts, bm):
    tm = idx_ref.shape[0]
    nbp = meta_ref.shape[0]

    @pl.when(pl.program_id(0) == 0)
    def _():
        r = lax.broadcasted_iota(jnp.int32, (tm, tm), 0)
        c = lax.broadcasted_iota(jnp.int32, (tm, tm), 1)
        ltri_ref[...] = jnp.where(c < r, 1.0, 0.0).astype(BF16)
        carry_ref[...] = jnp.zeros_like(carry_ref)
        nblk = jnp.floor((cnt_ref[...] + float(bm - 1)) * (1.0 / bm))
        ur = lax.broadcasted_iota(jnp.int32, (LANE, LANE), 0)
        uc = lax.broadcasted_iota(jnp.int32, (LANE, LANE), 1)
        upper = jnp.where(ur <= uc, 1.0, 0.0).astype(BF16)
        pend = jnp.dot(nblk.astype(BF16), upper, preferred_element_type=F32)
        pst_ref[...] = (pend - nblk) * float(bm)
        lane8 = lax.broadcasted_iota(jnp.int32, (8, LANE), 1)
        zst_ref[...] = jnp.maximum(pend * float(bm) - float(bm), 0.0).astype(jnp.int32)
        pend0 = pend[0:1, :]
        n_used = jnp.sum(jnp.where(lane8[0:1, :] == n_experts - 1, pend0, 0.0), axis=-1, keepdims=True)
        blk = lax.broadcasted_iota(jnp.int32, (nbp, LANE), 0).astype(F32)
        lane = lax.broadcasted_iota(jnp.int32, (nbp, LANE), 1)
        passed = jnp.where((lane < n_experts) & (pend0 <= blk), 1.0, 0.0)
        blk_e = jnp.minimum(jnp.sum(passed, axis=-1, keepdims=True), float(n_experts - 1))
        mine = lane.astype(F32) == blk_e
        e_count = jnp.sum(jnp.where(mine, cnt_ref[0:1, :], 0.0), axis=-1, keepdims=True)
        e_first = jnp.sum(jnp.where(mine, pend0 - nblk[0:1, :], 0.0), axis=-1, keepdims=True)
        n_valid = jnp.clip(e_count - (blk[:, 0:1] - e_first) * float(bm), 0.0, float(bm))
        meta = jnp.where(lane == 0, blk_e, jnp.where(lane == 1, n_used, jnp.where(lane == 2, n_valid, 0.0)))
        meta_ref[...] = meta.astype(jnp.int32)

    lane = lax.broadcasted_iota(jnp.int32, (tm, LANE), 1)
    idx = idx_ref[...]
    hots = []
    sel = jnp.zeros((tm, LANE), F32)
    for r in range(TOP_K):
        col = jnp.sum(jnp.where(lane == r, idx, 0).astype(F32), axis=-1, keepdims=True)
        hot = jnp.where(lane.astype(F32) == col, 1.0, 0.0)
        hots.append(hot)
        sel = sel + hot
    rank = jnp.dot(ltri_ref[...], sel.astype(BF16), preferred_element_type=F32) + carry_ref[0:1, :]
    pos = pst_ref[0:1, :] + rank
    dest = jnp.zeros((tm, LANE), F32)
    for r in range(TOP_K):
        d = jnp.sum(hots[r] * pos, axis=-1, keepdims=True)
        dest = jnp.where(lane == r, d, dest)
    dest_ref[...] = dest.T[0:8, :].astype(jnp.int32)
    carry_ref[0:1, :] += jnp.sum(sel, axis=0, keepdims=True)


def _positions(idx, cnt, n_experts, n_blocks):
    t = idx.shape[0]
    tm = TM_POS
    nbp = -(-n_blocks // 8) * 8
    return pl.pallas_call(
        functools.partial(_positions_kernel, n_experts=n_experts, bm=BM_EXP),
        grid=(t // tm,),
        in_specs=[pl.BlockSpec((tm, LANE), lambda i: (i, 0)),
                  pl.BlockSpec((8, LANE), lambda i: (0, 0))],
        out_specs=(pl.BlockSpec((8, tm), lambda i: (0, i)),
                   pl.BlockSpec((nbp, LANE), lambda i: (0, 0)),
                   pl.BlockSpec((8, LANE), lambda i: (0, 0))),
        out_shape=(jax.ShapeDtypeStruct((8, t), jnp.int32),
                   jax.ShapeDtypeStruct((nbp, LANE), jnp.int32),
                   jax.ShapeDtypeStruct((8, LANE), jnp.int32)),
        scratch_shapes=[pltpu.VMEM((8, LANE), F32), pltpu.VMEM((tm, tm), BF16), pltpu.VMEM((8, LANE), F32)],
        compiler_params=_params(("arbitrary",)),
        name="dispatch_positions",
    )(idx, cnt)


def _dispatch_kernel(zst_ref, dest_ref, h2_ref, xs_ref, zero_ref, zsem, sem, *, n_experts, bm):
    i = pl.program_id(0)
    tm = dest_ref.shape[1]

    def zero_copy(e):
        return pltpu.make_async_copy(zero_ref, xs_ref.at[pl.ds(zst_ref[e], bm)], zsem)

    def tail_copy(j):
        return pltpu.make_async_copy(zero_ref, xs_ref.at[pl.ds(j * bm, bm)], zsem)

    @pl.when(i == 0)
    def _():
        zero_ref[...] = jnp.zeros_like(zero_ref)
        n_used = zst_ref[n_experts]
        n_blocks = xs_ref.shape[0] // bm
        for e in range(n_experts):
            zero_copy(e).start()
        lax.fori_loop(n_used, n_blocks, lambda j, c: (tail_copy(j).start(), c)[1], 0)
        for e in range(n_experts):
            zero_copy(e).wait()
        lax.fori_loop(n_used, n_blocks, lambda j, c: (tail_copy(j).wait(), c)[1], 0)

    def body(j, carry):
        for r in range(TOP_K):
            d = dest_ref[r, j]
            pltpu.make_async_copy(h2_ref.at[j], xs_ref.at[d], sem).start(priority=r % 2)
        return carry

    lax.fori_loop(0, tm, body, 0, unroll=8)
    for _ in range(TOP_K):
        pltpu.make_async_copy(h2_ref, xs_ref.at[pl.ds(0, tm)], sem).wait()


def _dispatch(zst, dest, h2, n_experts, n_blocks):
    t, _, d = h2.shape
    grid_spec = pltpu.PrefetchScalarGridSpec(
        num_scalar_prefetch=1,
        grid=(t // TM_DISPATCH,),
        in_specs=[pl.BlockSpec((8, TM_DISPATCH), lambda i, z: (0, i), memory_space=pltpu.SMEM),
                  pl.BlockSpec((TM_DISPATCH, 1, d), lambda i, z: (i, 0, 0))],
        out_specs=pl.BlockSpec(memory_space=pl.ANY),
        scratch_shapes=[pltpu.VMEM((BM_EXP, 1, d), h2.dtype),
                        pltpu.SemaphoreType.DMA(()), pltpu.SemaphoreType.DMA(())],
    )
    return pl.pallas_call(
        functools.partial(_dispatch_kernel, n_experts=n_experts, bm=BM_EXP),
        grid_spec=grid_spec,
        out_shape=jax.ShapeDtypeStruct((n_blocks * BM_EXP, 1, d), h2.dtype),
        compiler_params=_params(("arbitrary",)),
        name="row_dispatch",
    )(zst, dest, h2)


def _expert_kernel(sp_ref, xs_ref, wg_ref, wl_ref, bg_ref, bl_ref, wd_ref, bd_ref, ys_ref,
                   x2d_ref, xb_ref, acc2_ref, in_sems, sems, *, n_blocks):
    j = pl.program_id(0)
    f = pl.program_id(1)
    nf = pl.num_programs(1)
    n_used = sp_ref[n_blocks]
    used = j < n_used
    bm = acc2_ref.shape[1]
    slot = j % 2
    acc_ref = acc2_ref.at[slot]

    def in_copy(blk, s):
        src = jnp.minimum(blk, n_used - 1)
        return pltpu.make_async_copy(xs_ref.at[pl.ds(src * bm, bm), 0, :], x2d_ref.at[s], in_sems.at[s])

    def out_copy(blk, s):
        return pltpu.make_async_copy(acc2_ref.at[s], ys_ref.at[pl.ds(blk * bm, bm), 0, :], sems.at[s])

    @pl.when((f == 0) & (j == 0))
    def _():
        in_copy(0, 0).start()

    @pl.when((f == 0) & (j + 1 < n_blocks))
    def _():
        in_copy(j + 1, 1 - slot).start()

    @pl.when(f == 0)
    def _():
        in_copy(j, slot).wait()

    @pl.when((f == 0) & (j >= 2))
    def _():
        out_copy(j - 2, slot).wait()

    @pl.when(used & (f == 0))
    def _():
        words = x2d_ref[slot]
        half = words.shape[1]
        xb_ref[:, :half] = pltpu.bitcast(words << 16, F32).astype(BF16)
        xb_ref[:, half:] = pltpu.bitcast(words & jnp.uint32(0xFFFF0000), F32).astype(BF16)

    @pl.when(jnp.logical_not(used) & (f == 0))
    def _():
        acc_ref[...] = jnp.zeros((bm, acc2_ref.shape[2]), F32)

    def mlp(rows):
        xb = xb_ref[0:rows]
        glu = jnp.dot(xb, wg_ref[0], preferred_element_type=F32) + bg_ref[0]
        lin = jnp.dot(xb, wl_ref[0], preferred_element_type=F32) + bl_ref[0]
        glu = jnp.minimum(glu, SWIGLU_LIMIT)
        lin = jnp.clip(lin, -SWIGLU_LIMIT, SWIGLU_LIMIT)
        act = glu * jax.nn.sigmoid(SWIGLU_ALPHA * glu) * (lin + 1.0)
        prev = jnp.where(f == 0, jnp.broadcast_to(bd_ref[0], (rows, acc2_ref.shape[2])), acc_ref[0:rows])
        acc_ref[0:rows] = prev + jnp.dot(act.astype(BF16), wd_ref[0], preferred_element_type=F32)

    sparse = sp_ref[n_blocks + 1 + jnp.minimum(j, n_blocks - 1)] <= bm // 2

    @pl.when(used & jnp.logical_not(sparse))
    def _():
        mlp(bm)

    @pl.when(used & sparse)
    def _():
        mlp(bm // 2)
        acc_ref[bm // 2:bm] = jnp.zeros((bm - bm // 2, acc2_ref.shape[2]), F32)

    @pl.when(f == nf - 1)
    def _():
        out_copy(j, slot).start()

    @pl.when((f == nf - 1) & (j == n_blocks - 1))
    def _():
        out_copy(j - 1, 1 - slot).wait()
        out_copy(j, slot).wait()


def _experts(sp, xs, w_gu, b_gu, w_dn, b_dn, n_blocks):
    p, _, dh = xs.shape
    n_experts, d, ff2 = w_gu.shape
    assert d == 2 * dh
    ff = ff2 // 2
    nf = ff // TF_EXP
    bm = BM_EXP

    def blk(j, s):
        return jnp.minimum(j, s[n_blocks] - 1)

    def fch(j, f, s):
        return jnp.where(j < s[n_blocks], f, nf - 1)

    grid_spec = pltpu.PrefetchScalarGridSpec(
        num_scalar_prefetch=1,
        grid=(n_blocks, nf),
        in_specs=[
            pl.BlockSpec(memory_space=pl.ANY),
            pl.BlockSpec((1, d, TF_EXP), lambda j, f, s: (s[blk(j, s)], 0, fch(j, f, s))),
            pl.BlockSpec((1, d, TF_EXP), lambda j, f, s: (s[blk(j, s)], 0, nf + fch(j, f, s))),
            pl.BlockSpec((1, 1, TF_EXP), lambda j, f, s: (s[blk(j, s)], 0, fch(j, f, s))),
            pl.BlockSpec((1, 1, TF_EXP), lambda j, f, s: (s[blk(j, s)], 0, nf + fch(j, f, s))),
            pl.BlockSpec((1, TF_EXP, d), lambda j, f, s: (s[blk(j, s)], fch(j, f, s), 0)),
            pl.BlockSpec((1, 1, d), lambda j, f, s: (s[blk(j, s)], 0, 0)),
        ],
        out_specs=pl.BlockSpec(memory_space=pl.ANY),
        scratch_shapes=[pltpu.VMEM((2, bm, dh), jnp.uint32), pltpu.VMEM((bm, d), BF16),
                        pltpu.VMEM((2, bm, d), F32),
                        pltpu.SemaphoreType.DMA((2,)), pltpu.SemaphoreType.DMA((2,))],
    )
    return pl.pallas_call(
        functools.partial(_expert_kernel, n_blocks=n_blocks),
        grid_spec=grid_spec,
        out_shape=jax.ShapeDtypeStruct((p, 1, d), F32),
        compiler_params=_params(("arbitrary", "arbitrary")),
        name="expert_mlp",
    )(sp, xs, w_gu, w_gu, b_gu.reshape(n_experts, 1, ff2), b_gu.reshape(n_experts, 1, ff2),
      w_dn, b_dn.reshape(n_experts, 1, d))


def _combine_kernel(dest_ref, dest_next_ref, gate_ref, x1_ref, mod_ref, g_ref, ys_ref, oa_ref, ob_ref,
                    buf_ref, row_ref, sems, *, first_tiles):
    tm = x1_ref.shape[0]
    i = pl.program_id(0)
    slot = i % 2

    def gather(table_ref, s):
        def body(j, carry):
            for r in range(TOP_K):
                d = table_ref[r, j]
                pltpu.make_async_copy(ys_ref.at[d], buf_ref.at[s, r, j], sems.at[s]).start(priority=r % 2)
            return carry

        lax.fori_loop(0, tm, body, 0, unroll=8)

    @pl.when(i == 0)
    def _():
        gather(dest_ref, 0)

    @pl.when(i + 1 < pl.num_programs(0))
    def _():
        gather(dest_next_ref, 1 - slot)

    for r in range(TOP_K):
        pltpu.make_async_copy(ys_ref.at[pl.ds(0, tm)], buf_ref.at[slot, r], sems.at[slot]).wait()

    gates = gate_ref[...]
    lane = lax.broadcasted_iota(jnp.int32, gates.shape, 1)
    f = None
    for r in range(TOP_K):
        g = jnp.sum(jnp.where(lane == r, gates, 0.0), axis=-1, keepdims=True)
        row_ref[...] = buf_ref[slot, r].reshape(row_ref.shape)
        term = row_ref[...] * g
        f = term if f is None else f + term
    out = x1_ref[...] + mod_ref[0, 5:6, :] * _rms(f, g_ref[...])

    @pl.when(pl.program_id(0) < first_tiles)
    def _():
        oa_ref[...] = out

    @pl.when(pl.program_id(0) >= first_tiles)
    def _():
        ob_ref[...] = out


def _combine(dest, gates, x1, mod3, g_post, ys, seq, n_first):
    t, d = x1.shape
    tm = TM_DMA
    tps = seq // tm
    first_tiles = n_first * tps
    n_tiles = t // tm
    return pl.pallas_call(
        functools.partial(_combine_kernel, first_tiles=first_tiles),
        grid=(n_tiles,),
        in_specs=[
            pl.BlockSpec((8, tm), lambda i: (0, i), memory_space=pltpu.SMEM),
            pl.BlockSpec((8, tm), lambda i: (0, jnp.minimum(i + 1, n_tiles - 1)), memory_space=pltpu.SMEM),
            pl.BlockSpec((tm, LANE), lambda i: (i, 0)),
            pl.BlockSpec((tm, d), lambda i: (i, 0)),
            pl.BlockSpec((1, N_MOD, d), lambda i: (i // tps, 0, 0)),
            pl.BlockSpec((1, d), lambda i: (0, 0)),
            pl.BlockSpec(memory_space=pl.ANY),
        ],
        out_specs=(pl.BlockSpec((tm, d), lambda i: (jnp.minimum(i, first_tiles - 1), 0)),
                   pl.BlockSpec((tm, d), lambda i: (jnp.maximum(i - first_tiles, 0), 0))),
        out_shape=(jax.ShapeDtypeStruct((first_tiles * tm, d), F32),
                   jax.ShapeDtypeStruct((t - first_tiles * tm, d), F32)),
        scratch_shapes=[pltpu.VMEM((2, TOP_K, tm, 1, d), F32), pltpu.VMEM((tm, d), F32),
                        pltpu.SemaphoreType.DMA((2,))],
        compiler_params=_params(("arbitrary",)),
        name="expert_combine",
    )(dest, dest, gates, x1, mod3, g_post.reshape(1, d), ys)


def _w_in_plan(d, qr, kvr):
    qsw = SWA_Q_HEADS * SWA_HEAD_DIM
    ksw = SWA_KV_HEADS * SWA_HEAD_DIM
    names = ("c_q", "c_kv", "k_rope", "q_s", "k_s", "v_s", "g_a", "g_b")
    sizes = (qr, kvr, MLA_ROPE, qsw, ksw, ksw, d, d)
    src = {}
    off = 0
    for name, s in zip(names, sizes):
        src[name] = (off, s)
        off += s
    order = ("q_s", "g_a", "g_b", "c_q", "c_kv", "k_s", "v_s", "k_rope", "k_rope")
    plan, cols, dst = [], {}, 0
    for name in order:
        s_off, width = src[name]
        cols.setdefault(name, dst)
        plan.append((s_off, width, dst))
        dst += width
    return tuple(plan), cols, dst


def _layout_w_uq(w_uq):
    r = w_uq.shape[0]
    w = w_uq.reshape(r, MLA_HEADS, MLA_NOPE + MLA_ROPE)
    nope = w[:, :, :MLA_NOPE].reshape(r, MLA_HEADS * MLA_NOPE)
    pe = w[:, :, MLA_NOPE:].reshape(r, MLA_HEADS * MLA_ROPE)
    return jnp.concatenate([nope, pe], axis=1).astype(BF16)


def _layout_w_ukv(w_ukv):
    r = w_ukv.shape[0]
    w = w_ukv.reshape(r, MLA_HEADS, MLA_NOPE + MLA_V)
    kn = w[:, :, :MLA_NOPE].reshape(r, MLA_HEADS * MLA_NOPE)
    v = w[:, :, MLA_NOPE:].reshape(r, MLA_HEADS * MLA_V)
    return jnp.concatenate([kn, v], axis=1).astype(BF16)


def _rope_tables(seq):
    half = MLA_ROPE // 2
    freqs = jnp.power(ROPE_THETA, -2.0 * jnp.arange(half, dtype=F32) / MLA_ROPE)
    ang = jnp.arange(seq, dtype=F32)[:, None] * freqs[None, :]
    cos, sin = jnp.cos(ang), jnp.sin(ang)
    return (jnp.concatenate([cos, cos, cos, cos], axis=1),
            jnp.concatenate([-sin, sin, -sin, sin], axis=1))


def kernel(x_prompt, x_sample, c_prompt, c_sample, w_ada, b_ada, g_pre_mix, w_in, g_q_lat, w_uq,
           g_kv_lat, w_ukv, attn_sinks, w_br_mla, w_br_swa, w_out, g_post_mix, g_pre_ffn,
           w_router, b_router, w_gu, b_gu, w_dn, b_dn, g_post_ffn):
    assert MLA_ROPE == SWA_HEAD_DIM == 64 and MLA_NOPE == MLA_V == LANE
    nb_p, seq, d = x_prompt.shape
    nb_s = x_sample.shape[0]
    assert x_sample.shape[1] == seq
    nseq = nb_p + nb_s
    t = nseq * seq
    n_experts = w_router.shape[-1]
    n_blocks = t * TOP_K // BM_EXP + n_experts

    xa, xb = x_prompt.reshape(nb_p * seq, d), x_sample.reshape(nb_s * seq, d)
    c8 = jnp.concatenate([c_prompt, c_sample, jnp.zeros((8 - nseq, d), F32)], axis=0)
    cos, sin = _rope_tables(seq)

    for l in range(w_ada.shape[0]):
        plan, cols, n_cols = _w_in_plan(d, g_q_lat.shape[-1], g_kv_lat.shape[-1])
        w_uq_p = _layout_w_uq(w_uq[l])
        w_ukv_p = _layout_w_ukv(w_ukv[l])
        w_router_p = jnp.pad(w_router[l], ((0, 0), (0, LANE - n_experts))).astype(BF16)
        b_router_p = jnp.pad(b_router[l], (0, LANE - n_experts)).reshape(1, LANE)

        mod, w_in_p = _modulation(c8, w_ada[l], b_ada[l], w_in[l], plan, n_cols)
        mod3 = mod[:nseq].reshape(nseq, N_MOD, d)
        z = _in_projection(xa, xb, mod3, g_pre_mix[l], w_in_p, seq)
        q, k, v, qs, klo, khi, vlo, vhi = _post_projection(
            z, cols, cos, sin, g_q_lat[l], g_kv_lat[l], w_uq_p, w_ukv_p, nseq, seq)
        o_a, w_gu_b, w_dn_b = _mla_attention(q, k, v, w_gu[l], w_dn[l])
        o_b = _swa_attention(attn_sinks[l], qs, klo, khi, vlo, vhi, nseq, seq)
        y = _merge(o_a, o_b, z, cols, w_br_mla[l].astype(BF16), w_br_swa[l].astype(BF16))
        x1, h2, idx, gates, cnt = _out_projection(
            y, xa, xb, mod3, g_post_mix[l], g_pre_ffn[l], w_out[l].astype(BF16), w_router_p, b_router_p,
            n_experts, seq)
        dest, meta, zst = _positions(idx, cnt, n_experts, n_blocks)
        sp = jnp.concatenate([meta[:n_blocks, 0], meta[0:1, 1], meta[:n_blocks, 2]])
        zmeta = jnp.concatenate([zst[0, :n_experts], meta[0:1, 1]])
        xs = _dispatch(zmeta, dest, h2, n_experts, n_blocks)
        ys = _experts(sp, xs, w_gu_b, b_gu[l], w_dn_b, b_dn[l], n_blocks)
        xa, xb = _combine(dest, gates, x1, mod3, g_post_ffn[l], ys, seq, nb_p)

    return (xa.reshape(nb_p, seq, d), xb.reshape(nb_s, seq, d))
```

```python
import functools

import jax
import jax.numpy as jnp
from jax import lax
from jax.experimental import pallas as pl
from jax.experimental.pallas import tpu as pltpu

MLA_HEADS = 16
MLA_NOPE = 128
MLA_ROPE = 64
MLA_V = 128
SWA_Q_HEADS = 32
SWA_KV_HEADS = 8
SWA_HEAD_DIM = 64
WINDOW = 128
TOP_K = 4
SWIGLU_LIMIT = 7.0
SWIGLU_ALPHA = 1.702
ROPE_THETA = 10000.0
RMS_EPS = 1e-6
N_MOD = 6

LANE = 128
VMEM_LIMIT = 56 << 20

TM_IN = 512
TN_IN = 1664
TM_POST = 512
TQ_MLA = 1024
KV_CHUNKS_MLA = (1024, 1024, 1024, 1024)
TQ_SWA = 512
SUB_SWA = 256
TM_MERGE = 1024
TN_MERGE = 1024
TM_OUT = 512
OUT_ROW_SPLITS = 2
TM_POS = 1024
TM_DISPATCH = 1024
TM_DMA = 256
BM_EXP = 512
TF_EXP = 1024

F32 = jnp.float32
BF16 = jnp.bfloat16
NEG_INF = float("-inf")
LOG2E = 1.4426950408889634


def _params(sem):
    return pltpu.CompilerParams(dimension_semantics=sem, vmem_limit_bytes=VMEM_LIMIT)


def _rms(x, g):
    return x * lax.rsqrt(jnp.mean(x * x, axis=-1, keepdims=True) + RMS_EPS) * g


def _mod_kernel(c_ref, w_ref, b_ref, win_ref, o_ref, wout_ref, *, plan, n_slabs):
    c = c_ref[...]
    a = (c * jax.nn.sigmoid(c)).astype(BF16)
    o_ref[...] = jnp.dot(a, w_ref[...].astype(BF16), preferred_element_type=F32) + b_ref[...]

    @pl.when(pl.program_id(0) < n_slabs)
    def _():
        for src, width, dst in plan:
            wout_ref[:, dst:dst + width] = win_ref[:, src:src + width].astype(BF16)


def _modulation(c8, w_ada, b_ada, w_in, plan, n_cols):
    d, n = w_ada.shape
    tn = 1024
    n_steps = n // tn
    n_slabs = 8
    assert n_slabs <= n_steps and d % (8 * n_slabs) == 0
    rows = d // n_slabs
    slab = lambda j: (jnp.minimum(j, n_slabs - 1), 0)
    return pl.pallas_call(
        functools.partial(_mod_kernel, plan=plan, n_slabs=n_slabs),
        grid=(n_steps,),
        in_specs=[
            pl.BlockSpec((8, d), lambda j: (0, 0)),
            pl.BlockSpec((d, tn), lambda j: (0, j)),
            pl.BlockSpec((1, tn), lambda j: (0, j)),
            pl.BlockSpec((rows, w_in.shape[1]), slab),
        ],
        out_specs=(pl.BlockSpec((8, tn), lambda j: (0, j)),
                   pl.BlockSpec((rows, n_cols), slab)),
        out_shape=(jax.ShapeDtypeStruct((8, n), F32),
                   jax.ShapeDtypeStruct((d, n_cols), BF16)),
        compiler_params=_params(("arbitrary",)),
        name="adaln_mod",
    )(c8, w_ada, b_ada.reshape(1, n), w_in)


def _two_group_specs(tm, d, first_tiles, n_grid_axes):
    if n_grid_axes == 1:
        return (pl.BlockSpec((tm, d), lambda i: (jnp.minimum(i, first_tiles - 1), 0)),
                pl.BlockSpec((tm, d), lambda i: (jnp.maximum(i - first_tiles, 0), 0)))
    return (pl.BlockSpec((tm, d), lambda i, j: (jnp.minimum(i, first_tiles - 1), 0)),
            pl.BlockSpec((tm, d), lambda i, j: (jnp.maximum(i - first_tiles, 0), 0)))


def _inproj_kernel(xa_ref, xb_ref, mod_ref, g_ref, w_ref, z_ref, h_ref, *, first_tiles):
    def normalise(x_ref):
        x = x_ref[...]
        gain = g_ref[...] * (1.0 + mod_ref[0, 1:2, :])
        h = x * lax.rsqrt(jnp.mean(x * x, axis=-1, keepdims=True) + RMS_EPS) * gain + mod_ref[0, 0:1, :]
        h_ref[...] = h.astype(BF16)

    first_col = pl.program_id(1) == 0
    in_first = pl.program_id(0) < first_tiles
    pl.when(first_col & in_first)(functools.partial(normalise, xa_ref))
    pl.when(first_col & jnp.logical_not(in_first))(functools.partial(normalise, xb_ref))

    z_ref[...] = jnp.dot(h_ref[...], w_ref[...], preferred_element_type=F32).astype(z_ref.dtype)


def _in_projection(xa, xb, mod3, g, w_in_p, seq):
    d = xa.shape[1]
    t = xa.shape[0] + xb.shape[0]
    n = w_in_p.shape[1]
    tiles_per_seq = seq // TM_IN
    first_tiles = xa.shape[0] // TM_IN
    return pl.pallas_call(
        functools.partial(_inproj_kernel, first_tiles=first_tiles),
        grid=(t // TM_IN, n // TN_IN),
        in_specs=[
            *_two_group_specs(TM_IN, d, first_tiles, 2),
            pl.BlockSpec((1, N_MOD, d), lambda i, j: (i // tiles_per_seq, 0, 0)),
            pl.BlockSpec((1, d), lambda i, j: (0, 0)),
            pl.BlockSpec((d, TN_IN), lambda i, j: (0, j)),
        ],
        out_specs=pl.BlockSpec((TM_IN, TN_IN), lambda i, j: (i, j)),
        out_shape=jax.ShapeDtypeStruct((t, n), BF16),
        scratch_shapes=[pltpu.VMEM((TM_IN, d), BF16)],
        compiler_params=_params(("arbitrary", "arbitrary")),
        name="in_projection",
    )(xa, xb, mod3, g.reshape(1, d), w_in_p)


def _postproj_kernel(cq_ref, ckv_ref, qs_ref, ks_ref, vs_ref, kr_ref, cos_ref, sin_ref,
                     gq_ref, gkv_ref, wuq_ref, wukv_ref,
                     q_ref, k_ref, v_ref, qso_ref, klo_ref, khi_ref, vlo_ref, vhi_ref):
    tm = cq_ref.shape[0]
    cos = cos_ref[...]
    sin = sin_ref[...]
    lane = lax.broadcasted_iota(jnp.int32, (tm, LANE), 1)
    first_half = (lane & 63) < 32
    low = lane < 64

    def rope(x, cos=cos, sin=sin):
        rot = jnp.where(first_half, pltpu.roll(x, LANE - 32, 1), pltpu.roll(x, 32, 1))
        return x * cos + rot * sin

    nh = MLA_HEADS
    scale = float((MLA_NOPE + MLA_ROPE) ** -0.5) * LOG2E
    cqn = (_rms(cq_ref[...].astype(F32), gq_ref[...]) * scale).astype(BF16)
    q = jnp.dot(cqn, wuq_ref[...], preferred_element_type=F32)
    for h in range(nh):
        q_ref[0, h, :, 0:LANE] = q[:, h * LANE:(h + 1) * LANE].astype(BF16)
    for m in range(nh // 2):
        pe = rope(q[:, (nh + m) * LANE:(nh + m + 1) * LANE]).astype(BF16)
        q_ref[0, 2 * m, :, LANE:2 * LANE] = pe
        q_ref[0, 2 * m + 1, :, LANE:2 * LANE] = pe

    ckvn = _rms(ckv_ref[...].astype(F32), gkv_ref[...]).astype(BF16)
    kv = jnp.dot(ckvn, wukv_ref[...], preferred_element_type=F32)
    kr = rope(kr_ref[...].astype(F32))
    kpe_lo = jnp.where(low, kr, 0.0).astype(BF16)
    kpe_hi = jnp.where(low, 0.0, kr).astype(BF16)
    for h in range(nh):
        k_ref[0, h, :, 0:LANE] = kv[:, h * LANE:(h + 1) * LANE].astype(BF16)
        k_ref[0, h, :, LANE:2 * LANE] = kpe_lo if h % 2 == 0 else kpe_hi
        v_ref[0, h, :, :] = kv[:, (nh + h) * LANE:(nh + h + 1) * LANE].astype(BF16)

    swa_scale = float(SWA_HEAD_DIM ** -0.5) * LOG2E
    cos_q, sin_q = cos * swa_scale, sin * swa_scale
    for m in range(SWA_Q_HEADS // 2):
        x = qs_ref[:, m * LANE:(m + 1) * LANE].astype(F32)
        qso_ref[:, m * LANE:(m + 1) * LANE] = rope(x, cos_q, sin_q).astype(BF16)

    for m in range(SWA_KV_HEADS // 2):
        sl = slice(m * LANE, (m + 1) * LANE)
        for src_ref, lo_ref, hi_ref, roped in ((ks_ref, klo_ref, khi_ref, True),
                                               (vs_ref, vlo_ref, vhi_ref, False)):
            a = src_ref[:, sl].astype(F32)
            if roped:
                a = rope(a)
            b = pltpu.roll(a, 64, 1)
            e0 = slice((2 * m) * LANE, (2 * m + 1) * LANE)
            e1 = slice((2 * m + 1) * LANE, (2 * m + 2) * LANE)
            lo_ref[:, e0] = jnp.where(low, a, 0.0).astype(BF16)
            hi_ref[:, e0] = jnp.where(low, 0.0, b).astype(BF16)
            lo_ref[:, e1] = jnp.where(low, b, 0.0).astype(BF16)
            hi_ref[:, e1] = jnp.where(low, 0.0, a).astype(BF16)


def _post_projection(z, cols, cos, sin, g_q, g_kv, w_uq_p, w_ukv_p, nseq, seq):
    t = z.shape[0]
    tm = TM_POST
    tps = seq // tm
    nh = MLA_HEADS
    qr, kvr = g_q.shape[0], g_kv.shape[0]
    qsw = SWA_Q_HEADS * SWA_HEAD_DIM
    ksw = SWA_KV_HEADS * SWA_HEAD_DIM

    def zspec(width, off):
        blk = off // width
        return pl.BlockSpec((tm, width), lambda i: (i, blk))

    head_map = lambda i: (i // tps, 0, i % tps, 0)
    tok_map = lambda i: (i, 0)
    out_shapes = (
        jax.ShapeDtypeStruct((nseq, nh, seq, 2 * LANE), BF16),
        jax.ShapeDtypeStruct((nseq, nh, seq, 2 * LANE), BF16),
        jax.ShapeDtypeStruct((nseq, nh, seq, LANE), BF16),
        jax.ShapeDtypeStruct((t, qsw), BF16),
        jax.ShapeDtypeStruct((t, SWA_KV_HEADS * LANE), BF16),
        jax.ShapeDtypeStruct((t, SWA_KV_HEADS * LANE), BF16),
        jax.ShapeDtypeStruct((t, SWA_KV_HEADS * LANE), BF16),
        jax.ShapeDtypeStruct((t, SWA_KV_HEADS * LANE), BF16),
    )
    out_specs = (
        pl.BlockSpec((1, nh, tm, 2 * LANE), head_map),
        pl.BlockSpec((1, nh, tm, 2 * LANE), head_map),
        pl.BlockSpec((1, nh, tm, LANE), head_map),
        pl.BlockSpec((tm, qsw), tok_map),
        pl.BlockSpec((tm, SWA_KV_HEADS * LANE), tok_map),
        pl.BlockSpec((tm, SWA_KV_HEADS * LANE), tok_map),
        pl.BlockSpec((tm, SWA_KV_HEADS * LANE), tok_map),
        pl.BlockSpec((tm, SWA_KV_HEADS * LANE), tok_map),
    )
    return pl.pallas_call(
        _postproj_kernel,
        grid=(t // tm,),
        in_specs=[
            zspec(qr, cols["c_q"]), zspec(kvr, cols["c_kv"]), zspec(qsw, cols["q_s"]),
            zspec(ksw, cols["k_s"]), zspec(ksw, cols["v_s"]), zspec(LANE, cols["k_rope"]),
            pl.BlockSpec((tm, LANE), lambda i: (i % tps, 0)),
            pl.BlockSpec((tm, LANE), lambda i: (i % tps, 0)),
            pl.BlockSpec((1, qr), lambda i: (0, 0)),
            pl.BlockSpec((1, kvr), lambda i: (0, 0)),
            pl.BlockSpec(w_uq_p.shape, lambda i: (0, 0), pipeline_mode=pl.Buffered(1)),
            pl.BlockSpec(w_ukv_p.shape, lambda i: (0, 0), pipeline_mode=pl.Buffered(1)),
        ],
        out_specs=out_specs,
        out_shape=out_shapes,
        compiler_params=_params(("arbitrary",)),
        name="post_projection",
    )(z, z, z, z, z, z, cos, sin, g_q.reshape(1, qr), g_kv.reshape(1, kvr), w_uq_p, w_ukv_p)


def _mla_kernel(q_ref, k_ref, v_ref, wgu_ref, wdn_ref, o_ref, wgu_o_ref, wdn_o_ref, *, n_gu):
    step = (pl.program_id(0) * pl.num_programs(1) + pl.program_id(1)) * pl.num_programs(2) + pl.program_id(2)

    @pl.when(step < n_gu)
    def _():
        wgu_o_ref[...] = wgu_ref[...].astype(BF16)

    @pl.when(step >= n_gu)
    def _():
        wdn_o_ref[...] = wdn_ref[...].astype(BF16)

    q = q_ref[0, 0]
    seq = k_ref.shape[2]
    m = l = acc = None
    assert sum(KV_CHUNKS_MLA) == seq
    start = 0
    for width in KV_CHUNKS_MLA:
        rows = slice(start, start + width)
        start += width
        s = lax.dot_general(q, k_ref[0, 0, rows, :], (((1,), (1,)), ((), ())),
                            preferred_element_type=F32)
        mc = jnp.max(s, axis=-1, keepdims=True)
        m_new = mc if m is None else jnp.maximum(m, mc)
        p = jnp.exp2(s - m_new)
        ps = jnp.sum(p, axis=-1, keepdims=True)
        pv = jnp.dot(p.astype(BF16), v_ref[0, 0, rows, :], preferred_element_type=F32)
        if m is None:
            l, acc = ps, pv
        else:
            alpha = jnp.exp2(m - m_new)
            l = alpha * l + ps
            acc = alpha * acc + pv
        m = m_new
    o_ref[...] = (acc / l).astype(o_ref.dtype)


def _mla_attention(q, k, v, w_gu, w_dn):
    nseq, nh, seq, dqk = q.shape
    dv = v.shape[-1]
    nq = seq // TQ_MLA
    n_steps = nseq * nh * nq
    gu2d = w_gu.reshape(-1, w_gu.shape[-1])
    dn2d = w_dn.reshape(-1, w_dn.shape[-1])
    n_gu, n_dn = 2 * n_steps // 3, n_steps // 3
    assert n_gu + n_dn == n_steps and gu2d.shape[0] % n_gu == 0 and dn2d.shape[0] % n_dn == 0
    gu_rows, dn_rows = gu2d.shape[0] // n_gu, dn2d.shape[0] // n_dn
    assert gu_rows % 16 == 0 and dn_rows % 16 == 0

    def step(b, h, i):
        return (b * nh + h) * nq + i

    gu_map = lambda b, h, i: (jnp.minimum(step(b, h, i), n_gu - 1), 0)
    dn_map = lambda b, h, i: (jnp.maximum(step(b, h, i) - n_gu, 0), 0)
    o, gu_b, dn_b = pl.pallas_call(
        functools.partial(_mla_kernel, n_gu=n_gu),
        grid=(nseq, nh, nq),
        in_specs=[
            pl.BlockSpec((1, 1, TQ_MLA, dqk), lambda b, h, i: (b, h, i, 0)),
            pl.BlockSpec((1, 1, seq, dqk), lambda b, h, i: (b, h, 0, 0)),
            pl.BlockSpec((1, 1, seq, dv), lambda b, h, i: (b, h, 0, 0)),
            pl.BlockSpec((gu_rows, gu2d.shape[1]), gu_map),
            pl.BlockSpec((dn_rows, dn2d.shape[1]), dn_map),
        ],
        out_specs=(
            pl.BlockSpec((TQ_MLA, dv), lambda b, h, i: (b * nq + i, h)),
            pl.BlockSpec((gu_rows, gu2d.shape[1]), gu_map),
            pl.BlockSpec((dn_rows, dn2d.shape[1]), dn_map),
        ),
        out_shape=(
            jax.ShapeDtypeStruct((nseq * seq, nh * dv), BF16),
            jax.ShapeDtypeStruct(gu2d.shape, BF16),
            jax.ShapeDtypeStruct(dn2d.shape, BF16),
        ),
        compiler_params=_params(("arbitrary", "arbitrary", "arbitrary")),
        name="mla_attention",
    )(q, k, v, gu2d, dn2d)
    return o, gu_b.reshape(w_gu.shape), dn_b.reshape(w_dn.shape)


def _swa_kernel(sink_ref, q_ref,
                klo_p, klo_c, klo_n, khi_p, khi_c, khi_n,
                vlo_p, vlo_c, vlo_n, vhi_p, vhi_c, vhi_n,
                o_ref, klo_w, khi_w, vlo_w, vhi_w, *, seq):
    tb = q_ref.shape[0]
    tq = SUB_SWA
    w = WINDOW
    for win, (p, c, n) in ((klo_w, (klo_p, klo_c, klo_n)), (khi_w, (khi_p, khi_c, khi_n)),
                           (vlo_w, (vlo_p, vlo_c, vlo_n)), (vhi_w, (vhi_p, vhi_c, vhi_n))):
        win[0:w, :] = p[...]
        win[w:w + tb, :] = c[...]
        win[w + tb:tb + 2 * w, :] = n[...]

    assert SWA_Q_HEADS // SWA_KV_HEADS == 4 and tq & (tq - 1) == 0 and tb % tq == 0
    nk = tq + 2 * w
    row2 = lax.broadcasted_iota(jnp.int32, (2 * tq, nk), 0)
    col = lax.broadcasted_iota(jnp.int32, (2 * tq, nk), 1)
    rel = col - w - (row2 & (tq - 1))
    band = (rel <= w) & (rel >= -w)
    top = lax.broadcasted_iota(jnp.int32, (2 * tq, 1), 0) < tq

    for sub in range(tb // tq):
        qrows = slice(sub * tq, (sub + 1) * tq)
        krows = slice(sub * tq, sub * tq + nk)
        kpos = pl.program_id(1) * tb + sub * tq - w + col
        valid = band & (kpos >= 0) & (kpos < seq)
        for g in range(SWA_KV_HEADS):
            gs = slice(g * LANE, (g + 1) * LANE)
            pa = slice((2 * g) * LANE, (2 * g + 1) * LANE)
            pb = slice((2 * g + 1) * LANE, (2 * g + 2) * LANE)
            qq = jnp.concatenate([q_ref[qrows, pa], q_ref[qrows, pb]], axis=0)
            acc = None
            for half, (kw, vw) in enumerate(((klo_w, vlo_w), (khi_w, vhi_w))):
                sink = jnp.where(top, sink_ref[4 * g + half], sink_ref[4 * g + 2 + half]) * LOG2E
                s = lax.dot_general(qq, kw[krows, gs], (((1,), (1,)), ((), ())),
                                    preferred_element_type=F32)
                s = jnp.where(valid, s, NEG_INF)
                mx = jnp.maximum(jnp.max(s, axis=-1, keepdims=True), sink)
                e = jnp.exp2(s - mx)
                den = jnp.sum(e, axis=-1, keepdims=True) + jnp.exp2(sink - mx)
                pv = jnp.dot(e.astype(BF16), vw[krows, gs], preferred_element_type=F32) / den
                acc = pv if acc is None else acc + pv
            o_ref[qrows, pa] = acc[0:tq].astype(o_ref.dtype)
            o_ref[qrows, pb] = acc[tq:2 * tq].astype(o_ref.dtype)


def _swa_attention(sinks, qs, klo, khi, vlo, vhi, nseq, seq):
    t, qw = qs.shape
    kw = klo.shape[1]
    tq = TQ_SWA
    nq = seq // tq
    r = tq // WINDOW
    nwb = seq // WINDOW

    prev = pl.BlockSpec((WINDOW, kw), lambda b, i, s: (b * nwb + jnp.maximum(i * r - 1, 0), 0))
    cur = pl.BlockSpec((tq, kw), lambda b, i, s: (b * nq + i, 0))
    nxt = pl.BlockSpec((WINDOW, kw), lambda b, i, s: (b * nwb + jnp.minimum((i + 1) * r, nwb - 1), 0))
    grid_spec = pltpu.PrefetchScalarGridSpec(
        num_scalar_prefetch=1,
        grid=(nseq, nq),
        in_specs=[pl.BlockSpec((tq, qw), lambda b, i, s: (b * nq + i, 0))] + [prev, cur, nxt] * 4,
        out_specs=pl.BlockSpec((tq, qw), lambda b, i, s: (b * nq + i, 0)),
        scratch_shapes=[pltpu.VMEM((tq + 2 * WINDOW, kw), BF16)] * 4,
    )
    return pl.pallas_call(
        functools.partial(_swa_kernel, seq=seq),
        grid_spec=grid_spec,
        out_shape=jax.ShapeDtypeStruct((t, qw), BF16),
        compiler_params=_params(("arbitrary", "arbitrary")),
        name="swa_attention",
    )(sinks, qs, klo, klo, klo, khi, khi, khi, vlo, vlo, vlo, vhi, vhi, vhi)


def _merge_kernel(oa_ref, ob_ref, ga_ref, gb_ref, wa_ref, wb_ref, y_ref):
    a = jnp.dot(oa_ref[...], wa_ref[...], preferred_element_type=F32)
    b = jnp.dot(ob_ref[...], wb_ref[...], preferred_element_type=F32)
    y = jax.nn.sigmoid(ga_ref[...].astype(F32)) * a + jax.nn.sigmoid(gb_ref[...].astype(F32)) * b
    y_ref[...] = y.astype(y_ref.dtype)


def _merge(o_a, o_b, z, cols, w_a, w_b):
    t, d = o_a.shape[0], w_a.shape[1]
    tm, tn = TM_MERGE, TN_MERGE
    ga_blk, gb_blk = cols["g_a"] // tn, cols["g_b"] // tn
    return pl.pallas_call(
        _merge_kernel,
        grid=(t // tm, d // tn),
        in_specs=[
            pl.BlockSpec((tm, o_a.shape[1]), lambda i, j: (i, 0)),
            pl.BlockSpec((tm, o_b.shape[1]), lambda i, j: (i, 0)),
            pl.BlockSpec((tm, tn), lambda i, j: (i, ga_blk + j)),
            pl.BlockSpec((tm, tn), lambda i, j: (i, gb_blk + j)),
            pl.BlockSpec((w_a.shape[0], tn), lambda i, j: (0, j)),
            pl.BlockSpec((w_b.shape[0], tn), lambda i, j: (0, j)),
        ],
        out_specs=pl.BlockSpec((tm, tn), lambda i, j: (i, j)),
        out_shape=jax.ShapeDtypeStruct((t, d), BF16),
        compiler_params=_params(("arbitrary", "arbitrary")),
        name="branch_merge",
    )(o_a, o_b, z, z, w_a, w_b)


def _outproj_kernel(y_ref, xa_ref, xb_ref, mod_ref, gpm_ref, gpf_ref, wo_ref, wr_ref, br_ref,
                    x1_ref, h2_ref, idx_ref, gate_ref, cnt_ref, pack_ref, *, n_experts, first_tiles):
    tm = y_ref.shape[0]
    u_all = jnp.dot(y_ref[...], wo_ref[...], preferred_element_type=F32)
    in_first = pl.program_id(0) < first_tiles
    hm = tm // OUT_ROW_SPLITS
    lane = lax.broadcasted_iota(jnp.int32, (hm, LANE), 1)
    lane_f = lane.astype(F32)
    counts = None
    for part in range(OUT_ROW_SPLITS):
        rows = slice(part * hm, (part + 1) * hm)
        x = jnp.where(in_first, xa_ref[rows, :], xb_ref[rows, :])
        x1 = x + mod_ref[0, 2:3, :] * _rms(u_all[rows], gpm_ref[...])
        x1_ref[rows, :] = x1
        h2 = _rms(x1, gpf_ref[...]) * (1.0 + mod_ref[0, 4:5, :]) + mod_ref[0, 3:4, :]
        h2b = h2.astype(BF16)
        bits = pltpu.bitcast(h2b.astype(F32), jnp.uint32)
        half = bits.shape[1] // 2
        pack_ref[rows, :] = (bits[:, :half] >> 16) | (bits[:, half:] & jnp.uint32(0xFFFF0000))
        h2_ref[rows] = pack_ref[rows, :].reshape(hm, 1, half)

        logits = jnp.dot(h2b, wr_ref[...], preferred_element_type=F32) + br_ref[...]
        cur = jnp.where(lane < n_experts, logits, NEG_INF)
        vals, idxs = [], []
        for _ in range(TOP_K):
            m = jnp.max(cur, axis=-1, keepdims=True)
            ix = jnp.min(jnp.where(cur == m, lane_f, float(LANE)), axis=-1, keepdims=True)
            vals.append(m)
            idxs.append(ix)
            cur = jnp.where(lane_f == ix, NEG_INF, cur)
        es = [jnp.exp(v - vals[0]) for v in vals]
        den = es[0]
        for e in es[1:]:
            den = den + e
        idx_out = jnp.zeros((hm, LANE), F32)
        gate_out = jnp.zeros((hm, LANE), F32)
        sel = jnp.zeros((hm, LANE), F32)
        for r in range(TOP_K):
            idx_out = jnp.where(lane == r, idxs[r], idx_out)
            gate_out = jnp.where(lane == r, es[r] / den, gate_out)
            sel = sel + jnp.where(lane_f == idxs[r], 1.0, 0.0)
        idx_ref[rows, :] = idx_out.astype(jnp.int32)
        gate_ref[rows, :] = gate_out
        part_counts = jnp.sum(sel, axis=0, keepdims=True)
        counts = part_counts if counts is None else counts + part_counts

    @pl.when(pl.program_id(0) == 0)
    def _():
        cnt_ref[...] = jnp.zeros_like(cnt_ref)

    cnt_ref[0:1, :] += counts


def _out_projection(y, xa, xb, mod3, g_pm, g_pf, w_out, w_router_p, b_router_p, n_experts, seq):
    t, d = y.shape
    tm = TM_OUT
    tps = seq // tm
    first_tiles = xa.shape[0] // tm
    tok = lambda i: (i, 0)
    const = lambda i: (0, 0)
    return pl.pallas_call(
        functools.partial(_outproj_kernel, n_experts=n_experts, first_tiles=first_tiles),
        grid=(t // tm,),
        in_specs=[
            pl.BlockSpec((tm, d), tok),
            *_two_group_specs(tm, d, first_tiles, 1),
            pl.BlockSpec((1, N_MOD, d), lambda i: (i // tps, 0, 0)),
            pl.BlockSpec((1, d), const),
            pl.BlockSpec((1, d), const),
            pl.BlockSpec((d, d), const),
            pl.BlockSpec((d, LANE), const),
            pl.BlockSpec((1, LANE), const),
        ],
        out_specs=(
            pl.BlockSpec((tm, d), tok),
            pl.BlockSpec((tm, 1, d // 2), lambda i: (i, 0, 0)),
            pl.BlockSpec((tm, LANE), tok),
            pl.BlockSpec((tm, LANE), tok),
            pl.BlockSpec((8, LANE), const),
        ),
        out_shape=(
            jax.ShapeDtypeStruct((t, d), F32),
            jax.ShapeDtypeStruct((t, 1, d // 2), jnp.uint32),
            jax.ShapeDtypeStruct((t, LANE), jnp.int32),
            jax.ShapeDtypeStruct((t, LANE), F32),
            jax.ShapeDtypeStruct((8, LANE), F32),
        ),
        scratch_shapes=[pltpu.VMEM((tm, d // 2), jnp.uint32)],
        compiler_params=_params(("arbitrary",)),
        name="out_projection_router",
    )(y, xa, xb, mod3, g_pm.reshape(1, d), g_pf.reshape(1, d), w_out, w_router_p, b_router_p)


def _positions_kernel(idx_ref, cnt_ref, dest_ref, meta_ref, zst_ref, carry_ref, ltri_ref, pst_ref,
                      *, n_experts, bm):
    tm = idx_ref.shape[0]
    nbp = meta_ref.shape[0]

    @pl.when(pl.program_id(0) == 0)
    def _():
        r = lax.broadcasted_iota(jnp.int32, (tm, tm), 0)
        c = lax.broadcasted_iota(jnp.int32, (tm, tm), 1)
        ltri_ref[...] = jnp.where(c < r, 1.0, 0.0).astype(BF16)
        carry_ref[...] = jnp.zeros_like(carry_ref)
        nblk = jnp.floor((cnt_ref[...] + float(bm - 1)) * (1.0 / bm))
        ur = lax.broadcasted_iota(jnp.int32, (LANE, LANE), 0)
        uc = lax.broadcasted_iota(jnp.int32, (LANE, LANE), 1)
        upper = jnp.where(ur <= uc, 1.0, 0.0).astype(BF16)
        pend = jnp.dot(nblk.astype(BF16), upper, preferred_element_type=F32)
        pst_ref[...] = (pend - nblk) * float(bm)
        lane8 = lax.broadcasted_iota(jnp.int32, (8, LANE), 1)
        zst_ref[...] = jnp.maximum(pend * float(bm) - float(bm), 0.0).astype(jnp.int32)
        pend0 = pend[0:1, :]
        n_used = jnp.sum(jnp.where(lane8[0:1, :] == n_experts - 1, pend0, 0.0), axis=-1, keepdims=True)
        blk = lax.broadcasted_iota(jnp.int32, (nbp, LANE), 0).astype(F32)
        lane = lax.broadcasted_iota(jnp.int32, (nbp, LANE), 1)
        passed = jnp.where((lane < n_experts) & (pend0 <= blk), 1.0, 0.0)
        blk_e = jnp.minimum(jnp.sum(passed, axis=-1, keepdims=True), float(n_experts - 1))
        mine = lane.astype(F32) == blk_e
        e_count = jnp.sum(jnp.where(mine, cnt_ref[0:1, :], 0.0), axis=-1, keepdims=True)
        e_first = jnp.sum(jnp.where(mine, pend0 - nblk[0:1, :], 0.0), axis=-1, keepdims=True)
        n_valid = jnp.clip(e_count - (blk[:, 0:1] - e_first) * float(bm), 0.0, float(bm))
        meta = jnp.where(lane == 0, blk_e, jnp.where(lane == 1, n_used, jnp.where(lane == 2, n_valid, 0.0)))
        meta_ref[...] = meta.astype(jnp.int32)

    lane = lax.broadcasted_iota(jnp.int32, (tm, LANE), 1)
    idx = idx_ref[...]
    hots = []
    sel = jnp.zeros((tm, LANE), F32)
    for r in range(TOP_K):
        col = jnp.sum(jnp.where(lane == r, idx, 0).astype(F32), axis=-1, keepdims=True)
        hot = jnp.where(lane.astype(F32) == col, 1.0, 0.0)
        hots.append(hot)
        sel = sel + hot
    rank = jnp.dot(ltri_ref[...], sel.astype(BF16), preferred_element_type=F32) + carry_ref[0:1, :]
    pos = pst_ref[0:1, :] + rank
    dest = jnp.zeros((tm, LANE), F32)
    for r in range(TOP_K):
        d = jnp.sum(hots[r] * pos, axis=-1, keepdims=True)
        dest = jnp.where(lane == r, d, dest)
    dest_ref[...] = dest.T[0:8, :].astype(jnp.int32)
    carry_ref[0:1, :] += jnp.sum(sel, axis=0, keepdims=True)


def _positions(idx, cnt, n_experts, n_blocks):
    t = idx.shape[0]
    tm = TM_POS
    nbp = -(-n_blocks // 8) * 8
    return pl.pallas_call(
        functools.partial(_positions_kernel, n_experts=n_experts, bm=BM_EXP),
        grid=(t // tm,),
        in_specs=[pl.BlockSpec((tm, LANE), lambda i: (i, 0)),
                  pl.BlockSpec((8, LANE), lambda i: (0, 0))],
        out_specs=(pl.BlockSpec((8, tm), lambda i: (0, i)),
                   pl.BlockSpec((nbp, LANE), lambda i: (0, 0)),
                   pl.BlockSpec((8, LANE), lambda i: (0, 0))),
        out_shape=(jax.ShapeDtypeStruct((8, t), jnp.int32),
                   jax.ShapeDtypeStruct((nbp, LANE), jnp.int32),
                   jax.ShapeDtypeStruct((8, LANE), jnp.int32)),
        scratch_shapes=[pltpu.VMEM((8, LANE), F32), pltpu.VMEM((tm, tm), BF16), pltpu.VMEM((8, LANE), F32)],
        compiler_params=_params(("arbitrary",)),
        name="dispatch_positions",
    )(idx, cnt)


def _dispatch_kernel(zst_ref, dest_ref, h2_ref, xs_ref, zero_ref, zsem, sem, *, n_experts, bm):
    i = pl.program_id(0)
    tm = dest_ref.shape[1]

    def zero_copy(e):
        return pltpu.make_async_copy(zero_ref, xs_ref.at[pl.ds(zst_ref[e], bm)], zsem)

    def tail_copy(j):
        return pltpu.make_async_copy(zero_ref, xs_ref.at[pl.ds(j * bm, bm)], zsem)

    @pl.when(i == 0)
    def _():
        zero_ref[...] = jnp.zeros_like(zero_ref)
        n_used = zst_ref[n_experts]
        n_blocks = xs_ref.shape[0] // bm
        for e in range(n_experts):
            zero_copy(e).start()
        lax.fori_loop(n_used, n_blocks, lambda j, c: (tail_copy(j).start(), c)[1], 0)
        for e in range(n_experts):
            zero_copy(e).wait()
        lax.fori_loop(n_used, n_blocks, lambda j, c: (tail_copy(j).wait(), c)[1], 0)

    def body(j, carry):
        for r in range(TOP_K):
            d = dest_ref[r, j]
            pltpu.make_async_copy(h2_ref.at[j], xs_ref.at[d], sem).start(priority=r % 2)
        return carry

    lax.fori_loop(0, tm, body, 0, unroll=8)
    for _ in range(TOP_K):
        pltpu.make_async_copy(h2_ref, xs_ref.at[pl.ds(0, tm)], sem).wait()


def _dispatch(zst, dest, h2, n_experts, n_blocks):
    t, _, d = h2.shape
    grid_spec = pltpu.PrefetchScalarGridSpec(
        num_scalar_prefetch=1,
        grid=(t // TM_DISPATCH,),
        in_specs=[pl.BlockSpec((8, TM_DISPATCH), lambda i, z: (0, i), memory_space=pltpu.SMEM),
                  pl.BlockSpec((TM_DISPATCH, 1, d), lambda i, z: (i, 0, 0))],
        out_specs=pl.BlockSpec(memory_space=pl.ANY),
        scratch_shapes=[pltpu.VMEM((BM_EXP, 1, d), h2.dtype),
                        pltpu.SemaphoreType.DMA(()), pltpu.SemaphoreType.DMA(())],
    )
    return pl.pallas_call(
        functools.partial(_dispatch_kernel, n_experts=n_experts, bm=BM_EXP),
        grid_spec=grid_spec,
        out_shape=jax.ShapeDtypeStruct((n_blocks * BM_EXP, 1, d), h2.dtype),
        compiler_params=_params(("arbitrary",)),
        name="row_dispatch",
    )(zst, dest, h2)


def _expert_kernel(sp_ref, xs_ref, wg_ref, wl_ref, bg_ref, bl_ref, wd_ref, bd_ref, ys_ref,
                   x2d_ref, xb_ref, acc2_ref, in_sems, sems, *, n_blocks):
    j = pl.program_id(0)
    f = pl.program_id(1)
    nf = pl.num_programs(1)
    n_used = sp_ref[n_blocks]
    used = j < n_used
    bm = acc2_ref.shape[1]
    slot = j % 2
    acc_ref = acc2_ref.at[slot]

    def in_copy(blk, s):
        src = jnp.minimum(blk, n_used - 1)
        return pltpu.make_async_copy(xs_ref.at[pl.ds(src * bm, bm), 0, :], x2d_ref.at[s], in_sems.at[s])

    def out_copy(blk, s):
        return pltpu.make_async_copy(acc2_ref.at[s], ys_ref.at[pl.ds(blk * bm, bm), 0, :], sems.at[s])

    @pl.when((f == 0) & (j == 0))
    def _():
        in_copy(0, 0).start()

    @pl.when((f == 0) & (j + 1 < n_blocks))
    def _():
        in_copy(j + 1, 1 - slot).start()

    @pl.when(f == 0)
    def _():
        in_copy(j, slot).wait()

    @pl.when((f == 0) & (j >= 2))
    def _():
        out_copy(j - 2, slot).wait()

    @pl.when(used & (f == 0))
    def _():
        words = x2d_ref[slot]
        half = words.shape[1]
        xb_ref[:, :half] = pltpu.bitcast(words << 16, F32).astype(BF16)
        xb_ref[:, half:] = pltpu.bitcast(words & jnp.uint32(0xFFFF0000), F32).astype(BF16)

    @pl.when(jnp.logical_not(used) & (f == 0))
    def _():
        acc_ref[...] = jnp.zeros((bm, acc2_ref.shape[2]), F32)

    def mlp(rows):
        xb = xb_ref[0:rows]
        glu = jnp.dot(xb, wg_ref[0], preferred_element_type=F32) + bg_ref[0]
        lin = jnp.dot(xb, wl_ref[0], preferred_element_type=F32) + bl_ref[0]
        glu = jnp.minimum(glu, SWIGLU_LIMIT)
        lin = jnp.clip(lin, -SWIGLU_LIMIT, SWIGLU_LIMIT)
        act = glu * jax.nn.sigmoid(SWIGLU_ALPHA * glu) * (lin + 1.0)
        prev = jnp.where(f == 0, jnp.broadcast_to(bd_ref[0], (rows, acc2_ref.shape[2])), acc_ref[0:rows])
        acc_ref[0:rows] = prev + jnp.dot(act.astype(BF16), wd_ref[0], preferred_element_type=F32)

    sparse = sp_ref[n_blocks + 1 + jnp.minimum(j, n_blocks - 1)] <= bm // 2

    @pl.when(used & jnp.logical_not(sparse))
    def _():
        mlp(bm)

    @pl.when(used & sparse)
    def _():
        mlp(bm // 2)
        acc_ref[bm // 2:bm] = jnp.zeros((bm - bm // 2, acc2_ref.shape[2]), F32)

    @pl.when(f == nf - 1)
    def _():
        out_copy(j, slot).start()

    @pl.when((f == nf - 1) & (j == n_blocks - 1))
    def _():
        out_copy(j - 1, 1 - slot).wait()
        out_copy(j, slot).wait()


def _experts(sp, xs, w_gu, b_gu, w_dn, b_dn, n_blocks):
    p, _, dh = xs.shape
    n_experts, d, ff2 = w_gu.shape
    assert d == 2 * dh
    ff = ff2 // 2
    nf = ff // TF_EXP
    bm = BM_EXP

    def blk(j, s):
        return jnp.minimum(j, s[n_blocks] - 1)

    def fch(j, f, s):
        return jnp.where(j < s[n_blocks], f, nf - 1)

    grid_spec = pltpu.PrefetchScalarGridSpec(
        num_scalar_prefetch=1,
        grid=(n_blocks, nf),
        in_specs=[
            pl.BlockSpec(memory_space=pl.ANY),
            pl.BlockSpec((1, d, TF_EXP), lambda j, f, s: (s[blk(j, s)], 0, fch(j, f, s))),
            pl.BlockSpec((1, d, TF_EXP), lambda j, f, s: (s[blk(j, s)], 0, nf + fch(j, f, s))),
            pl.BlockSpec((1, 1, TF_EXP), lambda j, f, s: (s[blk(j, s)], 0, fch(j, f, s))),
            pl.BlockSpec((1, 1, TF_EXP), lambda j, f, s: (s[blk(j, s)], 0, nf + fch(j, f, s))),
            pl.BlockSpec((1, TF_EXP, d), lambda j, f, s: (s[blk(j, s)], fch(j, f, s), 0)),
            pl.BlockSpec((1, 1, d), lambda j, f, s: (s[blk(j, s)], 0, 0)),
        ],
        out_specs=pl.BlockSpec(memory_space=pl.ANY),
        scratch_shapes=[pltpu.VMEM((2, bm, dh), jnp.uint32), pltpu.VMEM((bm, d), BF16),
                        pltpu.VMEM((2, bm, d), F32),
                        pltpu.SemaphoreType.DMA((2,)), pltpu.SemaphoreType.DMA((2,))],
    )
    return pl.pallas_call(
        functools.partial(_expert_kernel, n_blocks=n_blocks),
        grid_spec=grid_spec,
        out_shape=jax.ShapeDtypeStruct((p, 1, d), F32),
        compiler_params=_params(("arbitrary", "arbitrary")),
        name="expert_mlp",
    )(sp, xs, w_gu, w_gu, b_gu.reshape(n_experts, 1, ff2), b_gu.reshape(n_experts, 1, ff2),
      w_dn, b_dn.reshape(n_experts, 1, d))


def _combine_kernel(dest_ref, dest_next_ref, gate_ref, x1_ref, mod_ref, g_ref, ys_ref, oa_ref, ob_ref,
                    buf_ref, row_ref, sems, *, first_tiles):
    tm = x1_ref.shape[0]
    i = pl.program_id(0)
    slot = i % 2

    def gather(table_ref, s):
        def body(j, carry):
            for r in range(TOP_K):
                d = table_ref[r, j]
                pltpu.make_async_copy(ys_ref.at[d], buf_ref.at[s, r, j], sems.at[s]).start(priority=r % 2)
            return carry

        lax.fori_loop(0, tm, body, 0, unroll=8)

    @pl.when(i == 0)
    def _():
        gather(dest_ref, 0)

    @pl.when(i + 1 < pl.num_programs(0))
    def _():
        gather(dest_next_ref, 1 - slot)

    for r in range(TOP_K):
        pltpu.make_async_copy(ys_ref.at[pl.ds(0, tm)], buf_ref.at[slot, r], sems.at[slot]).wait()

    gates = gate_ref[...]
    lane = lax.broadcasted_iota(jnp.int32, gates.shape, 1)
    f = None
    for r in range(TOP_K):
        g = jnp.sum(jnp.where(lane == r, gates, 0.0), axis=-1, keepdims=True)
        row_ref[...] = buf_ref[slot, r].reshape(row_ref.shape)
        term = row_ref[...] * g
        f = term if f is None else f + term
    out = x1_ref[...] + mod_ref[0, 5:6, :] * _rms(f, g_ref[...])

    @pl.when(pl.program_id(0) < first_tiles)
    def _():
        oa_ref[...] = out

    @pl.when(pl.program_id(0) >= first_tiles)
    def _():
        ob_ref[...] = out


def _combine(dest, gates, x1, mod3, g_post, ys, seq, n_first):
    t, d = x1.shape
    tm = TM_DMA
    tps = seq // tm
    first_tiles = n_first * tps
    n_tiles = t // tm
    return pl.pallas_call(
        functools.partial(_combine_kernel, first_tiles=first_tiles),
        grid=(n_tiles,),
        in_specs=[
            pl.BlockSpec((8, tm), lambda i: (0, i), memory_space=pltpu.SMEM),
            pl.BlockSpec((8, tm), lambda i: (0, jnp.minimum(i + 1, n_tiles - 1)), memory_space=pltpu.SMEM),
            pl.BlockSpec((tm, LANE), lambda i: (i, 0)),
            pl.BlockSpec((tm, d), lambda i: (i, 0)),
            pl.BlockSpec((1, N_MOD, d), lambda i: (i // tps, 0, 0)),
            pl.BlockSpec((1, d), lambda i: (0, 0)),
            pl.BlockSpec(memory_space=pl.ANY),
        ],
        out_specs=(pl.BlockSpec((tm, d), lambda i: (jnp.minimum(i, first_tiles - 1), 0)),
                   pl.BlockSpec((tm, d), lambda i: (jnp.maximum(i - first_tiles, 0), 0))),
        out_shape=(jax.ShapeDtypeStruct((first_tiles * tm, d), F32),
                   jax.ShapeDtypeStruct((t - first_tiles * tm, d), F32)),
        scratch_shapes=[pltpu.VMEM((2, TOP_K, tm, 1, d), F32), pltpu.VMEM((tm, d), F32),
                        pltpu.SemaphoreType.DMA((2,))],
        compiler_params=_params(("arbitrary",)),
        name="expert_combine",
    )(dest, dest, gates, x1, mod3, g_post.reshape(1, d), ys)


def _w_in_plan(d, qr, kvr):
    qsw = SWA_Q_HEADS * SWA_HEAD_DIM
    ksw = SWA_KV_HEADS * SWA_HEAD_DIM
    names = ("c_q", "c_kv", "k_rope", "q_s", "k_s", "v_s", "g_a", "g_b")
    sizes = (qr, kvr, MLA_ROPE, qsw, ksw, ksw, d, d)
    src = {}
    off = 0
    for name, s in zip(names, sizes):
        src[name] = (off, s)
        off += s
    order = ("q_s", "g_a", "g_b", "c_q", "c_kv", "k_s", "v_s", "k_rope", "k_rope")
    plan, cols, dst = [], {}, 0
    for name in order:
        s_off, width = src[name]
        cols.setdefault(name, dst)
        plan.append((s_off, width, dst))
        dst += width
    return tuple(plan), cols, dst


def _layout_w_uq(w_uq):
    r = w_uq.shape[0]
    w = w_uq.reshape(r, MLA_HEADS, MLA_NOPE + MLA_ROPE)
    nope = w[:, :, :MLA_NOPE].reshape(r, MLA_HEADS * MLA_NOPE)
    pe = w[:, :, MLA_NOPE:].reshape(r, MLA_HEADS * MLA_ROPE)
    return jnp.concatenate([nope, pe], axis=1).astype(BF16)


def _layout_w_ukv(w_ukv):
    r = w_ukv.shape[0]
    w = w_ukv.reshape(r, MLA_HEADS, MLA_NOPE + MLA_V)
    kn = w[:, :, :MLA_NOPE].reshape(r, MLA_HEADS * MLA_NOPE)
    v = w[:, :, MLA_NOPE:].reshape(r, MLA_HEADS * MLA_V)
    return jnp.concatenate([kn, v], axis=1).astype(BF16)


def _rope_tables(seq):
    half = MLA_ROPE // 2
    freqs = jnp.power(ROPE_THETA, -2.0 * jnp.arange(half, dtype=F32) / MLA_ROPE)
    ang = jnp.arange(seq, dtype=F32)[:, None] * freqs[None, :]
    cos, sin = jnp.cos(ang), jnp.sin(ang)
    return (jnp.concatenate([cos, cos, cos, cos], axis=1),
            jnp.concatenate([-sin, sin, -sin, sin], axis=1))


def kernel(x_prompt, x_sample, c_prompt, c_sample, w_ada, b_ada, g_pre_mix, w_in, g_q_lat, w_uq,
           g_kv_lat, w_ukv, attn_sinks, w_br_mla, w_br_swa, w_out, g_post_mix, g_pre_ffn,
           w_router, b_router, w_gu, b_gu, w_dn, b_dn, g_post_ffn):
    assert MLA_ROPE == SWA_HEAD_DIM == 64 and MLA_NOPE == MLA_V == LANE
    nb_p, seq, d = x_prompt.shape
    nb_s = x_sample.shape[0]
    assert x_sample.shape[1] == seq
    nseq = nb_p + nb_s
    t = nseq * seq
    n_experts = w_router.shape[-1]
    n_blocks = t * TOP_K // BM_EXP + n_experts

    xa, xb = x_prompt.reshape(nb_p * seq, d), x_sample.reshape(nb_s * seq, d)
    c8 = jnp.concatenate([c_prompt, c_sample, jnp.zeros((8 - nseq, d), F32)], axis=0)
    cos, sin = _rope_tables(seq)

    for l in range(w_ada.shape[0]):
        plan, cols, n_cols = _w_in_plan(d, g_q_lat.shape[-1], g_kv_lat.shape[-1])
        w_uq_p = _layout_w_uq(w_uq[l])
        w_ukv_p = _layout_w_ukv(w_ukv[l])
        w_router_p = jnp.pad(w_router[l], ((0, 0), (0, LANE - n_experts))).astype(BF16)
        b_router_p = jnp.pad(b_router[l], (0, LANE - n_experts)).reshape(1, LANE)

        mod, w_in_p = _modulation(c8, w_ada[l], b_ada[l], w_in[l], plan, n_cols)
        mod3 = mod[:nseq].reshape(nseq, N_MOD, d)
        z = _in_projection(xa, xb, mod3, g_pre_mix[l], w_in_p, seq)
        q, k, v, qs, klo, khi, vlo, vhi = _post_projection(
            z, cols, cos, sin, g_q_lat[l], g_kv_lat[l], w_uq_p, w_ukv_p, nseq, seq)
        o_a, w_gu_b, w_dn_b = _mla_attention(q, k, v, w_gu[l], w_dn[l])
        o_b = _swa_attention(attn_sinks[l], qs, klo, khi, vlo, vhi, nseq, seq)
        y = _merge(o_a, o_b, z, cols, w_br_mla[l].astype(BF16), w_br_swa[l].astype(BF16))
        x1, h2, idx, gates, cnt = _out_projection(
            y, xa, xb, mod3, g_post_mix[l], g_pre_ffn[l], w_out[l].astype(BF16), w_router_p, b_router_p,
            n_experts, seq)
        dest, meta, zst = _positions(idx, cnt, n_experts, n_blocks)
        sp = jnp.concatenate([meta[:n_blocks, 0], meta[0:1, 1], meta[:n_blocks, 2]])
        zmeta = jnp.concatenate([zst[0, :n_experts], meta[0:1, 1]])
        xs = _dispatch(zmeta, dest, h2, n_experts, n_blocks)
        ys = _experts(sp, xs, w_gu_b, b_gu[l], w_dn_b, b_dn[l], n_blocks)
        xa, xb = _combine(dest, gates, x1, mod3, g_post_ffn[l], ys, seq, nb_p)

    return (xa.reshape(nb_p, seq, d), xb.reshape(nb_s, seq, d))
```
